```python
import math
import jax
import jax.numpy as jnp
from jax import lax
import numpy as np

D_MODEL = 2048
BATCH = 4
SEQ = 2048
DEPTH = 4
DEC_BATCH = 8
DEC_SEQ = 1
PAST_LEN = 16384
PAGE_SIZE = 128

RMS_EPS = 1e-6
NEG_INF = -1e30

H_A = 8
DH_A = 128
C_A = H_A * DH_A
MOBA_BLOCK = 256
MOBA_TOPK = 3
Q_CHUNK = 32
NUM_BUCKETS = 32
MAX_DISTANCE = 128
H_B = 16
DH_B = 64
C_B = H_B * DH_B
LORA_W = 64
LORA_A = 64
LORA_G = 160
LORA_V = 32
LNX_EPS = 64e-5
HK_C = 16
HV_C = 32
DK_C = 128
DV_C = 128
QK_C = HK_C * DK_C
V_C = HV_C * DV_C
CONV_W = 4
CONV_CH = 2 * QK_C + V_C
GDN_CHUNK = 64
PEER_HEADS = 8
N_KEYS = 128
N_EXPERTS = N_KEYS * N_KEYS
PEER_DK = 256
PEER_TOPK = 16
TOK_CHUNK = 128

kernel_name = 'hybrid_moba_rwkv7_gdn_peer_step'


def _rw_width(first):
    return 3 * C_B + LORA_W + LORA_A + LORA_G + (0 if first else LORA_V)


def _rmsnorm(x, g, eps=RMS_EPS):
    xf = x.astype(jnp.float32)
    y = xf * lax.rsqrt(jnp.mean(xf * xf, axis=-1, keepdims=True) + eps)
    return (y * g.astype(jnp.float32)).astype(x.dtype)


def _l2norm(x, eps):
    xf = x.astype(jnp.float32)
    return xf * lax.rsqrt(jnp.sum(xf * xf, axis=-1, keepdims=True) + eps)


def _t5_bucket(dist):
    n = jnp.maximum(dist, 0)
    max_exact = NUM_BUCKETS // 2
    ratio = jnp.log(jnp.maximum(n, 1).astype(jnp.float32) / max_exact) / math.log(MAX_DISTANCE / max_exact)
    large = max_exact + (ratio * (NUM_BUCKETS - max_exact)).astype(jnp.int32)
    return jnp.where(n < max_exact, n, jnp.minimum(large, NUM_BUCKETS - 1))


def _rel_bias(dist, table):
    hi = jnp.arange(H_A)[None, :, None, None]
    return table.astype(jnp.float32)[_t5_bucket(dist), hi]


def _masked_softmax(s, valid):
    return jax.nn.softmax(jnp.where(valid, s, NEG_INF), axis=-1)


def _select_blocks(q, kmean, qblk):
    n_sel = min(MOBA_TOPK, kmean.shape[2])
    gate = jnp.einsum('bhqd,bhjd->bhqj', q.astype(jnp.float32), kmean)
    past = jnp.arange(kmean.shape[2])[None, :] < qblk[:, None]
    _, idx = lax.top_k(jnp.where(past, gate, NEG_INF), n_sel)
    return idx, idx < qblk[:, None]


def _moba_prompt(q, k, v, rel_bias):
    b, s_len = q.shape[:2]
    nb = -(-s_len // MOBA_BLOCK)
    sp = nb * MOBA_BLOCK
    pad = ((0, 0), (0, sp - s_len), (0, 0), (0, 0))
    qh, kh, vh = (jnp.pad(t, pad).transpose(0, 2, 1, 3) for t in (q, k, v))
    kb = kh.reshape(b, H_A, nb, MOBA_BLOCK, DH_A)
    vb = vh.reshape(b, H_A, nb, MOBA_BLOCK, DH_A)
    kmean = jnp.mean(kb.astype(jnp.float32), axis=3)
    offs = jnp.arange(MOBA_BLOCK)
    bi = jnp.arange(b)[:, None, None, None]
    hi = jnp.arange(H_A)[None, :, None, None]
    scale = DH_A ** -0.5

    def one_chunk(c):
        q0 = c * Q_CHUNK
        qc = lax.dynamic_slice_in_dim(qh, q0, Q_CHUNK, axis=2)
        qpos = q0 + jnp.arange(Q_CHUNK)
        idx, ok = _select_blocks(qc, kmean, qpos // MOBA_BLOCK)
        n_sel = idx.shape[-1]
        k_sel = kb[bi, hi, idx]
        v_sel = vb[bi, hi, idx]
        own = q0 // MOBA_BLOCK
        k_own = lax.dynamic_index_in_dim(kb, own, axis=2, keepdims=False)
        v_own = lax.dynamic_index_in_dim(vb, own, axis=2, keepdims=False)
        s = jnp.concatenate([
            jnp.einsum('bhqd,bhqnkd->bhqnk', qc, k_sel).reshape(b, H_A, Q_CHUNK, n_sel * MOBA_BLOCK),
            jnp.einsum('bhqd,bhkd->bhqk', qc, k_own)], axis=-1).astype(jnp.float32) * scale
        pos_sel = (idx[..., None] * MOBA_BLOCK + offs).reshape(b, H_A, Q_CHUNK, n_sel * MOBA_BLOCK)
        pos_own = jnp.broadcast_to(own * MOBA_BLOCK + offs, (b, H_A, Q_CHUNK, MOBA_BLOCK))
        pos = jnp.concatenate([pos_sel, pos_own], axis=-1)
        ok_sel = jnp.broadcast_to(ok[..., None], ok.shape + (MOBA_BLOCK,)).reshape(b, H_A, Q_CHUNK, n_sel * MOBA_BLOCK)
        valid = jnp.concatenate([ok_sel, pos_own <= qpos[None, None, :, None]], axis=-1)
        s = s + _rel_bias(qpos[None, None, :, None] - pos, rel_bias)
        p = _masked_softmax(s, valid).astype(v.dtype)
        p_sel = p[..., :n_sel * MOBA_BLOCK].reshape(b, H_A, Q_CHUNK, n_sel, MOBA_BLOCK)
        return (jnp.einsum('bhqnk,bhqnkd->bhqd', p_sel, v_sel)
                + jnp.einsum('bhqk,bhkd->bhqd', p[..., n_sel * MOBA_BLOCK:], v_own))

    o = lax.map(one_chunk, jnp.arange(sp // Q_CHUNK))
    o = o.transpose(1, 0, 3, 2, 4).reshape(b, sp, H_A, DH_A)
    return o[:, :s_len]


def _paged_rows(cache, page_table, new, pos, bi, hi):
    past = page_table.shape[1] * PAGE_SIZE
    pp = jnp.clip(pos, 0, past - 1)
    rows_past = cache[page_table[bi, pp // PAGE_SIZE], pp % PAGE_SIZE, hi]
    rows_new = new[bi, jnp.clip(pos - past, 0, new.shape[1] - 1), hi]
    return jnp.where((pos < past)[..., None], rows_past, rows_new.astype(rows_past.dtype))


def _moba_sample(q, k, v, cache_k, cache_v, page_table, rel_bias):
    bd, tn = q.shape[:2]
    past = page_table.shape[1] * PAGE_SIZE
    total = past + tn
    nb = -(-total // MOBA_BLOCK)
    lp = nb * MOBA_BLOCK
    k_past = cache_k[page_table].reshape(bd, past, H_A, DH_A)
    k_all = jnp.concatenate([k_past, k.astype(k_past.dtype)], axis=1)
    k_all = jnp.pad(k_all, ((0, 0), (0, lp - total), (0, 0), (0, 0)))
    kmean = jnp.mean(k_all.reshape(bd, nb, MOBA_BLOCK, H_A, DH_A).astype(jnp.float32), axis=2).transpose(0, 2, 1, 3)
    qh = q.transpose(0, 2, 1, 3)
    qpos = past + jnp.arange(tn)
    qblk = qpos // MOBA_BLOCK
    idx, ok = _select_blocks(qh, kmean, qblk)
    n_sel = idx.shape[-1]
    offs = jnp.arange(MOBA_BLOCK)
    pos_sel = idx[..., None] * MOBA_BLOCK + offs
    pos_own = jnp.broadcast_to((qblk * MOBA_BLOCK)[:, None, None] + offs, (bd, H_A, tn, 1, MOBA_BLOCK))
    n_rows = (n_sel + 1) * MOBA_BLOCK
    pos = jnp.concatenate([pos_sel, pos_own], axis=3).reshape(bd, H_A, tn, n_rows)
    valid = jnp.concatenate([jnp.broadcast_to(ok[..., None], pos_sel.shape),
                             pos_own <= qpos[:, None, None]], axis=3).reshape(bd, H_A, tn, n_rows)
    bi = jnp.arange(bd)[:, None, None, None]
    hi = jnp.arange(H_A)[None, :, None, None]
    k_rows = k_all[bi, pos, hi]
    v_rows = _paged_rows(cache_v, page_table, v, pos, bi, hi)
    s = jnp.einsum('bhqd,bhqnd->bhqn', qh, k_rows.astype(qh.dtype)).astype(jnp.float32) * DH_A ** -0.5
    s = s + _rel_bias(qpos[None, None, :, None] - pos, rel_bias)
    p = _masked_softmax(s, valid).astype(v_rows.dtype)
    o = jnp.einsum('bhqn,bhqnd->bhqd', p, v_rows)
    return o.transpose(0, 2, 1, 3)


def _rwkv7(p_cur, p_prev, s0, v_first, prm, first):
    b, t, _ = p_cur.shape
    f32 = jnp.float32
    m = (p_cur + (p_prev - p_cur) * prm['mu']).astype(f32)
    r, k, v = m[..., :C_B], m[..., C_B:2 * C_B], m[..., 2 * C_B:3 * C_B]
    o = 3 * C_B
    wl = m[..., o:o + LORA_W]
    o += LORA_W
    al = m[..., o:o + LORA_A]
    o += LORA_A
    gl = m[..., o:o + LORA_G]
    o += LORA_G
    w_log = -jax.nn.softplus(-(prm['w0'] + jnp.tanh(wl) @ prm['w2'])) - 0.5
    decay = jnp.exp(-jnp.exp(w_log.astype(f32)))
    a = jax.nn.sigmoid((prm['a0'] + al @ prm['a2']).astype(f32))
    g = (jax.nn.sigmoid(gl) @ prm['g2']).astype(f32)
    if first:
        v_first = v
    else:
        vl = m[..., o:o + LORA_V]
        v = v + (v_first - v) * jax.nn.sigmoid((prm['v0'] + vl @ prm['v2']).astype(f32))

    def hd(z):
        return z.reshape(b, t, H_B, DH_B)

    kk = _l2norm(hd(k * prm['kk_scale']), 1e-24)
    k = k * (1.0 + (a - 1.0) * prm['ka_mix'])
    r_h, k_h, v_h = hd(r), hd(k), hd(v)
    b_h = kk * hd(a)
    xs = tuple(jnp.moveaxis(z.astype(f32), 1, 0) for z in (r_h, hd(decay), k_h, v_h, kk, b_h))

    def step(s, xt):
        rt, wt, kt, vt, kkt, bt = xt
        sa = -jnp.einsum('bhij,bhj->bhi', s, kkt)
        s = s * wt[:, :, None, :] + sa[..., None] * bt[:, :, None, :] + vt[..., None] * kt[:, :, None, :]
        return s, jnp.einsum('bhij,bhj->bhi', s, rt)

    s_fin, y = lax.scan(step, s0.astype(f32), xs)
    y = jnp.moveaxis(y, 0, 1)
    mu = jnp.mean(y, axis=-1, keepdims=True)
    var = jnp.mean(jnp.square(y - mu), axis=-1, keepdims=True)
    y = ((y - mu) * lax.rsqrt(var + LNX_EPS)).reshape(b, t, C_B) * prm['lnx_w'] + prm['lnx_b']
    bonus = jnp.sum(r_h * k_h * prm['rk_bonus'], axis=-1, keepdims=True) * v_h
    y = (y + bonus.reshape(b, t, C_B)) * g
    return y, s_fin.astype(s0.dtype), v_first


def _even_mixer(h, h_prev, wkv0, v_first, attend, prm, first):
    b, t, _ = h.shape
    p = h @ prm['w_in']
    q = p[..., :C_A].reshape(b, t, H_A, DH_A)
    k = p[..., C_A:2 * C_A].reshape(b, t, H_A, DH_A)
    v = p[..., 2 * C_A:3 * C_A].reshape(b, t, H_A, DH_A)
    p_rw = p[..., 3 * C_A:]
    prev0 = h_prev.astype(h.dtype) @ prm['w_in'][:, 3 * C_A:]
    p_prev = jnp.concatenate([prev0[:, None], p_rw[:, :-1]], axis=1)
    o_a = attend(q, k, v).reshape(b, t, C_A)
    o_b, wkv, v_first = _rwkv7(p_rw, p_prev, wkv0, v_first, prm, first)
    y = jnp.concatenate([o_a.astype(h.dtype), o_b.astype(h.dtype)], axis=-1) @ prm['w_out']
    return y, k, v, wkv, h[:, -1], v_first


def _gated_delta_chunked(q, k, v, g, beta, s0):
    b, t, h, dk = q.shape
    dv = v.shape[-1]
    c = GDN_CHUNK
    n = -(-t // c)
    tp = n * c - t

    def prep(z):
        z = jnp.pad(z, ((0, 0), (0, tp)) + ((0, 0),) * (z.ndim - 2))
        z = jnp.moveaxis(z, 2, 1)
        return z.reshape((b, h, n, c) + z.shape[3:])

    q, k, v, g, beta = (prep(z) for z in (q * dk ** -0.5, k, v, g, beta))
    gc = jnp.cumsum(g, axis=-1)
    tril = jnp.tril(jnp.ones((c, c), bool))
    strict = jnp.tril(jnp.ones((c, c), bool), -1)
    diff = gc[..., :, None] - gc[..., None, :]
    decay = jnp.where(tril, jnp.exp(jnp.where(tril, diff, 0.0)), 0.0)
    kb = k * beta[..., None]
    lower = jnp.where(strict, jnp.einsum('bhnid,bhnjd->bhnij', kb, k) * decay, 0.0)
    a_mat = lower + jnp.eye(c, dtype=jnp.float32)
    rhs = jnp.concatenate([v * beta[..., None], kb * jnp.exp(gc)[..., None]], axis=-1)
    sol = lax.linalg.triangular_solve(a_mat, rhs, left_side=True, lower=True, unit_diagonal=True)
    u_in, w_cum = sol[..., :dv], sol[..., dv:]
    attn = jnp.einsum('bhnid,bhnjd->bhnij', q, k) * decay
    q_dec = q * jnp.exp(gc)[..., None]
    k_tail = k * jnp.exp(gc[..., -1:] - gc)[..., None]
    g_tot = jnp.exp(gc[..., -1])

    def step(s, xs):
        u_c, w_c, a_c, qd_c, kt_c, gt_c = xs
        u = u_c - jnp.einsum('bhck,bhkv->bhcv', w_c, s)
        o = jnp.einsum('bhck,bhkv->bhcv', qd_c, s) + jnp.einsum('bhcj,bhjv->bhcv', a_c, u)
        s = s * gt_c[..., None, None] + jnp.einsum('bhck,bhcv->bhkv', kt_c, u)
        return s, o

    xs = tuple(jnp.moveaxis(z, 2, 0) for z in (u_in, w_cum, attn, q_dec, k_tail, g_tot))
    s_fin, o = lax.scan(step, s0, xs)
    o = jnp.moveaxis(o, 0, 2).reshape(b, h, n * c, dv)[:, :, :t]
    return jnp.moveaxis(o, 1, 2), s_fin


def _gdn_mixer(h, conv_buf, s0, prm):
    b, t, _ = h.shape
    f32 = jnp.float32
    p = h @ prm['w_in']
    xqkv = p[..., :CONV_CH]
    z = p[..., CONV_CH:CONV_CH + V_C].reshape(b, t, HV_C, DV_C)
    beta_raw = p[..., CONV_CH + V_C:CONV_CH + V_C + HV_C]
    a_raw = p[..., CONV_CH + V_C + HV_C:]
    xc = jnp.concatenate([conv_buf.astype(xqkv.dtype), xqkv], axis=1)
    conv = lax.conv_general_dilated(xc, prm['conv_w'][:, None, :].astype(xc.dtype), window_strides=(1,),
                                    padding='VALID', dimension_numbers=('NWC', 'WIO', 'NWC'),
                                    feature_group_count=CONV_CH)
    qkv = jax.nn.silu(conv)
    rep = HV_C // HK_C
    q = jnp.repeat(_l2norm(qkv[..., :QK_C].reshape(b, t, HK_C, DK_C), 1e-6), rep, axis=2)
    k = jnp.repeat(_l2norm(qkv[..., QK_C:2 * QK_C].reshape(b, t, HK_C, DK_C), 1e-6), rep, axis=2)
    v = qkv[..., 2 * QK_C:].reshape(b, t, HV_C, DV_C).astype(f32)
    beta = jax.nn.sigmoid(beta_raw.astype(f32))
    g = -jnp.exp(prm['a_log'].astype(f32)) * jax.nn.softplus(a_raw.astype(f32) + prm['dt_bias'].astype(f32))
    o, s_fin = _gated_delta_chunked(q, k, v, g, beta, s0.astype(f32))
    o = _rmsnorm(o, prm['onorm_w']) * jax.nn.silu(z.astype(f32))
    y = o.astype(h.dtype).reshape(b, t, V_C) @ prm['w_out']
    return y, xc[:, -(CONV_W - 1):], s_fin.astype(s0.dtype)


def _peer(h, wq, sub_keys, u_tab, v_tab):
    b, t, d = h.shape
    n_tok = b * t
    tc = min(TOK_CHUNK, n_tok)
    nc = -(-n_tok // tc)
    xf = jnp.pad(h.reshape(n_tok, d), ((0, nc * tc - n_tok), (0, 0))).reshape(nc, tc, d)
    kf = sub_keys.astype(jnp.float32)

    def one_chunk(xc):
        q = (xc @ wq).astype(jnp.float32).reshape(tc, PEER_HEADS, 2, PEER_DK // 2)
        s = jnp.einsum('thpd,pnd->thpn', q, kf)
        sv, si = lax.top_k(s, PEER_TOPK)
        cand = (sv[:, :, 0, :, None] + sv[:, :, 1, None, :]).reshape(tc, PEER_HEADS, PEER_TOPK * PEER_TOPK)
        cv, ci = lax.top_k(cand, PEER_TOPK)
        i1 = jnp.take_along_axis(si[:, :, 0], ci // PEER_TOPK, axis=-1)
        i2 = jnp.take_along_axis(si[:, :, 1], ci % PEER_TOPK, axis=-1)
        eid = i1 * N_KEYS + i2
        gate = jax.nn.softmax(cv, axis=-1)
        act = jax.nn.gelu(jnp.einsum('td,thkd->thk', xc, u_tab[eid]).astype(jnp.float32), approximate=False)
        return jnp.einsum('thk,thkd->td', (gate * act).astype(xc.dtype), v_tab[eid])

    y = lax.map(one_chunk, xf).reshape(nc * tc, d)[:n_tok]
    return y.reshape(b, t, d)


def setup_inputs(seed: int = 0) -> dict:
    key = jax.random.key(seed)
    keys = iter(jax.random.split(key, 160))
    f32 = jnp.float32

    def nrm(shape, scale):
        return jax.random.normal(next(keys), shape, f32) * scale

    def gain(n):
        return 1.0 + nrm((n,), 0.02)

    n_pages = PAST_LEN // PAGE_SIZE
    n_used = DEC_BATCH * n_pages
    n_pool = n_used + n_used // 4
    inp = {}
    inp['x_prompt'] = nrm((BATCH, SEQ, D_MODEL), 1.0)
    inp['x_sample'] = nrm((DEC_BATCH, DEC_SEQ, D_MODEL), 1.0)
    for i in range(DEPTH):
        if i % 2 == 0:
            inp[f'cache_k_{i}'] = nrm((n_pool, PAGE_SIZE, H_A, DH_A), 1.0)
            inp[f'cache_v_{i}'] = nrm((n_pool, PAGE_SIZE, H_A, DH_A), 1.0)
            inp[f'state_wkv_{i}'] = nrm((DEC_BATCH, H_B, DH_B, DH_B), 0.1)
            inp[f'state_shift_{i}'] = nrm((DEC_BATCH, D_MODEL), 1.0)
        else:
            inp[f'state_gdn_{i}'] = nrm((DEC_BATCH, HV_C, DK_C, DV_C), 0.1)
            inp[f'state_conv_{i}'] = nrm((DEC_BATCH, CONV_W - 1, CONV_CH), 1.0)
    perm = jax.random.permutation(next(keys), n_pool)[:n_used]
    inp['page_table'] = perm.reshape(DEC_BATCH, n_pages).astype(jnp.int32)
    inp['rel_bias'] = nrm((NUM_BUCKETS, H_A), 0.5)
    for i in range(DEPTH):
        inp[f'norm_mix_{i}'] = gain(D_MODEL)
        if i % 2 == 0:
            first = i == 0
            rw = _rw_width(first)
            inp[f'w_in_{i}'] = nrm((D_MODEL, 3 * C_A + rw), D_MODEL ** -0.5)
            inp[f'mu_{i}'] = jax.random.uniform(next(keys), (rw,), f32)
            inp[f'w0_{i}'] = nrm((C_B,), 0.5)
            inp[f'w2_{i}'] = nrm((LORA_W, C_B), 0.1)
            inp[f'a0_{i}'] = nrm((C_B,), 0.1)
            inp[f'a2_{i}'] = nrm((LORA_A, C_B), 0.1)
            if not first:
                inp[f'v0_{i}'] = nrm((C_B,), 0.1)
                inp[f'v2_{i}'] = nrm((LORA_V, C_B), 0.1)
            inp[f'g2_{i}'] = nrm((LORA_G, C_B), LORA_G ** -0.5)
            inp[f'kk_scale_{i}'] = 0.85 + nrm((C_B,), 0.02)
            inp[f'ka_mix_{i}'] = 1.0 + nrm((C_B,), 0.02)
            inp[f'rk_bonus_{i}'] = nrm((H_B, DH_B), 0.1)
            inp[f'lnx_w_{i}'] = gain(C_B)
            inp[f'lnx_b_{i}'] = nrm((C_B,), 0.01)
            inp[f'w_out_{i}'] = nrm((C_A + C_B, D_MODEL), (C_A + C_B) ** -0.5)
        else:
            inp[f'w_in_{i}'] = nrm((D_MODEL, CONV_CH + V_C + 2 * HV_C), D_MODEL ** -0.5)
            inp[f'conv_w_{i}'] = nrm((CONV_W, CONV_CH), CONV_W ** -0.5)
            inp[f'a_log_{i}'] = jnp.log(jax.random.uniform(next(keys), (HV_C,), f32, 1.0, 16.0))
            inp[f'dt_bias_{i}'] = nrm((HV_C,), 0.1)
            inp[f'onorm_w_{i}'] = gain(DV_C)
            inp[f'w_out_{i}'] = nrm((V_C, D_MODEL), V_C ** -0.5)
        inp[f'norm_ffn_{i}'] = gain(D_MODEL)
        inp[f'peer_wq_{i}'] = nrm((D_MODEL, PEER_HEADS * PEER_DK), D_MODEL ** -0.5)
        inp[f'peer_keys_{i}'] = nrm((2, N_KEYS, PEER_DK // 2), (PEER_DK // 2) ** -0.5)
        inp[f'peer_u_{i}'] = nrm((N_EXPERTS, D_MODEL), D_MODEL ** -0.5)
        inp[f'peer_v_{i}'] = nrm((N_EXPERTS, D_MODEL), 0.1)
    inp['norm_final'] = gain(D_MODEL)
    return inp


def reference(x_prompt, x_sample,
              cache_k_0, cache_v_0, state_wkv_0, state_shift_0,
              state_gdn_1, state_conv_1,
              cache_k_2, cache_v_2, state_wkv_2, state_shift_2,
              state_gdn_3, state_conv_3,
              page_table, rel_bias,
              norm_mix_0, w_in_0, mu_0, w0_0, w2_0, a0_0, a2_0, g2_0, kk_scale_0, ka_mix_0, rk_bonus_0,
              lnx_w_0, lnx_b_0, w_out_0,
              norm_ffn_0, peer_wq_0, peer_keys_0, peer_u_0, peer_v_0,
              norm_mix_1, w_in_1, conv_w_1, a_log_1, dt_bias_1, onorm_w_1, w_out_1,
              norm_ffn_1, peer_wq_1, peer_keys_1, peer_u_1, peer_v_1,
              norm_mix_2, w_in_2, mu_2, w0_2, w2_2, a0_2, a2_2, v0_2, v2_2, g2_2, kk_scale_2, ka_mix_2,
              rk_bonus_2, lnx_w_2, lnx_b_2, w_out_2,
              norm_ffn_2, peer_wq_2, peer_keys_2, peer_u_2, peer_v_2,
              norm_mix_3, w_in_3, conv_w_3, a_log_3, dt_bias_3, onorm_w_3, w_out_3,
              norm_ffn_3, peer_wq_3, peer_keys_3, peer_u_3, peer_v_3,
              norm_final):
    mix_prm = [
        dict(norm=norm_mix_0, w_in=w_in_0, mu=mu_0, w0=w0_0, w2=w2_0, a0=a0_0, a2=a2_0, g2=g2_0,
             kk_scale=kk_scale_0, ka_mix=ka_mix_0, rk_bonus=rk_bonus_0, lnx_w=lnx_w_0, lnx_b=lnx_b_0,
             w_out=w_out_0),
        dict(norm=norm_mix_1, w_in=w_in_1, conv_w=conv_w_1, a_log=a_log_1, dt_bias=dt_bias_1,
             onorm_w=onorm_w_1, w_out=w_out_1),
        dict(norm=norm_mix_2, w_in=w_in_2, mu=mu_2, w0=w0_2, w2=w2_2, a0=a0_2, a2=a2_2, v0=v0_2, v2=v2_2,
             g2=g2_2, kk_scale=kk_scale_2, ka_mix=ka_mix_2, rk_bonus=rk_bonus_2, lnx_w=lnx_w_2,
             lnx_b=lnx_b_2, w_out=w_out_2),
        dict(norm=norm_mix_3, w_in=w_in_3, conv_w=conv_w_3, a_log=a_log_3, dt_bias=dt_bias_3,
             onorm_w=onorm_w_3, w_out=w_out_3),
    ]
    ffn_prm = [
        (norm_ffn_0, peer_wq_0, peer_keys_0, peer_u_0, peer_v_0),
        (norm_ffn_1, peer_wq_1, peer_keys_1, peer_u_1, peer_v_1),
        (norm_ffn_2, peer_wq_2, peer_keys_2, peer_u_2, peer_v_2),
        (norm_ffn_3, peer_wq_3, peer_keys_3, peer_u_3, peer_v_3),
    ]
    layer_state = [
        (cache_k_0, cache_v_0, state_wkv_0, state_shift_0),
        (state_gdn_1, state_conv_1),
        (cache_k_2, cache_v_2, state_wkv_2, state_shift_2),
        (state_gdn_3, state_conv_3),
    ]

    def attend_prompt(q, k, v):
        return _moba_prompt(q, k, v, rel_bias)

    bp = x_prompt.shape[0]
    dt = x_prompt.dtype
    xp, xs = x_prompt, x_sample
    vf_p = None
    vf_s = None
    new = []
    for i in range(DEPTH):
        prm = mix_prm[i]
        hp = _rmsnorm(xp, prm['norm'])
        hs = _rmsnorm(xs, prm['norm'])
        if i % 2 == 0:
            ck, cv, wkv_in, shift_in = layer_state[i]
            first = i == 0

            def attend_sample(q, k, v, ck=ck, cv=cv):
                return _moba_sample(q, k, v, ck, cv, page_table, rel_bias)

            yp, kp, vp, wkv_p, sh_p, vf_p = _even_mixer(
                hp, jnp.zeros((bp, D_MODEL), dt), jnp.zeros((bp, H_B, DH_B, DH_B), dt), vf_p,
                attend_prompt, prm, first)
            ys, ks, vs, wkv_s, sh_s, vf_s = _even_mixer(hs, shift_in, wkv_in, vf_s, attend_sample, prm, first)
            page_shape = (bp, kp.shape[1] // PAGE_SIZE, PAGE_SIZE, H_A, DH_A)
            new.append((kp.reshape(page_shape), vp.reshape(page_shape), ks, vs, wkv_p, wkv_s, sh_p, sh_s))
        else:
            gdn_in, conv_in = layer_state[i]
            yp, conv_p, gdn_p = _gdn_mixer(hp, jnp.zeros((bp, CONV_W - 1, CONV_CH), dt),
                                           jnp.zeros((bp, HV_C, DK_C, DV_C), dt), prm)
            ys, conv_s, gdn_s = _gdn_mixer(hs, conv_in, gdn_in, prm)
            new.append((gdn_p, gdn_s, conv_p, conv_s))
        xp = xp + yp.astype(xp.dtype)
        xs = xs + ys.astype(xs.dtype)
        g_ffn, wq, sub_keys, u_tab, v_tab = ffn_prm[i]
        xp = xp + _peer(_rmsnorm(xp, g_ffn), wq, sub_keys, u_tab, v_tab)
        xs = xs + _peer(_rmsnorm(xs, g_ffn), wq, sub_keys, u_tab, v_tab)
    y_prompt = _rmsnorm(xp, norm_final)
    y_sample = _rmsnorm(xs, norm_final)
    k_pages_p_0, v_pages_p_0, k_rows_s_0, v_rows_s_0, wkv_p_0, wkv_s_0, shift_p_0, shift_s_0 = new[0]
    gdn_p_1, gdn_s_1, conv_p_1, conv_s_1 = new[1]
    k_pages_p_2, v_pages_p_2, k_rows_s_2, v_rows_s_2, wkv_p_2, wkv_s_2, shift_p_2, shift_s_2 = new[2]
    gdn_p_3, gdn_s_3, conv_p_3, conv_s_3 = new[3]
    return (y_prompt, y_sample,
            k_pages_p_0, v_pages_p_0, k_rows_s_0, v_rows_s_0, wkv_p_0, wkv_s_0, shift_p_0, shift_s_0,
            gdn_p_1, gdn_s_1, conv_p_1, conv_s_1,
            k_pages_p_2, v_pages_p_2, k_rows_s_2, v_rows_s_2, wkv_p_2, wkv_s_2, shift_p_2, shift_s_2,
            gdn_p_3, gdn_s_3, conv_p_3, conv_s_3)
```

```python
import functools
import math

import jax
import jax.numpy as jnp
from jax import lax
from jax.experimental import pallas as pl
from jax.experimental.pallas import tpu as pltpu

f32 = jnp.float32
bf16 = jnp.bfloat16
i32 = jnp.int32

RMS_EPS = 1e-6
NEG_INF = -1e30

PAGE_SIZE = 128
H_A, DH_A = 8, 128
C_A = H_A * DH_A
MOBA_BLOCK, MOBA_TOPK = 256, 3
NUM_BUCKETS, MAX_DISTANCE = 32, 128
H_B, DH_B = 16, 64
C_B = H_B * DH_B
LORA_W, LORA_A, LORA_G, LORA_V = 64, 64, 160, 32
LNX_EPS = 64e-5
HK_C, HV_C, DK_C, DV_C = 16, 32, 128, 128
QK_C, V_C = HK_C * DK_C, HV_C * DV_C
CONV_W = 4
CONV_CH = 2 * QK_C + V_C
GDN_CHUNK = 64
PEER_HEADS, N_KEYS, PEER_DK, PEER_TOPK = 8, 128, 256, 16
N_EXPERTS = N_KEYS * N_KEYS

LANES = 128
SUBLANES = 8
VMEM_LIMIT = 56 * 1024 * 1024


def _cparams(*sem):
    return pltpu.CompilerParams(dimension_semantics=sem, vmem_limit_bytes=VMEM_LIMIT)


def _rms(x, g):
    return x * lax.rsqrt(jnp.mean(x * x, axis=-1, keepdims=True) + RMS_EPS) * g


def _mm_body(*refs, norm, has_res):
    if has_res:
        x_ref, g_ref, w_ref, res_ref, o_ref, h_ref = refs
    else:
        x_ref, g_ref, w_ref, o_ref, h_ref = refs

    @pl.when(pl.program_id(1) == 0)
    def _():
        x = x_ref[...]
        if norm:
            x = _rms(x, g_ref[...])
        h_ref[...] = x.astype(bf16)

    acc = jnp.dot(h_ref[...], w_ref[...].astype(bf16), preferred_element_type=f32)
    if has_res:
        acc = res_ref[...] + acc
    o_ref[...] = acc


def _matmul(x, w, gain=None, res=None, col0=0, ncols=None, tn=512):
    m, k = x.shape
    ncols = w.shape[1] - col0 if ncols is None else ncols
    assert col0 % tn == 0 and m % 16 == 0
    tm = min(m, 512)
    assert m % tm == 0
    nj = pl.cdiv(ncols, tn)
    j0 = col0 // tn
    norm = gain is not None
    g = (gain if norm else jnp.ones((k,), f32)).reshape(1, k)
    in_specs = [pl.BlockSpec((tm, k), lambda i, j: (i, 0)),
                pl.BlockSpec((1, k), lambda i, j: (0, 0)),
                pl.BlockSpec((k, tn), lambda i, j: (0, j + j0))]
    args = [x, g, w]
    if res is not None:
        in_specs.append(pl.BlockSpec((tm, tn), lambda i, j: (i, j)))
        args.append(res)
    return pl.pallas_call(
        functools.partial(_mm_body, norm=norm, has_res=res is not None),
        grid=(m // tm, nj),
        in_specs=in_specs,
        out_specs=pl.BlockSpec((tm, tn), lambda i, j: (i, j)),
        out_shape=jax.ShapeDtypeStruct((m, ncols), f32),
        scratch_shapes=[pltpu.VMEM((tm, k), bf16)],
        compiler_params=_cparams("parallel", "arbitrary"),
        name="matmul",
    )(*args)


def _t5_bucket(dist):
    n = jnp.maximum(dist, 0)
    max_exact = NUM_BUCKETS // 2
    ratio = jnp.log(jnp.maximum(n, 1).astype(f32) / max_exact) / math.log(MAX_DISTANCE / max_exact)
    large = max_exact + (ratio * (NUM_BUCKETS - max_exact)).astype(i32)
    return jnp.where(n < max_exact, n, jnp.minimum(large, NUM_BUCKETS - 1))


def _moba_prompt_body(q_ref, k_ref, v_ref, bias_ref, o_ref, km_ref, *, nb):
    qb = pl.program_id(2)
    blk = MOBA_BLOCK
    s_len = nb * blk

    @pl.when(qb == 0)
    def _():
        k = k_ref[...]
        km = jnp.mean(k.reshape(nb, blk, DH_A), axis=1)
        km_ref[...] = jnp.zeros_like(km_ref)
        km_ref[0:nb, :] = km

    q = q_ref[...]
    gate = lax.dot_general(q, km_ref[...], (((1,), (1,)), ((), ())), preferred_element_type=f32,
                           precision=lax.Precision.HIGHEST)
    lane = lax.broadcasted_iota(i32, gate.shape, 1)
    past = lane < qb
    g = jnp.where(past, gate, NEG_INF)
    sel = jnp.zeros(gate.shape, f32)
    for _ in range(MOBA_TOPK):
        m = jnp.max(g, axis=1, keepdims=True)
        idx = jnp.min(jnp.where(g == m, lane, LANES), axis=1, keepdims=True)
        pick = lane == idx
        sel = jnp.where(pick, jnp.where(past, 1.0, 0.0), sel)
        g = jnp.where(pick, -jnp.inf, g)
    sel = jnp.where(lane == qb, 1.0, sel)
    selneg = jnp.where(sel > 0.0, 0.0, NEG_INF).astype(bf16)
    blk_of_key = lax.broadcasted_iota(i32, (LANES, s_len), 1) // blk
    blockind = jnp.where(blk_of_key == lax.broadcasted_iota(i32, (LANES, s_len), 0), 1.0, 0.0).astype(bf16)
    maskadd = jnp.dot(selneg, blockind, preferred_element_type=f32)

    s = lax.dot_general(q.astype(bf16), k_ref[...].astype(bf16), (((1,), (1,)), ((), ())),
                        preferred_element_type=f32) * (DH_A ** -0.5)
    off = pl.multiple_of((nb - 1 - qb) * blk, blk)
    s = s + bias_ref[0, :, pl.ds(off, s_len)] + maskadd
    kidx = lax.broadcasted_iota(i32, s.shape, 1)
    qidx = qb * blk + lax.broadcasted_iota(i32, s.shape, 0)
    s = jnp.where(kidx <= qidx, s, NEG_INF)
    m = jnp.max(s, axis=1, keepdims=True)
    p = jnp.exp(s - m)
    l = jnp.sum(p, axis=1, keepdims=True)
    o = jnp.dot(p.astype(bf16), v_ref[...].astype(bf16), preferred_element_type=f32)
    o_ref[...] = o / l


def _moba_prompt(p, rel_bias, b, s_len):
    blk = MOBA_BLOCK
    assert s_len % blk == 0
    nb = s_len // blk
    r = jnp.arange(blk)[:, None]
    x = jnp.arange((2 * nb - 1) * blk)[None, :]
    strip = rel_bias.astype(f32)[_t5_bucket(r - x + (nb - 1) * blk)]
    strip = jnp.moveaxis(strip, 2, 0)
    hd = C_A // DH_A
    return pl.pallas_call(
        functools.partial(_moba_prompt_body, nb=nb),
        grid=(b, H_A, nb),
        in_specs=[pl.BlockSpec((blk, DH_A), lambda bi, h, qb: (bi * nb + qb, h)),
                  pl.BlockSpec((s_len, DH_A), lambda bi, h, qb: (bi, hd + h)),
                  pl.BlockSpec((s_len, DH_A), lambda bi, h, qb: (bi, 2 * hd + h)),
                  pl.BlockSpec((1, blk, (2 * nb - 1) * blk), lambda bi, h, qb: (h, 0, 0))],
        out_specs=pl.BlockSpec((blk, DH_A), lambda bi, h, qb: (bi * nb + qb, h)),
        out_shape=jax.ShapeDtypeStruct((b * s_len, C_A), f32),
        scratch_shapes=[pltpu.VMEM((LANES, DH_A), f32)],
        compiler_params=_cparams("parallel", "parallel", "arbitrary"),
        name="moba_prompt",
    )(p, p, p, strip)


def _wkv_body(r_ref, w_ref, k_ref, kk_ref, b_ref, v_ref, s0_ref, y_ref, sf_ref, s_ref, *, tc):
    @pl.when(pl.program_id(1) == 0)
    def _():
        s_ref[...] = s0_ref[0]

    half = LANES // 2

    def step(t, c):
        r, w, k, kk, b = r_ref[0, t], w_ref[0, t], k_ref[0, t], kk_ref[0, t], b_ref[0, t]
        for i in range(DH_B):
            s = s_ref[i]
            vi = v_ref[0, t, pl.ds(i, 1), :]
            t1 = jnp.sum(s * kk, axis=0, keepdims=True)
            sa = -(t1 + pltpu.roll(t1, half, 1))
            s = s * w + sa * b + vi * k
            s_ref[i] = s
            t2 = jnp.sum(s * r, axis=0, keepdims=True)
            y_ref[0, t, pl.ds(i, 1), :] = t2 + pltpu.roll(t2, half, 1)
        return c

    lax.fori_loop(0, tc, step, 0)

    @pl.when(pl.program_id(1) == pl.num_programs(1) - 1)
    def _():
        sf_ref[0] = s_ref[...]


def _wkv_scan(r, w, k, kk, b, v, s0):
    bsz, t, h, n = r.shape
    assert n == DH_B and (bsz * h) % (LANES // 2) == 0
    grp = bsz * h // (LANES // 2)
    bg = bsz // grp
    hn = n // 2

    def vec(z):
        z = z.reshape(grp, bg, t, h, 2, hn)
        return z.transpose(0, 2, 5, 4, 1, 3).reshape(grp, t, hn, LANES)

    def row(z):
        z = z.reshape(grp, bg, t, h, n).transpose(0, 2, 4, 1, 3).reshape(grp, t, n, LANES // 2)
        return jnp.concatenate([z, z], axis=-1)

    s0k = s0.astype(f32).reshape(grp, bg, h, n, 2, hn).transpose(0, 3, 5, 4, 1, 2).reshape(grp, n, hn, LANES)
    tc = math.gcd(t, 64)
    vspec = pl.BlockSpec((1, tc, hn, LANES), lambda g, c: (g, c, 0, 0))
    rspec = pl.BlockSpec((1, tc, n, LANES), lambda g, c: (g, c, 0, 0))
    sspec = pl.BlockSpec((1, n, hn, LANES), lambda g, c: (g, 0, 0, 0))
    y, sf = pl.pallas_call(
        functools.partial(_wkv_body, tc=tc),
        grid=(grp, t // tc),
        in_specs=[vspec] * 5 + [rspec, sspec],
        out_specs=[rspec, sspec],
        out_shape=[jax.ShapeDtypeStruct((grp, t, n, LANES), f32), jax.ShapeDtypeStruct((grp, n, hn, LANES), f32)],
        scratch_shapes=[pltpu.VMEM((n, hn, LANES), f32)],
        compiler_params=_cparams("parallel", "arbitrary"),
        name="wkv_scan",
    )(vec(r), vec(w), vec(k), vec(kk), vec(b), row(v), s0k)
    y = y[..., :LANES // 2].reshape(grp, t, n, bg, h).transpose(0, 3, 1, 4, 2).reshape(bsz, t, h, n)
    sf = sf.reshape(grp, n, hn, 2, bg, h).transpose(0, 4, 5, 1, 3, 2).reshape(bsz, h, n, n)
    return y, sf


def _top_rows(v, k):
    rows = lax.broadcasted_iota(i32, v.shape, 0)
    vals, idxs = [], []
    for _ in range(k):
        m = jnp.max(v, axis=0, keepdims=True)
        idx = jnp.min(jnp.where(v == m, rows, v.shape[0]), axis=0, keepdims=True)
        v = jnp.where(rows == idx, -jnp.inf, v)
        vals.append(m)
        idxs.append(idx)
    return jnp.concatenate(vals, axis=0), jnp.concatenate(idxs, axis=0)


_PAIR_ROWS = [PEER_TOPK if m == 0 else SUBLANES for m in range(PEER_TOPK)]
_PAIR_CNT = [PEER_TOPK // (m + 1) for m in range(PEER_TOPK)]


def _route_body(x_ref, g_ref, wq_ref, keys_ref, xn_ref, i1_ref, i2_ref, gate_ref, q_scr, e_scr, w_scr, *, tt):
    xn = _rms(x_ref[...], g_ref[...]).astype(bf16)
    xn_ref[...] = xn
    q_scr[...] = jnp.dot(xn, wq_ref[...], preferred_element_type=f32)
    hw = PEER_DK // 2
    nt = (((1,), (1,)), ((), ()))

    def head(h, c):
        row = pl.multiple_of(h * PEER_TOPK, PEER_TOPK)
        for sub in range(tt // LANES):
            rs = slice(sub * LANES, (sub + 1) * LANES)
            tops = []
            for p in range(2):
                col = pl.multiple_of(h * PEER_DK + p * hw, hw)
                qp = q_scr[rs, pl.ds(col, hw)]
                st = lax.dot_general(keys_ref[p], qp, nt, preferred_element_type=f32)
                tops.append(_top_rows(st, PEER_TOPK))
            (sv1, si1), (sv2, si2) = tops
            si1 = si1.astype(f32) * float(N_KEYS)
            si2 = si2.astype(f32)
            cands, ceids = [], []
            for m in range(PEER_TOPK):
                nr = _PAIR_ROWS[m]
                cm = sv1[m:m + 1] + sv2[:nr]
                if _PAIR_CNT[m] < nr:
                    cm = jnp.where(lax.broadcasted_iota(i32, cm.shape, 0) < _PAIR_CNT[m], cm, -jnp.inf)
                cands.append(cm)
                ceids.append(si1[m:m + 1] + si2[:nr])
            cand = jnp.concatenate(cands, axis=0)
            ceid = jnp.concatenate(ceids, axis=0)
            rows = lax.broadcasted_iota(i32, cand.shape, 0)
            cvs, eids = [], []
            for _ in range(PEER_TOPK):
                mx = jnp.max(cand, axis=0, keepdims=True)
                ridx = jnp.min(jnp.where(cand == mx, rows, cand.shape[0]), axis=0, keepdims=True)
                pick = rows == ridx
                eids.append(jnp.sum(jnp.where(pick, ceid, 0.0), axis=0, keepdims=True))
                cand = jnp.where(pick, -jnp.inf, cand)
                cvs.append(mx)
            cv = jnp.concatenate(cvs, axis=0)
            ex = jnp.exp(cv - cv[0:1])
            e_scr[pl.ds(row, PEER_TOPK), rs] = jnp.concatenate(eids, axis=0)
            w_scr[pl.ds(row, PEER_TOPK), rs] = ex / jnp.sum(ex, axis=0, keepdims=True)
        return c

    lax.fori_loop(0, PEER_HEADS, head, 0)
    eid = e_scr[...].T
    i1 = jnp.floor(eid * (1.0 / N_KEYS))
    i1_ref[...] = i1
    i2_ref[...] = eid - i1 * float(N_KEYS)
    gate_ref[...] = w_scr[...].T


def _peer_route(x, gain, wq_bf, keys):
    t, d = x.shape
    tt = min(t, 256)
    assert t % tt == 0 and tt % LANES == 0
    nsel = PEER_HEADS * PEER_TOPK
    sel = jax.ShapeDtypeStruct((t, nsel), f32)
    sspec = pl.BlockSpec((tt, nsel), lambda i: (i, 0))
    return pl.pallas_call(
        functools.partial(_route_body, tt=tt),
        grid=(t // tt,),
        in_specs=[pl.BlockSpec((tt, d), lambda i: (i, 0)),
                  pl.BlockSpec((1, d), lambda i: (0, 0)),
                  pl.BlockSpec(wq_bf.shape, lambda i: (0, 0)),
                  pl.BlockSpec(keys.shape, lambda i: (0, 0, 0))],
        out_specs=[pl.BlockSpec((tt, d), lambda i: (i, 0)), sspec, sspec, sspec],
        out_shape=[jax.ShapeDtypeStruct((t, d), bf16), sel, sel, sel],
        scratch_shapes=[pltpu.VMEM((tt, wq_bf.shape[1]), f32), pltpu.VMEM((nsel, tt), f32),
                        pltpu.VMEM((nsel, tt), f32)],
        compiler_params=_cparams("parallel"),
        name="peer_route",
    )(x, gain.reshape(1, d), wq_bf, keys)


def _gate_build_body(i1_ref, i2_ref, g_ref, o_ref, *, tb):
    keyrow = lax.broadcasted_iota(i32, (N_KEYS, i1_ref.shape[1]), 0).astype(f32)
    nt = (((1,), (1,)), ((), ()))

    def tok(t, c):
        i1 = i1_ref[pl.ds(t, 1), :]
        i2 = i2_ref[pl.ds(t, 1), :]
        g = g_ref[pl.ds(t, 1), :]
        a = jnp.where(keyrow == i1, 1.0, 0.0).astype(bf16)
        ghi = g.astype(bf16).astype(f32)
        glo = g - ghi
        m2 = keyrow == i2
        bhi = jnp.where(m2, ghi, 0.0).astype(bf16)
        blo = jnp.where(m2, glo, 0.0).astype(bf16)
        o_ref[t] = (lax.dot_general(a, bhi, nt, preferred_element_type=f32)
                    + lax.dot_general(a, blo, nt, preferred_element_type=f32))
        return c

    lax.fori_loop(0, tb, tok, 0)


def _gate_build(i1, i2, gate):
    t, nsel = i1.shape
    tb = min(t, 32)
    assert t % tb == 0
    spec = pl.BlockSpec((tb, nsel), lambda i: (i, 0))
    return pl.pallas_call(
        functools.partial(_gate_build_body, tb=tb),
        grid=(t // tb,),
        in_specs=[spec, spec, spec],
        out_specs=pl.BlockSpec((tb, N_KEYS, N_KEYS), lambda i: (i, 0, 0)),
        out_shape=jax.ShapeDtypeStruct((t, N_KEYS, N_KEYS), f32),
        compiler_params=_cparams("parallel"),
        name="peer_gate_build",
    )(i1, i2, gate)


def _peer_mix_body(x_ref, g_ref, u_ref, v_ref, res_ref, o_ref, *, na):
    j = pl.program_id(1)

    @pl.when(j == 0)
    def _():
        o_ref[...] = res_ref[...]

    h = lax.dot_general(x_ref[...], u_ref[...], (((1,), (1,)), ((), ())), preferred_element_type=f32)
    parts = []
    for a in range(na):
        ha = h[:, a * N_KEYS:(a + 1) * N_KEYS]
        act = 0.5 * ha * (1.0 + lax.erf(ha * (2.0 ** -0.5)))
        parts.append((g_ref[:, a, :] * act).astype(bf16))
    p = jnp.concatenate(parts, axis=1)
    o_ref[...] += jnp.dot(p, v_ref[...], preferred_element_type=f32)


def _peer_mix(xn, g3, u_bf, v_bf, res):
    t, d = xn.shape
    tt = min(t, 512)
    na = SUBLANES
    et = na * N_KEYS
    assert t % tt == 0
    return pl.pallas_call(
        functools.partial(_peer_mix_body, na=na),
        grid=(t // tt, N_EXPERTS // et),
        in_specs=[pl.BlockSpec((tt, d), lambda i, j: (i, 0)),
                  pl.BlockSpec((tt, na, N_KEYS), lambda i, j: (i, j, 0)),
                  pl.BlockSpec((et, d), lambda i, j: (j, 0)),
                  pl.BlockSpec((et, d), lambda i, j: (j, 0)),
                  pl.BlockSpec((tt, d), lambda i, j: (i, 0))],
        out_specs=pl.BlockSpec((tt, d), lambda i, j: (i, 0)),
        out_shape=jax.ShapeDtypeStruct((t, d), f32),
        compiler_params=_cparams("parallel", "arbitrary"),
        name="peer_mix",
    )(xn, g3, u_bf, v_bf, res)


def _peer(x, gain, wq_bf, keys, u_bf, v_bf):
    xn, i1, i2, gate = _peer_route(x, gain, wq_bf, keys)
    g3 = _gate_build(i1, i2, gate)
    return _peer_mix(xn, g3, u_bf, v_bf, x)


def _l2norm(x, eps):
    return x * lax.rsqrt(jnp.sum(x * x, axis=-1, keepdims=True) + eps)


def _rwkv7(p_cur, p_prev, s0, v_first, prm, first):
    b, t, _ = p_cur.shape
    m = (p_cur + (p_prev - p_cur) * prm['mu']).astype(f32)
    r, k, v = m[..., :C_B], m[..., C_B:2 * C_B], m[..., 2 * C_B:3 * C_B]
    o = 3 * C_B
    wl = m[..., o:o + LORA_W]
    o += LORA_W
    al = m[..., o:o + LORA_A]
    o += LORA_A
    gl = m[..., o:o + LORA_G]
    o += LORA_G
    w_log = -jax.nn.softplus(-(prm['w0'] + jnp.tanh(wl) @ prm['w2'])) - 0.5
    decay = jnp.exp(-jnp.exp(w_log.astype(f32)))
    a = jax.nn.sigmoid((prm['a0'] + al @ prm['a2']).astype(f32))
    g = (jax.nn.sigmoid(gl) @ prm['g2']).astype(f32)
    if first:
        v_first = v
    else:
        vl = m[..., o:o + LORA_V]
        v = v + (v_first - v) * jax.nn.sigmoid((prm['v0'] + vl @ prm['v2']).astype(f32))

    def hd(z):
        return z.reshape(b, t, H_B, DH_B)

    kk = _l2norm(hd(k * prm['kk_scale']), 1e-24)
    k = k * (1.0 + (a - 1.0) * prm['ka_mix'])
    r_h, k_h, v_h = hd(r), hd(k), hd(v)
    b_h = kk * hd(a)
    y, s_fin = _wkv_scan(r_h, hd(decay), k_h, kk, b_h, v_h, s0)
    mu = jnp.mean(y, axis=-1, keepdims=True)
    var = jnp.mean(jnp.square(y - mu), axis=-1, keepdims=True)
    y = ((y - mu) * lax.rsqrt(var + LNX_EPS)).reshape(b, t, C_B) * prm['lnx_w'] + prm['lnx_b']
    bonus = jnp.sum(r_h * k_h * prm['rk_bonus'], axis=-1, keepdims=True) * v_h
    y = (y + bonus.reshape(b, t, C_B)) * g
    return y, s_fin.astype(s0.dtype), v_first


def _select_blocks(q, kmean, qblk):
    n_sel = min(MOBA_TOPK, kmean.shape[2])
    gate = jnp.einsum('bhqd,bhjd->bhqj', q.astype(f32), kmean, precision=lax.Precision.HIGHEST)
    past = jnp.arange(kmean.shape[2])[None, :] < qblk[:, None]
    _, idx = lax.top_k(jnp.where(past, gate, NEG_INF), n_sel)
    return idx, idx < qblk[:, None]


def _rel_bias(dist, table):
    hi = jnp.arange(H_A)[None, :, None, None]
    return table.astype(f32)[_t5_bucket(dist), hi]


def _paged_rows(cache, page_table, new, pos, bi, hi):
    past = page_table.shape[1] * PAGE_SIZE
    pp = jnp.clip(pos, 0, past - 1)
    rows_past = cache[page_table[bi, pp // PAGE_SIZE], pp % PAGE_SIZE, hi]
    rows_new = new[bi, jnp.clip(pos - past, 0, new.shape[1] - 1), hi]
    return jnp.where((pos < past)[..., None], rows_past, rows_new.astype(rows_past.dtype))


def _moba_sample(q, k, v, cache_k, cache_v, page_table, rel_bias):
    bd, tn = q.shape[:2]
    past = page_table.shape[1] * PAGE_SIZE
    total = past + tn
    nb = -(-total // MOBA_BLOCK)
    lp = nb * MOBA_BLOCK
    k_past = cache_k[page_table].reshape(bd, past, H_A, DH_A)
    k_all = jnp.concatenate([k_past, k.astype(k_past.dtype)], axis=1)
    k_all = jnp.pad(k_all, ((0, 0), (0, lp - total), (0, 0), (0, 0)))
    kmean = jnp.mean(k_all.reshape(bd, nb, MOBA_BLOCK, H_A, DH_A).astype(f32), axis=2).transpose(0, 2, 1, 3)
    qh = q.transpose(0, 2, 1, 3)
    qpos = past + jnp.arange(tn)
    qblk = qpos // MOBA_BLOCK
    idx, ok = _select_blocks(qh, kmean, qblk)
    offs = jnp.arange(MOBA_BLOCK)
    pos_sel = idx[..., None] * MOBA_BLOCK + offs
    pos_own = jnp.broadcast_to((qblk * MOBA_BLOCK)[:, None, None] + offs, (bd, H_A, tn, 1, MOBA_BLOCK))
    n_rows = (idx.shape[-1] + 1) * MOBA_BLOCK
    pos = jnp.concatenate([pos_sel, pos_own], axis=3).reshape(bd, H_A, tn, n_rows)
    valid = jnp.concatenate([jnp.broadcast_to(ok[..., None], pos_sel.shape),
                             pos_own <= qpos[:, None, None]], axis=3).reshape(bd, H_A, tn, n_rows)
    bi = jnp.arange(bd)[:, None, None, None]
    hi = jnp.arange(H_A)[None, :, None, None]
    k_rows = k_all[bi, pos, hi]
    v_rows = _paged_rows(cache_v, page_table, v, pos, bi, hi)
    s = jnp.einsum('bhqd,bhqnd->bhqn', qh, k_rows.astype(qh.dtype)).astype(f32) * DH_A ** -0.5
    s = s + _rel_bias(qpos[None, None, :, None] - pos, rel_bias)
    p = jax.nn.softmax(jnp.where(valid, s, NEG_INF), axis=-1).astype(v_rows.dtype)
    o = jnp.einsum('bhqn,bhqnd->bhqd', p, v_rows)
    return o.transpose(0, 2, 1, 3)


def _gated_delta_chunked(q, k, v, g, beta, s0):
    b, t, h, dk = q.shape
    dv = v.shape[-1]
    c = GDN_CHUNK
    n = -(-t // c)
    tp = n * c - t

    def prep(z):
        z = jnp.pad(z, ((0, 0), (0, tp)) + ((0, 0),) * (z.ndim - 2))
        z = jnp.moveaxis(z, 2, 1)
        return z.reshape((b, h, n, c) + z.shape[3:])

    q, k, v, g, beta = (prep(z) for z in (q * dk ** -0.5, k, v, g, beta))
    gc = jnp.cumsum(g, axis=-1)
    tril = jnp.tril(jnp.ones((c, c), bool))
    strict = jnp.tril(jnp.ones((c, c), bool), -1)
    diff = gc[..., :, None] - gc[..., None, :]
    decay = jnp.where(tril, jnp.exp(jnp.where(tril, diff, 0.0)), 0.0)
    kb = k * beta[..., None]
    lower = jnp.where(strict, jnp.einsum('bhnid,bhnjd->bhnij', kb, k) * decay, 0.0)
    a_mat = lower + jnp.eye(c, dtype=f32)
    rhs = jnp.concatenate([v * beta[..., None], kb * jnp.exp(gc)[..., None]], axis=-1)
    sol = lax.linalg.triangular_solve(a_mat, rhs, left_side=True, lower=True, unit_diagonal=True)
    u_in, w_cum = sol[..., :dv], sol[..., dv:]
    attn = jnp.einsum('bhnid,bhnjd->bhnij', q, k) * decay
    q_dec = q * jnp.exp(gc)[..., None]
    k_tail = k * jnp.exp(gc[..., -1:] - gc)[..., None]
    g_tot = jnp.exp(gc[..., -1])

    def step(s, xs):
        u_c, w_c, a_c, qd_c, kt_c, gt_c = xs
        u = u_c - jnp.einsum('bhck,bhkv->bhcv', w_c, s)
        o = jnp.einsum('bhck,bhkv->bhcv', qd_c, s) + jnp.einsum('bhcj,bhjv->bhcv', a_c, u)
        s = s * gt_c[..., None, None] + jnp.einsum('bhck,bhcv->bhkv', kt_c, u)
        return s, o

    xs = tuple(jnp.moveaxis(z, 2, 0) for z in (u_in, w_cum, attn, q_dec, k_tail, g_tot))
    s_fin, o = lax.scan(step, s0, xs)
    o = jnp.moveaxis(o, 0, 2).reshape(b, h, n * c, dv)[:, :, :t]
    return jnp.moveaxis(o, 1, 2), s_fin


def _gdn_core(p, conv_buf, s0, prm):
    b, t, _ = p.shape
    xqkv = p[..., :CONV_CH]
    z = p[..., CONV_CH:CONV_CH + V_C].reshape(b, t, HV_C, DV_C)
    beta_raw = p[..., CONV_CH + V_C:CONV_CH + V_C + HV_C]
    a_raw = p[..., CONV_CH + V_C + HV_C:CONV_CH + V_C + 2 * HV_C]
    xc = jnp.concatenate([conv_buf.astype(xqkv.dtype), xqkv], axis=1)
    conv = sum(xc[:, i:i + t] * prm['conv_w'][i] for i in range(CONV_W))
    qkv = jax.nn.silu(conv)
    rep = HV_C // HK_C
    q = jnp.repeat(_l2norm(qkv[..., :QK_C].reshape(b, t, HK_C, DK_C), 1e-6), rep, axis=2)
    k = jnp.repeat(_l2norm(qkv[..., QK_C:2 * QK_C].reshape(b, t, HK_C, DK_C), 1e-6), rep, axis=2)
    v = qkv[..., 2 * QK_C:].reshape(b, t, HV_C, DV_C).astype(f32)
    beta = jax.nn.sigmoid(beta_raw.astype(f32))
    g = -jnp.exp(prm['a_log'].astype(f32)) * jax.nn.softplus(a_raw.astype(f32) + prm['dt_bias'].astype(f32))
    o, s_fin = _gated_delta_chunked(q, k, v, g, beta, s0.astype(f32))
    o = o * lax.rsqrt(jnp.mean(o * o, axis=-1, keepdims=True) + RMS_EPS) * prm['onorm_w']
    o = o * jax.nn.silu(z.astype(f32))
    return o.reshape(b, t, V_C), xc[:, -(CONV_W - 1):], s_fin.astype(s0.dtype)


def _pad_rows(x, rows):
    return jnp.pad(x, ((0, rows - x.shape[0]), (0, 0)))


def kernel(x_prompt, x_sample, cache_k_0, cache_v_0, state_wkv_0, state_shift_0, state_gdn_1, state_conv_1, cache_k_2, cache_v_2, state_wkv_2, state_shift_2, state_gdn_3, state_conv_3, page_table, rel_bias, norm_mix_0, w_in_0, mu_0, w0_0, w2_0, a0_0, a2_0, g2_0, kk_scale_0, ka_mix_0, rk_bonus_0, lnx_w_0, lnx_b_0, w_out_0, norm_ffn_0, peer_wq_0, peer_keys_0, peer_u_0, peer_v_0, norm_mix_1, w_in_1, conv_w_1, a_log_1, dt_bias_1, onorm_w_1, w_out_1, norm_ffn_1, peer_wq_1, peer_keys_1, peer_u_1, peer_v_1, norm_mix_2, w_in_2, mu_2, w0_2, w2_2, a0_2, a2_2, v0_2, v2_2, g2_2, kk_scale_2, ka_mix_2, rk_bonus_2, lnx_w_2, lnx_b_2, w_out_2, norm_ffn_2, peer_wq_2, peer_keys_2, peer_u_2, peer_v_2, norm_mix_3, w_in_3, conv_w_3, a_log_3, dt_bias_3, onorm_w_3, w_out_3, norm_ffn_3, peer_wq_3, peer_keys_3, peer_u_3, peer_v_3, norm_final):
    mix_prm = [
        dict(norm=norm_mix_0, w_in=w_in_0, mu=mu_0, w0=w0_0, w2=w2_0, a0=a0_0, a2=a2_0, g2=g2_0,
             kk_scale=kk_scale_0, ka_mix=ka_mix_0, rk_bonus=rk_bonus_0, lnx_w=lnx_w_0, lnx_b=lnx_b_0,
             w_out=w_out_0),
        dict(norm=norm_mix_1, w_in=w_in_1, conv_w=conv_w_1, a_log=a_log_1, dt_bias=dt_bias_1,
             onorm_w=onorm_w_1, w_out=w_out_1),
        dict(norm=norm_mix_2, w_in=w_in_2, mu=mu_2, w0=w0_2, w2=w2_2, a0=a0_2, a2=a2_2, v0=v0_2, v2=v2_2,
             g2=g2_2, kk_scale=kk_scale_2, ka_mix=ka_mix_2, rk_bonus=rk_bonus_2, lnx_w=lnx_w_2,
             lnx_b=lnx_b_2, w_out=w_out_2),
        dict(norm=norm_mix_3, w_in=w_in_3, conv_w=conv_w_3, a_log=a_log_3, dt_bias=dt_bias_3,
             onorm_w=onorm_w_3, w_out=w_out_3),
    ]
    ffn_prm = [
        (norm_ffn_0, peer_wq_0, peer_keys_0, peer_u_0, peer_v_0),
        (norm_ffn_1, peer_wq_1, peer_keys_1, peer_u_1, peer_v_1),
        (norm_ffn_2, peer_wq_2, peer_keys_2, peer_u_2, peer_v_2),
        (norm_ffn_3, peer_wq_3, peer_keys_3, peer_u_3, peer_v_3),
    ]
    layer_state = [
        (cache_k_0, cache_v_0, state_wkv_0, state_shift_0),
        (state_gdn_1, state_conv_1),
        (cache_k_2, cache_v_2, state_wkv_2, state_shift_2),
        (state_gdn_3, state_conv_3),
    ]
    bp, sp, d = x_prompt.shape
    bs, ts, _ = x_sample.shape
    ns = bs * ts
    ns_mm = -(-ns // 16) * 16
    ns_peer = -(-ns // LANES) * LANES
    xp = x_prompt.reshape(bp * sp, d)
    xs = x_sample.reshape(ns, d)
    depth = len(mix_prm)
    vf_p = vf_s = None
    new = []
    for i in range(depth):
        prm = mix_prm[i]
        xs_pad = _pad_rows(xs, ns_mm)
        pp = _matmul(xp, prm['w_in'], gain=prm['norm'])
        ps = _matmul(xs_pad, prm['w_in'], gain=prm['norm'])[:ns]
        if i % 2 == 0:
            ck, cv, wkv_in, shift_in = layer_state[i]
            first = i == 0
            rw = prm['w_in'].shape[1] - 3 * C_A
            o_a = _moba_prompt(pp, rel_bias, bp, sp)
            p_rw = pp[:, 3 * C_A:].reshape(bp, sp, rw)
            p_prev = jnp.concatenate([jnp.zeros((bp, 1, rw), f32), p_rw[:, :-1]], axis=1)
            o_b, wkv_p, vf_p = _rwkv7(p_rw, p_prev, jnp.zeros((bp, H_B, DH_B, DH_B), f32), vf_p, prm, first)
            cat = jnp.concatenate([o_a, o_b.reshape(bp * sp, C_B)], axis=-1)
            xp_new = _matmul(cat, prm['w_out'], res=xp)
            kp = pp[:, C_A:2 * C_A].reshape(bp, sp // PAGE_SIZE, PAGE_SIZE, H_A, DH_A)
            vp = pp[:, 2 * C_A:3 * C_A].reshape(bp, sp // PAGE_SIZE, PAGE_SIZE, H_A, DH_A)
            hp_last = _rms(xp.reshape(bp, sp, d)[:, -1], prm['norm'])
            qs = ps[:, :C_A].reshape(bs, ts, H_A, DH_A)
            ks = ps[:, C_A:2 * C_A].reshape(bs, ts, H_A, DH_A)
            vs = ps[:, 2 * C_A:3 * C_A].reshape(bs, ts, H_A, DH_A)
            o_as = _moba_sample(qs, ks, vs, ck, cv, page_table, rel_bias).reshape(ns, C_A)
            ps_rw = ps[:, 3 * C_A:].reshape(bs, ts, rw)
            prev0 = _matmul(_pad_rows(shift_in, ns_mm), prm['w_in'], col0=3 * C_A)[:bs]
            ps_prev = jnp.concatenate([prev0[:, None], ps_rw[:, :-1]], axis=1)
            o_bs, wkv_s, vf_s = _rwkv7(ps_rw, ps_prev, wkv_in, vf_s, prm, first)
            cat_s = jnp.concatenate([o_as, o_bs.reshape(ns, C_B)], axis=-1)
            xs_new = _matmul(_pad_rows(cat_s, ns_mm), prm['w_out'], res=xs_pad)[:ns]
            hs_last = _rms(xs.reshape(bs, ts, d)[:, -1], prm['norm'])
            new.append((kp, vp, ks, vs, wkv_p, wkv_s, hp_last, hs_last))
        else:
            gdn_in, conv_in = layer_state[i]
            wd = prm['w_in'].shape[1]
            o_p, conv_p, gdn_p = _gdn_core(pp.reshape(bp, sp, wd), jnp.zeros((bp, CONV_W - 1, CONV_CH), f32),
                                           jnp.zeros((bp, HV_C, DK_C, DV_C), f32), prm)
            o_s, conv_s, gdn_s = _gdn_core(ps.reshape(bs, ts, wd), conv_in, gdn_in, prm)
            xp_new = _matmul(o_p.reshape(bp * sp, V_C), prm['w_out'], res=xp)
            xs_new = _matmul(_pad_rows(o_s.reshape(ns, V_C), ns_mm), prm['w_out'], res=xs_pad)[:ns]
            new.append((gdn_p, gdn_s, conv_p, conv_s))
        xp, xs = xp_new, xs_new
        g_ffn, wq, sub_keys, u_tab, v_tab = ffn_prm[i]
        wq_bf, u_bf, v_bf = wq.astype(bf16), u_tab.astype(bf16), v_tab.astype(bf16)
        keys = sub_keys.astype(f32)
        xp = _peer(xp, g_ffn, wq_bf, keys, u_bf, v_bf)
        xs = _peer(_pad_rows(xs, ns_peer), g_ffn, wq_bf, keys, u_bf, v_bf)[:ns]
    y_prompt = _rms(xp, norm_final).reshape(bp, sp, d)
    y_sample = _rms(xs, norm_final).reshape(bs, ts, d)
    out = [y_prompt, y_sample]
    for layer in new:
        out.extend(layer)
    return tuple(out)
```

```python
import functools
import math

import jax
import jax.numpy as jnp
from jax import lax
from jax.experimental import pallas as pl
from jax.experimental.pallas import tpu as pltpu

f32 = jnp.float32
bf16 = jnp.bfloat16
i32 = jnp.int32

RMS_EPS = 1e-6
NEG_INF = -1e30

PAGE_SIZE = 128
H_A, DH_A = 8, 128
C_A = H_A * DH_A
MOBA_BLOCK, MOBA_TOPK = 256, 3
NUM_BUCKETS, MAX_DISTANCE = 32, 128
H_B, DH_B = 16, 64
C_B = H_B * DH_B
LORA_W, LORA_A, LORA_G, LORA_V = 64, 64, 160, 32
LNX_EPS = 64e-5
HK_C, HV_C, DK_C, DV_C = 16, 32, 128, 128
QK_C, V_C = HK_C * DK_C, HV_C * DV_C
CONV_W = 4
CONV_CH = 2 * QK_C + V_C
GDN_CHUNK = 64
PEER_HEADS, N_KEYS, PEER_DK, PEER_TOPK = 8, 128, 256, 16
N_EXPERTS = N_KEYS * N_KEYS

LANES = 128
SUBLANES = 8
VMEM_LIMIT = 56 * 1024 * 1024


def _cparams(*sem):
    return pltpu.CompilerParams(dimension_semantics=sem, vmem_limit_bytes=VMEM_LIMIT)


def _rms(x, g):
    return x * lax.rsqrt(jnp.mean(x * x, axis=-1, keepdims=True) + RMS_EPS) * g


def _mm_body(*refs, norm, has_res):
    if has_res:
        x_ref, g_ref, w_ref, res_ref, o_ref, h_ref = refs
    else:
        x_ref, g_ref, w_ref, o_ref, h_ref = refs

    @pl.when(pl.program_id(1) == 0)
    def _():
        x = x_ref[...]
        if norm:
            x = _rms(x, g_ref[...])
        h_ref[...] = x.astype(bf16)

    acc = jnp.dot(h_ref[...], w_ref[...].astype(bf16), preferred_element_type=f32)
    if has_res:
        acc = res_ref[...] + acc
    o_ref[...] = acc


def _matmul(x, w, gain=None, res=None, col0=0, ncols=None, tn=512):
    m, k = x.shape
    ncols = w.shape[1] - col0 if ncols is None else ncols
    assert col0 % tn == 0 and m % 16 == 0
    tm = min(m, 512)
    assert m % tm == 0
    nj = pl.cdiv(ncols, tn)
    j0 = col0 // tn
    norm = gain is not None
    g = (gain if norm else jnp.ones((k,), f32)).reshape(1, k)
    in_specs = [pl.BlockSpec((tm, k), lambda i, j: (i, 0)),
                pl.BlockSpec((1, k), lambda i, j: (0, 0)),
                pl.BlockSpec((k, tn), lambda i, j: (0, j + j0))]
    args = [x, g, w]
    if res is not None:
        in_specs.append(pl.BlockSpec((tm, tn), lambda i, j: (i, j)))
        args.append(res)
    return pl.pallas_call(
        functools.partial(_mm_body, norm=norm, has_res=res is not None),
        grid=(m // tm, nj),
        in_specs=in_specs,
        out_specs=pl.BlockSpec((tm, tn), lambda i, j: (i, j)),
        out_shape=jax.ShapeDtypeStruct((m, ncols), f32),
        scratch_shapes=[pltpu.VMEM((tm, k), bf16)],
        compiler_params=_cparams("parallel", "arbitrary"),
        name="matmul",
    )(*args)


def _t5_bucket(dist):
    n = jnp.maximum(dist, 0)
    max_exact = NUM_BUCKETS // 2
    ratio = jnp.log(jnp.maximum(n, 1).astype(f32) / max_exact) / math.log(MAX_DISTANCE / max_exact)
    large = max_exact + (ratio * (NUM_BUCKETS - max_exact)).astype(i32)
    return jnp.where(n < max_exact, n, jnp.minimum(large, NUM_BUCKETS - 1))


def _moba_prompt_body(q_ref, k_ref, v_ref, bias_ref, o_ref, km_ref, *, nb):
    qb = pl.program_id(2)
    blk = MOBA_BLOCK
    s_len = nb * blk

    @pl.when(qb == 0)
    def _():
        k = k_ref[...]
        km = jnp.mean(k.reshape(nb, blk, DH_A), axis=1)
        km_ref[...] = jnp.zeros_like(km_ref)
        km_ref[0:nb, :] = km

    q = q_ref[...]
    gate = lax.dot_general(q, km_ref[...], (((1,), (1,)), ((), ())), preferred_element_type=f32,
                           precision=lax.Precision.HIGHEST)
    lane = lax.broadcasted_iota(i32, gate.shape, 1)
    past = lane < qb
    g = jnp.where(past, gate, NEG_INF)
    sel = jnp.zeros(gate.shape, f32)
    for _ in range(MOBA_TOPK):
        m = jnp.max(g, axis=1, keepdims=True)
        idx = jnp.min(jnp.where(g == m, lane, LANES), axis=1, keepdims=True)
        pick = lane == idx
        sel = jnp.where(pick, jnp.where(past, 1.0, 0.0), sel)
        g = jnp.where(pick, -jnp.inf, g)
    sel = jnp.where(lane == qb, 1.0, sel)
    selneg = jnp.where(sel > 0.0, 0.0, NEG_INF).astype(bf16)
    blk_of_key = lax.broadcasted_iota(i32, (LANES, s_len), 1) // blk
    blockind = jnp.where(blk_of_key == lax.broadcasted_iota(i32, (LANES, s_len), 0), 1.0, 0.0).astype(bf16)
    maskadd = jnp.dot(selneg, blockind, preferred_element_type=f32)

    s = lax.dot_general(q.astype(bf16), k_ref[...].astype(bf16), (((1,), (1,)), ((), ())),
                        preferred_element_type=f32) * (DH_A ** -0.5)
    off = pl.multiple_of((nb - 1 - qb) * blk, blk)
    s = s + bias_ref[0, :, pl.ds(off, s_len)] + maskadd
    kidx = lax.broadcasted_iota(i32, s.shape, 1)
    qidx = qb * blk + lax.broadcasted_iota(i32, s.shape, 0)
    s = jnp.where(kidx <= qidx, s, NEG_INF)
    m = jnp.max(s, axis=1, keepdims=True)
    p = jnp.exp(s - m)
    l = jnp.sum(p, axis=1, keepdims=True)
    o = jnp.dot(p.astype(bf16), v_ref[...].astype(bf16), preferred_element_type=f32)
    o_ref[...] = o / l


def _moba_prompt(p, rel_bias, b, s_len):
    blk = MOBA_BLOCK
    assert s_len % blk == 0
    nb = s_len // blk
    r = jnp.arange(blk)[:, None]
    x = jnp.arange((2 * nb - 1) * blk)[None, :]
    strip = rel_bias.astype(f32)[_t5_bucket(r - x + (nb - 1) * blk)]
    strip = jnp.moveaxis(strip, 2, 0)
    hd = C_A // DH_A
    return pl.pallas_call(
        functools.partial(_moba_prompt_body, nb=nb),
        grid=(b, H_A, nb),
        in_specs=[pl.BlockSpec((blk, DH_A), lambda bi, h, qb: (bi * nb + qb, h)),
                  pl.BlockSpec((s_len, DH_A), lambda bi, h, qb: (bi, hd + h)),
                  pl.BlockSpec((s_len, DH_A), lambda bi, h, qb: (bi, 2 * hd + h)),
                  pl.BlockSpec((1, blk, (2 * nb - 1) * blk), lambda bi, h, qb: (h, 0, 0))],
        out_specs=pl.BlockSpec((blk, DH_A), lambda bi, h, qb: (bi * nb + qb, h)),
        out_shape=jax.ShapeDtypeStruct((b * s_len, C_A), f32),
        scratch_shapes=[pltpu.VMEM((LANES, DH_A), f32)],
        compiler_params=_cparams("parallel", "parallel", "arbitrary"),
        name="moba_prompt",
    )(p, p, p, strip)


def _wkv_body(r_ref, w_ref, k_ref, kk_ref, b_ref, v_ref, s0_ref, y_ref, sf_ref, s_ref, *, tc):
    @pl.when(pl.program_id(1) == 0)
    def _():
        s_ref[...] = s0_ref[0]

    half = LANES // 2

    def step(t, c):
        r, w, k, kk, b = r_ref[0, t], w_ref[0, t], k_ref[0, t], kk_ref[0, t], b_ref[0, t]
        for i in range(DH_B):
            s = s_ref[i]
            vi = v_ref[0, t, pl.ds(i, 1), :]
            t1 = jnp.sum(s * kk, axis=0, keepdims=True)
            sa = -(t1 + pltpu.roll(t1, half, 1))
            s = s * w + sa * b + vi * k
            s_ref[i] = s
            t2 = jnp.sum(s * r, axis=0, keepdims=True)
            y_ref[0, t, pl.ds(i, 1), :] = t2 + pltpu.roll(t2, half, 1)
        return c

    lax.fori_loop(0, tc, step, 0)

    @pl.when(pl.program_id(1) == pl.num_programs(1) - 1)
    def _():
        sf_ref[0] = s_ref[...]


def _wkv_scan(r, w, k, kk, b, v, s0):
    bsz, t, h, n = r.shape
    assert n == DH_B and (bsz * h) % (LANES // 2) == 0
    grp = bsz * h // (LANES // 2)
    bg = bsz // grp
    hn = n // 2

    def vec(z):
        z = z.reshape(grp, bg, t, h, 2, hn)
        return z.transpose(0, 2, 5, 4, 1, 3).reshape(grp, t, hn, LANES)

    def row(z):
        z = z.reshape(grp, bg, t, h, n).transpose(0, 2, 4, 1, 3).reshape(grp, t, n, LANES // 2)
        return jnp.concatenate([z, z], axis=-1)

    s0k = s0.astype(f32).reshape(grp, bg, h, n, 2, hn).transpose(0, 3, 5, 4, 1, 2).reshape(grp, n, hn, LANES)
    tc = math.gcd(t, 64)
    vspec = pl.BlockSpec((1, tc, hn, LANES), lambda g, c: (g, c, 0, 0))
    rspec = pl.BlockSpec((1, tc, n, LANES), lambda g, c: (g, c, 0, 0))
    sspec = pl.BlockSpec((1, n, hn, LANES), lambda g, c: (g, 0, 0, 0))
    y, sf = pl.pallas_call(
        functools.partial(_wkv_body, tc=tc),
        grid=(grp, t // tc),
        in_specs=[vspec] * 5 + [rspec, sspec],
        out_specs=[rspec, sspec],
        out_shape=[jax.ShapeDtypeStruct((grp, t, n, LANES), f32), jax.ShapeDtypeStruct((grp, n, hn, LANES), f32)],
        scratch_shapes=[pltpu.VMEM((n, hn, LANES), f32)],
        compiler_params=_cparams("parallel", "arbitrary"),
        name="wkv_scan",
    )(vec(r), vec(w), vec(k), vec(kk), vec(b), row(v), s0k)
    y = y[..., :LANES // 2].reshape(grp, t, n, bg, h).transpose(0, 3, 1, 4, 2).reshape(bsz, t, h, n)
    sf = sf.reshape(grp, n, hn, 2, bg, h).transpose(0, 4, 5, 1, 3, 2).reshape(bsz, h, n, n)
    return y, sf


def _top_rows(v, k):
    rows = lax.broadcasted_iota(i32, v.shape, 0)
    vals, idxs = [], []
    for _ in range(k):
        m = jnp.max(v, axis=0, keepdims=True)
        idx = jnp.min(jnp.where(v == m, rows, v.shape[0]), axis=0, keepdims=True)
        v = jnp.where(rows == idx, -jnp.inf, v)
        vals.append(m)
        idxs.append(idx)
    return jnp.concatenate(vals, axis=0), jnp.concatenate(idxs, axis=0)


_PAIR_ROWS = [PEER_TOPK if m == 0 else SUBLANES for m in range(PEER_TOPK)]
_PAIR_CNT = [PEER_TOPK // (m + 1) for m in range(PEER_TOPK)]


def _route_body(x_ref, g_ref, wq_ref, keys_ref, xn_ref, i1_ref, i2_ref, gate_ref, q_scr, e_scr, w_scr, *, tt):
    xn = _rms(x_ref[...], g_ref[...]).astype(bf16)
    xn_ref[...] = xn
    q_scr[...] = jnp.dot(xn, wq_ref[...], preferred_element_type=f32)
    hw = PEER_DK // 2
    nt = (((1,), (1,)), ((), ()))

    def head(h, c):
        row = pl.multiple_of(h * PEER_TOPK, PEER_TOPK)
        for sub in range(tt // LANES):
            rs = slice(sub * LANES, (sub + 1) * LANES)
            tops = []
            for p in range(2):
                col = pl.multiple_of(h * PEER_DK + p * hw, hw)
                qp = q_scr[rs, pl.ds(col, hw)]
                st = lax.dot_general(keys_ref[p], qp, nt, preferred_element_type=f32)
                tops.append(_top_rows(st, PEER_TOPK))
            (sv1, si1), (sv2, si2) = tops
            si1 = si1.astype(f32) * float(N_KEYS)
            si2 = si2.astype(f32)
            cands, ceids = [], []
            for m in range(PEER_TOPK):
                nr = _PAIR_ROWS[m]
                cm = sv1[m:m + 1] + sv2[:nr]
                if _PAIR_CNT[m] < nr:
                    cm = jnp.where(lax.broadcasted_iota(i32, cm.shape, 0) < _PAIR_CNT[m], cm, -jnp.inf)
                cands.append(cm)
                ceids.append(si1[m:m + 1] + si2[:nr])
            cand = jnp.concatenate(cands, axis=0)
            ceid = jnp.concatenate(ceids, axis=0)
            rows = lax.broadcasted_iota(i32, cand.shape, 0)
            cvs, eids = [], []
            for _ in range(PEER_TOPK):
                mx = jnp.max(cand, axis=0, keepdims=True)
                ridx = jnp.min(jnp.where(cand == mx, rows, cand.shape[0]), axis=0, keepdims=True)
                pick = rows == ridx
                eids.append(jnp.sum(jnp.where(pick, ceid, 0.0), axis=0, keepdims=True))
                cand = jnp.where(pick, -jnp.inf, cand)
                cvs.append(mx)
            cv = jnp.concatenate(cvs, axis=0)
            ex = jnp.exp(cv - cv[0:1])
            e_scr[pl.ds(row, PEER_TOPK), rs] = jnp.concatenate(eids, axis=0)
            w_scr[pl.ds(row, PEER_TOPK), rs] = ex / jnp.sum(ex, axis=0, keepdims=True)
        return c

    lax.fori_loop(0, PEER_HEADS, head, 0)
    eid = e_scr[...].T
    i1 = jnp.floor(eid * (1.0 / N_KEYS))
    i1_ref[...] = i1
    i2_ref[...] = eid - i1 * float(N_KEYS)
    gate_ref[...] = w_scr[...].T


def _peer_route(x, gain, wq_bf, keys):
    t, d = x.shape
    tt = min(t, 256)
    assert t % tt == 0 and tt % LANES == 0
    nsel = PEER_HEADS * PEER_TOPK
    sel = jax.ShapeDtypeStruct((t, nsel), f32)
    sspec = pl.BlockSpec((tt, nsel), lambda i: (i, 0))
    return pl.pallas_call(
        functools.partial(_route_body, tt=tt),
        grid=(t // tt,),
        in_specs=[pl.BlockSpec((tt, d), lambda i: (i, 0)),
                  pl.BlockSpec((1, d), lambda i: (0, 0)),
                  pl.BlockSpec(wq_bf.shape, lambda i: (0, 0)),
                  pl.BlockSpec(keys.shape, lambda i: (0, 0, 0))],
        out_specs=[pl.BlockSpec((tt, d), lambda i: (i, 0)), sspec, sspec, sspec],
        out_shape=[jax.ShapeDtypeStruct((t, d), bf16), sel, sel, sel],
        scratch_shapes=[pltpu.VMEM((tt, wq_bf.shape[1]), f32), pltpu.VMEM((nsel, tt), f32),
                        pltpu.VMEM((nsel, tt), f32)],
        compiler_params=_cparams("parallel"),
        name="peer_route",
    )(x, gain.reshape(1, d), wq_bf, keys)


def _gate_build_body(i1_ref, i2_ref, g_ref, o_ref, *, tb):
    keyrow = lax.broadcasted_iota(i32, (N_KEYS, i1_ref.shape[1]), 0).astype(f32)
    nt = (((1,), (1,)), ((), ()))

    def toks(i, c):
        t0 = pl.multiple_of(i * SUBLANES, SUBLANES)
        base = pl.multiple_of(i * (SUBLANES * N_KEYS), SUBLANES * N_KEYS)
        i1 = i1_ref[pl.ds(t0, SUBLANES), :]
        i2 = i2_ref[pl.ds(t0, SUBLANES), :]
        g = g_ref[pl.ds(t0, SUBLANES), :]
        ghi = g.astype(bf16).astype(f32)
        glo = g - ghi
        ops = []
        for r in range(SUBLANES):
            a = jnp.where(keyrow == i1[r:r + 1], 1.0, 0.0).astype(bf16)
            m2 = keyrow == i2[r:r + 1]
            ops.append((a, jnp.where(m2, ghi[r:r + 1], 0.0).astype(bf16),
                        jnp.where(m2, glo[r:r + 1], 0.0).astype(bf16)))
        gs = [lax.dot_general(a, bhi, nt, preferred_element_type=f32)
              + lax.dot_general(a, blo, nt, preferred_element_type=f32) for a, bhi, blo in ops]
        for r in range(SUBLANES):
            o_ref[pl.ds(base + r, N_KEYS, stride=SUBLANES), :] = gs[r]
        return c

    lax.fori_loop(0, tb // SUBLANES, toks, 0)


def _gate_build(i1, i2, gate):
    t, nsel = i1.shape
    tb = min(t, 32)
    assert t % tb == 0 and tb % SUBLANES == 0
    spec = pl.BlockSpec((tb, nsel), lambda i: (i, 0))
    g = pl.pallas_call(
        functools.partial(_gate_build_body, tb=tb),
        grid=(t // tb,),
        in_specs=[spec, spec, spec],
        out_specs=pl.BlockSpec((tb * N_KEYS, N_KEYS), lambda i: (i, 0)),
        out_shape=jax.ShapeDtypeStruct((t * N_KEYS, N_KEYS), f32),
        compiler_params=_cparams("parallel"),
        name="peer_gate_build",
    )(i1, i2, gate)
    return g.reshape(t // SUBLANES, N_KEYS, SUBLANES, N_KEYS)


def _peer_mix_body(x_ref, g_ref, u_ref, v_ref, o_ref, *, na, tt):
    j = pl.program_id(1)

    @pl.when(j == 0)
    def _():
        o_ref[...] = jnp.zeros_like(o_ref)

    h = lax.dot_general(x_ref[...], u_ref[...], (((1,), (1,)), ((), ())), preferred_element_type=f32)
    parts = []
    for a in range(na):
        ha = h[:, a * N_KEYS:(a + 1) * N_KEYS]
        act = 0.5 * ha * (1.0 + lax.erf(ha * (2.0 ** -0.5)))
        parts.append((g_ref[:, a].reshape(tt, N_KEYS) * act).astype(bf16))
    p = jnp.concatenate(parts, axis=1)
    o_ref[...] += jnp.dot(p, v_ref[...], preferred_element_type=f32)


def _peer_mix(xn, g4, u_bf, v_bf):
    t, d = xn.shape
    tt = min(t, 1024)
    na = 4
    et = na * N_KEYS
    assert t % tt == 0 and tt % SUBLANES == 0
    return pl.pallas_call(
        functools.partial(_peer_mix_body, na=na, tt=tt),
        grid=(t // tt, N_EXPERTS // et),
        in_specs=[pl.BlockSpec((tt, d), lambda i, j: (i, 0)),
                  pl.BlockSpec((tt // SUBLANES, na, SUBLANES, N_KEYS), lambda i, j: (i, j, 0, 0)),
                  pl.BlockSpec((et, d), lambda i, j: (j, 0)),
                  pl.BlockSpec((et, d), lambda i, j: (j, 0))],
        out_specs=pl.BlockSpec((tt, d), lambda i, j: (i, 0)),
        out_shape=jax.ShapeDtypeStruct((t, d), f32),
        compiler_params=_cparams("parallel", "arbitrary"),
        name="peer_mix",
    )(xn, g4, u_bf, v_bf)


def _peer(x, gain, wq_bf, keys, u_bf, v_bf):
    xn, i1, i2, gate = _peer_route(x, gain, wq_bf, keys)
    return x + _peer_mix(xn, _gate_build(i1, i2, gate), u_bf, v_bf)


def _l2norm(x, eps):
    return x * lax.rsqrt(jnp.sum(x * x, axis=-1, keepdims=True) + eps)


def _rwkv7(p_cur, p_prev, s0, v_first, prm, first):
    b, t, _ = p_cur.shape
    m = (p_cur + (p_prev - p_cur) * prm['mu']).astype(f32)
    r, k, v = m[..., :C_B], m[..., C_B:2 * C_B], m[..., 2 * C_B:3 * C_B]
    o = 3 * C_B
    wl = m[..., o:o + LORA_W]
    o += LORA_W
    al = m[..., o:o + LORA_A]
    o += LORA_A
    gl = m[..., o:o + LORA_G]
    o += LORA_G
    w_log = -jax.nn.softplus(-(prm['w0'] + jnp.tanh(wl) @ prm['w2'])) - 0.5
    decay = jnp.exp(-jnp.exp(w_log.astype(f32)))
    a = jax.nn.sigmoid((prm['a0'] + al @ prm['a2']).astype(f32))
    g = (jax.nn.sigmoid(gl) @ prm['g2']).astype(f32)
    if first:
        v_first = v
    else:
        vl = m[..., o:o + LORA_V]
        v = v + (v_first - v) * jax.nn.sigmoid((prm['v0'] + vl @ prm['v2']).astype(f32))

    def hd(z):
        return z.reshape(b, t, H_B, DH_B)

    kk = _l2norm(hd(k * prm['kk_scale']), 1e-24)
    k = k * (1.0 + (a - 1.0) * prm['ka_mix'])
    r_h, k_h, v_h = hd(r), hd(k), hd(v)
    b_h = kk * hd(a)
    y, s_fin = _wkv_scan(r_h, hd(decay), k_h, kk, b_h, v_h, s0)
    mu = jnp.mean(y, axis=-1, keepdims=True)
    var = jnp.mean(jnp.square(y - mu), axis=-1, keepdims=True)
    y = ((y - mu) * lax.rsqrt(var + LNX_EPS)).reshape(b, t, C_B) * prm['lnx_w'] + prm['lnx_b']
    bonus = jnp.sum(r_h * k_h * prm['rk_bonus'], axis=-1, keepdims=True) * v_h
    y = (y + bonus.reshape(b, t, C_B)) * g
    return y, s_fin.astype(s0.dtype), v_first


def _select_blocks(q, kmean, qblk):
    n_sel = min(MOBA_TOPK, kmean.shape[2])
    gate = jnp.einsum('bhqd,bhjd->bhqj', q.astype(f32), kmean, precision=lax.Precision.HIGHEST)
    past = jnp.arange(kmean.shape[2])[None, :] < qblk[:, None]
    _, idx = lax.top_k(jnp.where(past, gate, NEG_INF), n_sel)
    return idx, idx < qblk[:, None]


def _kmean_body(pt_ref, *refs):
    o_ref = refs[-1]
    acc = jnp.sum(refs[0][0], axis=0, keepdims=True)
    for r in refs[1:-1]:
        acc = acc + jnp.sum(r[0], axis=0, keepdims=True)
    o_ref[0, 0] = acc * (1.0 / MOBA_BLOCK)


def _paged_block_means(cache, page_table):
    bd, n_pages = page_table.shape
    ppb = MOBA_BLOCK // PAGE_SIZE
    nbp = n_pages // ppb
    c = cache.shape[-1]
    specs = [pl.BlockSpec((1, PAGE_SIZE, c), functools.partial(lambda b, j, pt, r: (pt[b, j * ppb + r], 0, 0), r=r))
             for r in range(ppb)]
    return pl.pallas_call(
        _kmean_body,
        grid_spec=pltpu.PrefetchScalarGridSpec(
            num_scalar_prefetch=1, grid=(bd, nbp), in_specs=specs,
            out_specs=pl.BlockSpec((1, 1, 1, c), lambda b, j, pt: (b, j, 0, 0))),
        out_shape=jax.ShapeDtypeStruct((bd, nbp, 1, c), f32),
        compiler_params=_cparams("parallel", "arbitrary"),
        name="moba_block_means",
    )(page_table, *([cache] * ppb))


def _moba_decode_body(pp_ref, lp_ref, ok_ref, q_ref, kn_ref, vn_ref, ob_ref, k_ref, v_ref, b_ref, o_ref,
                      m_ref, l_ref, acc_ref, *, ppb):
    b, h, s = pl.program_id(0), pl.program_id(1), pl.program_id(2)
    scale = DH_A ** -0.5
    q = q_ref[0]

    @pl.when(s == 0)
    def _():
        m_ref[...] = jnp.sum(q * kn_ref[0], axis=1, keepdims=True) * scale + ob_ref[0][:, 0:1]
        l_ref[...] = jnp.ones_like(l_ref)
        acc_ref[...] = vn_ref[0]

    sc = lax.dot_general(q.astype(bf16), k_ref[0].astype(bf16), (((1,), (1,)), ((), ())),
                         preferred_element_type=f32) * scale + b_ref[0, 0]
    sc = jnp.where(ok_ref[b, h, s // ppb] != 0, sc, NEG_INF)
    m_old = m_ref[...]
    m_new = jnp.maximum(m_old, jnp.max(sc, axis=1, keepdims=True))
    alpha = jnp.exp(m_old - m_new)
    p = jnp.exp(sc - m_new)
    m_ref[...] = m_new
    l_ref[...] = alpha * l_ref[...] + jnp.sum(p, axis=1, keepdims=True)
    acc_ref[...] = alpha * acc_ref[...] + jnp.dot(p.astype(bf16), v_ref[0].astype(bf16), preferred_element_type=f32)

    @pl.when(s == pl.num_programs(2) - 1)
    def _():
        o_ref[0] = acc_ref[...] / l_ref[...]


def _moba_sample(q, k, v, cache_k, cache_v, page_table, rel_bias):
    bd, tn = q.shape[:2]
    n_pages = page_table.shape[1]
    past = n_pages * PAGE_SIZE
    assert tn == 1 and past % MOBA_BLOCK == 0 and MOBA_BLOCK % PAGE_SIZE == 0
    ppb = MOBA_BLOCK // PAGE_SIZE
    nbp = past // MOBA_BLOCK
    ck = cache_k.reshape(cache_k.shape[0], PAGE_SIZE, C_A)
    cv = cache_v.reshape(cache_v.shape[0], PAGE_SIZE, C_A)
    km_past = _paged_block_means(ck, page_table)[:, :, 0]
    km_new = k.astype(f32).reshape(bd, 1, C_A) * (1.0 / MOBA_BLOCK)
    kmean = jnp.concatenate([km_past, km_new], axis=1).reshape(bd, nbp + 1, H_A, DH_A).transpose(0, 2, 1, 3)
    qpos = past + jnp.arange(tn)
    idx, ok = _select_blocks(q.transpose(0, 2, 1, 3), kmean, qpos // MOBA_BLOCK)
    idx, ok = idx[:, :, 0], ok[:, :, 0]
    n_sel = idx.shape[-1]
    lpage = jnp.clip((idx[..., None] * ppb + jnp.arange(ppb)).reshape(bd, H_A, n_sel * ppb), 0, n_pages - 1)
    ppage = jnp.take_along_axis(page_table[:, None, :], lpage, axis=2).astype(i32)
    okp = ok.astype(i32)
    pos = jnp.arange(past)
    bias_pos = rel_bias.astype(f32)[_t5_bucket(qpos[0] - pos)].T.reshape(H_A, n_pages, 1, PAGE_SIZE)
    own_bias = jnp.broadcast_to(rel_bias.astype(f32)[_t5_bucket(jnp.zeros((), i32))][:, None, None], (H_A, 1, LANES))
    rep8 = lambda z: jnp.broadcast_to(z.reshape(bd, 1, C_A).astype(f32), (bd, SUBLANES, C_A))
    hspec = pl.BlockSpec((1, SUBLANES, DH_A), lambda b, h, s, pp, lp, okr: (b, 0, h))
    o = pl.pallas_call(
        functools.partial(_moba_decode_body, ppb=ppb),
        grid_spec=pltpu.PrefetchScalarGridSpec(
            num_scalar_prefetch=3, grid=(bd, H_A, n_sel * ppb),
            in_specs=[hspec, hspec, hspec,
                      pl.BlockSpec((1, 1, LANES), lambda b, h, s, pp, lp, okr: (h, 0, 0)),
                      pl.BlockSpec((1, PAGE_SIZE, DH_A), lambda b, h, s, pp, lp, okr: (pp[b, h, s], 0, h)),
                      pl.BlockSpec((1, PAGE_SIZE, DH_A), lambda b, h, s, pp, lp, okr: (pp[b, h, s], 0, h)),
                      pl.BlockSpec((1, 1, 1, PAGE_SIZE), lambda b, h, s, pp, lp, okr: (h, lp[b, h, s], 0, 0))],
            out_specs=hspec,
            scratch_shapes=[pltpu.VMEM((SUBLANES, 1), f32), pltpu.VMEM((SUBLANES, 1), f32),
                            pltpu.VMEM((SUBLANES, DH_A), f32)]),
        out_shape=jax.ShapeDtypeStruct((bd, SUBLANES, C_A), f32),
        compiler_params=_cparams("parallel", "parallel", "arbitrary"),
        name="moba_decode",
    )(ppage, lpage.astype(i32), okp, rep8(q), rep8(k), rep8(v), own_bias, ck, cv, bias_pos)
    return o[:, 0].reshape(bd, tn, H_A, DH_A)


def _sigmoid(x):
    return 1.0 / (1.0 + jnp.exp(-x))


def _dot3(a, b):
    ah = a.astype(bf16)
    al = (a - ah.astype(f32)).astype(bf16)
    bh = b.astype(bf16)
    bl = (b - bh.astype(f32)).astype(bf16)
    return (jnp.dot(ah, bh, preferred_element_type=f32) + jnp.dot(ah, bl, preferred_element_type=f32)
            + jnp.dot(al, bh, preferred_element_type=f32))


def _col_to_row(col, eye):
    return jnp.sum(jnp.where(eye, col, 0.0), axis=0, keepdims=True)


def _gdn_prep_body(q_ref, k_ref, v_ref, bg_ref, cq_ref, ck_ref, cv_ref, wq_ref, wk_ref, wv_ref, gp_ref,
                   uin_ref, wcum_ref, qdec_ref, ktail_ref, attn_ref, gtot_ref, pq_ref, pk_ref, pv_ref, *, t_len, nsub):
    hk = pl.program_id(1)
    c = pl.program_id(2)
    cs = GDN_CHUNK
    rep = HV_C // HK_C
    rows = nsub * cs

    @pl.when(c == 0)
    def _():
        pq_ref[...] = cq_ref[0]
        pk_ref[...] = ck_ref[0]
        pv_ref[...] = cv_ref[0]

    def conv_silu(x_ref, prev_ref, w_ref):
        cur = x_ref[...]
        ext = jnp.concatenate([prev_ref[...], cur], axis=0)
        acc = cur * w_ref[CONV_W - 1:CONV_W, :]
        for i in range(CONV_W - 1):
            lo = SUBLANES - (CONV_W - 1) + i
            acc = acc + ext[lo:lo + rows] * w_ref[i:i + 1, :]
        prev_ref[...] = cur[rows - SUBLANES:]
        return acc * _sigmoid(acc)

    def l2n(x):
        return x * lax.rsqrt(jnp.sum(x * x, axis=-1, keepdims=True) + 1e-6)

    q_all = l2n(conv_silu(q_ref, pq_ref, wq_ref)) * (DK_C ** -0.5)
    k_all = l2n(conv_silu(k_ref, pk_ref, wk_ref))
    v_all = conv_silu(v_ref, pv_ref, wv_ref)

    bg = bg_ref[...]
    beta_all = _sigmoid(bg)
    xg = bg + gp_ref[1:2, :]
    g_all = -jnp.exp(gp_ref[0:1, :]) * (jnp.maximum(xg, 0.0) + jnp.log1p(jnp.exp(-jnp.abs(xg))))
    lane = lax.broadcasted_iota(i32, bg.shape, 1)
    if t_len % cs:
        live = (c * rows + lax.broadcasted_iota(i32, (rows, 1), 0)) < t_len
    hsel = []
    for e in range(rep):
        hv = hk * rep + e
        beta = jnp.sum(jnp.where(lane == hv, beta_all, 0.0), axis=1, keepdims=True)
        g = jnp.sum(jnp.where(lane == hv + HV_C, g_all, 0.0), axis=1, keepdims=True)
        if t_len % cs:
            beta = jnp.where(live, beta, 0.0)
            g = jnp.where(live, g, 0.0)
        hsel.append((beta, g))
    ri = lax.broadcasted_iota(i32, (cs, cs), 0)
    ci = lax.broadcasted_iota(i32, (cs, cs), 1)
    eye = ri == ci
    tril = ci <= ri
    strict = ci < ri
    nt = (((1,), (1,)), ((), ()))
    results = []

    chains = []
    for sub in range(nsub):
        rs = slice(sub * cs, (sub + 1) * cs)
        q, k = q_all[rs], k_all[rs]
        kb16 = k.astype(bf16)
        kk = lax.dot_general(kb16, kb16, nt, preferred_element_type=f32)
        qk = lax.dot_general(q.astype(bf16), kb16, nt, preferred_element_type=f32)
        for e in range(rep):
            beta, g = hsel[e][0][rs], hsel[e][1][rs]
            gc = jnp.sum(jnp.where(tril, _col_to_row(g, eye), 0.0), axis=1, keepdims=True)
            gc_row = _col_to_row(gc, eye)
            decay = jnp.where(tril, jnp.exp(jnp.where(tril, gc - gc_row, 0.0)), 0.0)
            pw = jnp.where(strict, -(kk * beta * decay), 0.0)
            chains.append(dict(sub=sub, e=e, q=q, k=k, qk=qk, beta=beta, gc=gc, decay=decay, pw=pw,
                               inv=jnp.where(eye, 1.0, pw)))
    for _ in range(5):
        for ch in chains:
            ch['pw'] = _dot3(ch['pw'], ch['pw'])
        for ch in chains:
            ch['inv'] = ch['inv'] + _dot3(ch['inv'], ch['pw'])
    for ch in chains:
        sub, e, q, k, beta, gc = ch['sub'], ch['e'], ch['q'], ch['k'], ch['beta'], ch['gc']
        e_col = jnp.exp(gc)
        v = v_all[sub * cs:(sub + 1) * cs, e * DV_C:(e + 1) * DV_C]
        rhs = jnp.concatenate([v * beta, k * (beta * e_col)], axis=1)
        sol = _dot3(ch['inv'], rhs)
        gl = jnp.sum(jnp.where(ri[:, 0:1] == cs - 1, gc, 0.0), axis=0, keepdims=True)
        results.append((sub, e, sol[:, :DV_C], sol[:, DV_C:], q * e_col, k * jnp.exp(gl - gc),
                        jnp.where(tril, ch['qk'] * ch['decay'], 0.0), jnp.broadcast_to(jnp.exp(gl), (cs, DV_C))))

    for sub, e, uin, wcum, qdec, ktail, attn, gtot in results:
        rs = slice(sub * cs, (sub + 1) * cs)
        sl = slice(e * DV_C, (e + 1) * DV_C)
        uin_ref[rs, sl] = uin
        wcum_ref[rs, sl] = wcum
        qdec_ref[rs, sl] = qdec
        ktail_ref[rs, sl] = ktail
        attn_ref[rs, e * cs:(e + 1) * cs] = attn
        gtot_ref[rs, sl] = gtot


def _gdn_scan_body(uin_ref, wcum_ref, qdec_ref, ktail_ref, attn_ref, gtot_ref, z_ref, ow_ref, s0_ref,
                   o_ref, sf_ref, s_ref, *, ng):
    c = pl.program_id(2)
    cs = GDN_CHUNK

    @pl.when(c == 0)
    def _():
        s_ref[...] = s0_ref[0]

    tn = (((0,), (0,)), ((), ()))
    sls = [slice(e * DV_C, (e + 1) * DV_C) for e in range(ng)]
    ss = [s_ref[e] for e in range(ng)]
    s16 = [s.astype(bf16) for s in ss]
    us = [uin_ref[:, sls[e]] - jnp.dot(wcum_ref[:, sls[e]].astype(bf16), s16[e], preferred_element_type=f32)
          for e in range(ng)]
    u16 = [u.astype(bf16) for u in us]
    os_ = [jnp.dot(qdec_ref[:, sls[e]].astype(bf16), s16[e], preferred_element_type=f32)
           + jnp.dot(attn_ref[:, e * cs:(e + 1) * cs].astype(bf16), u16[e], preferred_element_type=f32)
           for e in range(ng)]
    states = [ss[e] * jnp.concatenate([gtot_ref[:, sls[e]]] * (DK_C // cs), axis=0)
              + lax.dot_general(ktail_ref[:, sls[e]].astype(bf16), u16[e], tn, preferred_element_type=f32)
              for e in range(ng)]
    outs = []
    for e in range(ng):
        o = os_[e] * lax.rsqrt(jnp.mean(os_[e] * os_[e], axis=-1, keepdims=True) + RMS_EPS) * ow_ref[...]
        z = z_ref[:, sls[e]]
        outs.append(o * (z * _sigmoid(z)))
    for e in range(ng):
        s_ref[e] = states[e]
        o_ref[:, sls[e]] = outs[e]

    @pl.when(c == pl.num_programs(2) - 1)
    def _():
        sf_ref[0] = s_ref[...]


def _gdn_mixer(p, t_len, conv_buf, s0, prm):
    b, tp, wd = p.shape
    cs = GDN_CHUNK
    assert tp % cs == 0 and DK_C == DV_C and DK_C % cs == 0
    n = tp // cs
    rep = HV_C // HK_C
    p2 = p.reshape(b * tp, wd)
    cw = prm['conv_w'].astype(f32)
    cb = jnp.pad(conv_buf.astype(f32), ((0, 0), (SUBLANES - (CONV_W - 1), 0), (0, 0)))
    gp = jnp.zeros((2, LANES), f32).at[0, HV_C:2 * HV_C].set(prm['a_log'].astype(f32))
    gp = gp.at[1, HV_C:2 * HV_C].set(prm['dt_bias'].astype(f32))
    nq = QK_C // DK_C
    vw = rep * DV_C
    nsub = math.gcd(n, 4)
    rows = nsub * cs
    ns = n // nsub
    row = lambda bi, hk, c: bi * ns + c
    hv_out = jax.ShapeDtypeStruct((b * tp, V_C), f32)
    hspec = pl.BlockSpec((rows, vw), lambda bi, hk, c: (row(bi, hk, c), hk))
    uin, wcum, qdec, ktail, attn, gtot = pl.pallas_call(
        functools.partial(_gdn_prep_body, t_len=t_len, nsub=nsub),
        grid=(b, HK_C, ns),
        in_specs=[pl.BlockSpec((rows, DK_C), lambda bi, hk, c: (row(bi, hk, c), hk)),
                  pl.BlockSpec((rows, DK_C), lambda bi, hk, c: (row(bi, hk, c), nq + hk)),
                  pl.BlockSpec((rows, vw), lambda bi, hk, c: (row(bi, hk, c), 2 * QK_C // vw + hk)),
                  pl.BlockSpec((rows, LANES), lambda bi, hk, c: (row(bi, hk, c), (CONV_CH + V_C) // LANES)),
                  pl.BlockSpec((1, SUBLANES, DK_C), lambda bi, hk, c: (bi, 0, hk)),
                  pl.BlockSpec((1, SUBLANES, DK_C), lambda bi, hk, c: (bi, 0, nq + hk)),
                  pl.BlockSpec((1, SUBLANES, vw), lambda bi, hk, c: (bi, 0, 2 * QK_C // vw + hk)),
                  pl.BlockSpec((CONV_W, DK_C), lambda bi, hk, c: (0, hk)),
                  pl.BlockSpec((CONV_W, DK_C), lambda bi, hk, c: (0, nq + hk)),
                  pl.BlockSpec((CONV_W, vw), lambda bi, hk, c: (0, 2 * QK_C // vw + hk)),
                  pl.BlockSpec((2, LANES), lambda bi, hk, c: (0, 0))],
        out_specs=[hspec, hspec, hspec, hspec,
                   pl.BlockSpec((rows, rep * cs), lambda bi, hk, c: (row(bi, hk, c), hk)), hspec],
        out_shape=[hv_out, hv_out, hv_out, hv_out, jax.ShapeDtypeStruct((b * tp, HK_C * rep * cs), f32), hv_out],
        scratch_shapes=[pltpu.VMEM((SUBLANES, DK_C), f32), pltpu.VMEM((SUBLANES, DK_C), f32),
                        pltpu.VMEM((SUBLANES, vw), f32)],
        compiler_params=_cparams("parallel", "parallel", "arbitrary"),
        name="gdn_prep",
    )(p2, p2, p2, p2, cb, cb, cb, cw, cw, cw, gp)

    ng = 8
    gw = ng * DV_C
    gspec = pl.BlockSpec((cs, gw), lambda bi, hg, c: (bi * n + c, hg))
    sspec = pl.BlockSpec((1, ng, DK_C, DV_C), lambda bi, hg, c: (bi, hg, 0, 0))
    o, sf = pl.pallas_call(
        functools.partial(_gdn_scan_body, ng=ng),
        grid=(b, HV_C // ng, n),
        in_specs=[gspec, gspec, gspec, gspec,
                  pl.BlockSpec((cs, ng * cs), lambda bi, hg, c: (bi * n + c, hg)), gspec,
                  pl.BlockSpec((cs, gw), lambda bi, hg, c: (bi * n + c, CONV_CH // gw + hg)),
                  pl.BlockSpec((1, DV_C), lambda bi, hg, c: (0, 0)), sspec],
        out_specs=[gspec, sspec],
        out_shape=[hv_out, jax.ShapeDtypeStruct((b, HV_C, DK_C, DV_C), f32)],
        scratch_shapes=[pltpu.VMEM((ng, DK_C, DV_C), f32)],
        compiler_params=_cparams("parallel", "parallel", "arbitrary"),
        name="gdn_scan",
    )(uin, wcum, qdec, ktail, attn, gtot, p2, prm['onorm_w'].astype(f32).reshape(1, DV_C), s0.astype(f32))
    return o, sf


def _pad_rows(x, rows):
    return jnp.pad(x, ((0, rows - x.shape[0]), (0, 0)))


def kernel(x_prompt, x_sample, cache_k_0, cache_v_0, state_wkv_0, state_shift_0, state_gdn_1, state_conv_1, cache_k_2, cache_v_2, state_wkv_2, state_shift_2, state_gdn_3, state_conv_3, page_table, rel_bias, norm_mix_0, w_in_0, mu_0, w0_0, w2_0, a0_0, a2_0, g2_0, kk_scale_0, ka_mix_0, rk_bonus_0, lnx_w_0, lnx_b_0, w_out_0, norm_ffn_0, peer_wq_0, peer_keys_0, peer_u_0, peer_v_0, norm_mix_1, w_in_1, conv_w_1, a_log_1, dt_bias_1, onorm_w_1, w_out_1, norm_ffn_1, peer_wq_1, peer_keys_1, peer_u_1, peer_v_1, norm_mix_2, w_in_2, mu_2, w0_2, w2_2, a0_2, a2_2, v0_2, v2_2, g2_2, kk_scale_2, ka_mix_2, rk_bonus_2, lnx_w_2, lnx_b_2, w_out_2, norm_ffn_2, peer_wq_2, peer_keys_2, peer_u_2, peer_v_2, norm_mix_3, w_in_3, conv_w_3, a_log_3, dt_bias_3, onorm_w_3, w_out_3, norm_ffn_3, peer_wq_3, peer_keys_3, peer_u_3, peer_v_3, norm_final):
    mix_prm = [
        dict(norm=norm_mix_0, w_in=w_in_0, mu=mu_0, w0=w0_0, w2=w2_0, a0=a0_0, a2=a2_0, g2=g2_0,
             kk_scale=kk_scale_0, ka_mix=ka_mix_0, rk_bonus=rk_bonus_0, lnx_w=lnx_w_0, lnx_b=lnx_b_0,
             w_out=w_out_0),
        dict(norm=norm_mix_1, w_in=w_in_1, conv_w=conv_w_1, a_log=a_log_1, dt_bias=dt_bias_1,
             onorm_w=onorm_w_1, w_out=w_out_1),
        dict(norm=norm_mix_2, w_in=w_in_2, mu=mu_2, w0=w0_2, w2=w2_2, a0=a0_2, a2=a2_2, v0=v0_2, v2=v2_2,
             g2=g2_2, kk_scale=kk_scale_2, ka_mix=ka_mix_2, rk_bonus=rk_bonus_2, lnx_w=lnx_w_2,
             lnx_b=lnx_b_2, w_out=w_out_2),
        dict(norm=norm_mix_3, w_in=w_in_3, conv_w=conv_w_3, a_log=a_log_3, dt_bias=dt_bias_3,
             onorm_w=onorm_w_3, w_out=w_out_3),
    ]
    ffn_prm = [
        (norm_ffn_0, peer_wq_0, peer_keys_0, peer_u_0, peer_v_0),
        (norm_ffn_1, peer_wq_1, peer_keys_1, peer_u_1, peer_v_1),
        (norm_ffn_2, peer_wq_2, peer_keys_2, peer_u_2, peer_v_2),
        (norm_ffn_3, peer_wq_3, peer_keys_3, peer_u_3, peer_v_3),
    ]
    layer_state = [
        (cache_k_0, cache_v_0, state_wkv_0, state_shift_0),
        (state_gdn_1, state_conv_1),
        (cache_k_2, cache_v_2, state_wkv_2, state_shift_2),
        (state_gdn_3, state_conv_3),
    ]
    bp, sp, d = x_prompt.shape
    bs, ts, _ = x_sample.shape
    ns = bs * ts
    ns_mm = -(-ns // 16) * 16
    ns_peer = -(-ns // LANES) * LANES
    xp = x_prompt.reshape(bp * sp, d)
    xs = x_sample.reshape(ns, d)
    depth = len(mix_prm)
    vf_p = vf_s = None
    new = []
    for i in range(depth):
        prm = mix_prm[i]
        xs_pad = _pad_rows(xs, ns_mm)
        pp = _matmul(xp, prm['w_in'], gain=prm['norm'])
        ps = _matmul(xs_pad, prm['w_in'], gain=prm['norm'])[:ns]
        if i % 2 == 0:
            ck, cv, wkv_in, shift_in = layer_state[i]
            first = i == 0
            rw = prm['w_in'].shape[1] - 3 * C_A
            o_a = _moba_prompt(pp, rel_bias, bp, sp)
            p_rw = pp[:, 3 * C_A:].reshape(bp, sp, rw)
            p_prev = jnp.concatenate([jnp.zeros((bp, 1, rw), f32), p_rw[:, :-1]], axis=1)
            o_b, wkv_p, vf_p = _rwkv7(p_rw, p_prev, jnp.zeros((bp, H_B, DH_B, DH_B), f32), vf_p, prm, first)
            cat = jnp.concatenate([o_a, o_b.reshape(bp * sp, C_B)], axis=-1)
            xp_new = _matmul(cat, prm['w_out'], res=xp)
            kp = pp[:, C_A:2 * C_A].reshape(bp, sp // PAGE_SIZE, PAGE_SIZE, H_A, DH_A)
            vp = pp[:, 2 * C_A:3 * C_A].reshape(bp, sp // PAGE_SIZE, PAGE_SIZE, H_A, DH_A)
            hp_last = _rms(xp.reshape(bp, sp, d)[:, -1], prm['norm'])
            qs = ps[:, :C_A].reshape(bs, ts, H_A, DH_A)
            ks = ps[:, C_A:2 * C_A].reshape(bs, ts, H_A, DH_A)
            vs = ps[:, 2 * C_A:3 * C_A].reshape(bs, ts, H_A, DH_A)
            o_as = _moba_sample(qs, ks, vs, ck, cv, page_table, rel_bias).reshape(ns, C_A)
            ps_rw = ps[:, 3 * C_A:].reshape(bs, ts, rw)
            prev0 = _matmul(_pad_rows(shift_in, ns_mm), prm['w_in'], col0=3 * C_A)[:bs]
            ps_prev = jnp.concatenate([prev0[:, None], ps_rw[:, :-1]], axis=1)
            o_bs, wkv_s, vf_s = _rwkv7(ps_rw, ps_prev, wkv_in, vf_s, prm, first)
            cat_s = jnp.concatenate([o_as, o_bs.reshape(ns, C_B)], axis=-1)
            xs_new = _matmul(_pad_rows(cat_s, ns_mm), prm['w_out'], res=xs_pad)[:ns]
            hs_last = _rms(xs.reshape(bs, ts, d)[:, -1], prm['norm'])
            new.append((kp, vp, ks, vs, wkv_p, wkv_s, hp_last, hs_last))
        else:
            gdn_in, conv_in = layer_state[i]
            wd = prm['w_in'].shape[1]

            def gdn(p3, conv_buf, s0):
                bb, tt, _ = p3.shape
                tpad = -(-tt // GDN_CHUNK) * GDN_CHUNK
                o, sf = _gdn_mixer(jnp.pad(p3, ((0, 0), (0, tpad - tt), (0, 0))), tt, conv_buf, s0, prm)
                conv = jnp.concatenate([conv_buf.astype(f32), p3[..., :CONV_CH]], axis=1)[:, -(CONV_W - 1):]
                return o.reshape(bb, tpad, V_C)[:, :tt].reshape(bb * tt, V_C), conv, sf

            o_p, conv_p, gdn_p = gdn(pp.reshape(bp, sp, wd), jnp.zeros((bp, CONV_W - 1, CONV_CH), f32),
                                     jnp.zeros((bp, HV_C, DK_C, DV_C), f32))
            o_s, conv_s, gdn_s = gdn(ps.reshape(bs, ts, wd), conv_in, gdn_in)
            xp_new = _matmul(o_p, prm['w_out'], res=xp)
            xs_new = _matmul(_pad_rows(o_s, ns_mm), prm['w_out'], res=xs_pad)[:ns]
            new.append((gdn_p, gdn_s, conv_p, conv_s))
        xp, xs = xp_new, xs_new
        g_ffn, wq, sub_keys, u_tab, v_tab = ffn_prm[i]
        wq_bf, u_bf, v_bf = wq.astype(bf16), u_tab.astype(bf16), v_tab.astype(bf16)
        keys = sub_keys.astype(f32)
        xp = _peer(xp, g_ffn, wq_bf, keys, u_bf, v_bf)
        xs = _peer(_pad_rows(xs, ns_peer), g_ffn, wq_bf, keys, u_bf, v_bf)[:ns]
    y_prompt = _rms(xp, norm_final).reshape(bp, sp, d)
    y_sample = _rms(xs, norm_final).reshape(bs, ts, d)
    out = [y_prompt, y_sample]
    for layer in new:
        out.extend(layer)
    return tuple(out)
```

```python
import functools
import math

import jax
import jax.numpy as jnp
from jax import lax
from jax.experimental import pallas as pl
from jax.experimental.pallas import tpu as pltpu

f32 = jnp.float32
bf16 = jnp.bfloat16
i32 = jnp.int32

RMS_EPS = 1e-6
NEG_INF = -1e30

PAGE_SIZE = 128
H_A, DH_A = 8, 128
C_A = H_A * DH_A
MOBA_BLOCK, MOBA_TOPK = 256, 3
NUM_BUCKETS, MAX_DISTANCE = 32, 128
H_B, DH_B = 16, 64
C_B = H_B * DH_B
LORA_W, LORA_A, LORA_G, LORA_V = 64, 64, 160, 32
LNX_EPS = 64e-5
HK_C, HV_C, DK_C, DV_C = 16, 32, 128, 128
QK_C, V_C = HK_C * DK_C, HV_C * DV_C
CONV_W = 4
CONV_CH = 2 * QK_C + V_C
GDN_CHUNK = 64
PEER_HEADS, N_KEYS, PEER_DK, PEER_TOPK = 8, 128, 256, 16
N_EXPERTS = N_KEYS * N_KEYS

LANES = 128
SUBLANES = 8
VMEM_LIMIT = 56 * 1024 * 1024


def _cparams(*sem):
    return pltpu.CompilerParams(dimension_semantics=sem, vmem_limit_bytes=VMEM_LIMIT)


def _rms(x, g):
    return x * lax.rsqrt(jnp.mean(x * x, axis=-1, keepdims=True) + RMS_EPS) * g


def _mm_body(*refs, norm, has_res):
    if has_res:
        x_ref, g_ref, w_ref, res_ref, o_ref, h_ref = refs
    else:
        x_ref, g_ref, w_ref, o_ref, h_ref = refs

    @pl.when(pl.program_id(1) == 0)
    def _():
        x = x_ref[...]
        if norm:
            x = _rms(x, g_ref[...])
        h_ref[...] = x.astype(bf16)

    acc = jnp.dot(h_ref[...], w_ref[...], preferred_element_type=f32)
    if has_res:
        acc = res_ref[...] + acc
    o_ref[...] = acc


def _matmul(x, w, gain=None, res=None, col0=0, ncols=None, tn=512):
    m, k = x.shape
    ncols = w.shape[1] - col0 if ncols is None else ncols
    assert col0 % tn == 0 and m % 16 == 0 and w.dtype == bf16
    tm = min(m, 1024 if k <= 2048 else 512)
    assert m % tm == 0
    nj = pl.cdiv(ncols, tn)
    j0 = col0 // tn
    norm = gain is not None
    g = (gain if norm else jnp.ones((k,), f32)).reshape(1, k)
    in_specs = [pl.BlockSpec((tm, k), lambda i, j: (i, 0)),
                pl.BlockSpec((1, k), lambda i, j: (0, 0)),
                pl.BlockSpec((k, tn), lambda i, j: (0, j + j0))]
    args = [x, g, w]
    if res is not None:
        in_specs.append(pl.BlockSpec((tm, tn), lambda i, j: (i, j)))
        args.append(res)
    return pl.pallas_call(
        functools.partial(_mm_body, norm=norm, has_res=res is not None),
        grid=(m // tm, nj),
        in_specs=in_specs,
        out_specs=pl.BlockSpec((tm, tn), lambda i, j: (i, j)),
        out_shape=jax.ShapeDtypeStruct((m, ncols), f32),
        scratch_shapes=[pltpu.VMEM((tm, k), bf16)],
        compiler_params=_cparams("parallel", "arbitrary"),
        name="matmul",
    )(*args)


def _t5_bucket(dist):
    n = jnp.maximum(dist, 0)
    max_exact = NUM_BUCKETS // 2
    ratio = jnp.log(jnp.maximum(n, 1).astype(f32) / max_exact) / math.log(MAX_DISTANCE / max_exact)
    large = max_exact + (ratio * (NUM_BUCKETS - max_exact)).astype(i32)
    return jnp.where(n < max_exact, n, jnp.minimum(large, NUM_BUCKETS - 1))


def _moba_prompt_body(q_ref, k_ref, v_ref, bias_ref, o_ref, km_ref, *, nb):
    qb = pl.program_id(2)
    blk = MOBA_BLOCK
    s_len = nb * blk

    @pl.when(qb == 0)
    def _():
        k = k_ref[...]
        km = jnp.mean(k.reshape(nb, blk, DH_A), axis=1)
        km_ref[...] = jnp.zeros_like(km_ref)
        km_ref[0:nb, :] = km

    q = q_ref[...]
    gate = lax.dot_general(q, km_ref[...], (((1,), (1,)), ((), ())), preferred_element_type=f32,
                           precision=lax.Precision.HIGHEST)
    lane = lax.broadcasted_iota(i32, gate.shape, 1)
    past = lane < qb
    g = jnp.where(past, gate, NEG_INF)
    sel = jnp.zeros(gate.shape, f32)
    for _ in range(MOBA_TOPK):
        m = jnp.max(g, axis=1, keepdims=True)
        idx = jnp.min(jnp.where(g == m, lane, LANES), axis=1, keepdims=True)
        pick = lane == idx
        sel = jnp.where(pick, jnp.where(past, 1.0, 0.0), sel)
        g = jnp.where(pick, -jnp.inf, g)
    sel = jnp.where(lane == qb, 1.0, sel)
    selneg = jnp.where(sel > 0.0, 0.0, NEG_INF).astype(bf16)
    blk_of_key = lax.broadcasted_iota(i32, (LANES, s_len), 1) // blk
    blockind = jnp.where(blk_of_key == lax.broadcasted_iota(i32, (LANES, s_len), 0), 1.0, 0.0).astype(bf16)
    maskadd = jnp.dot(selneg, blockind, preferred_element_type=f32)

    s = lax.dot_general(q.astype(bf16), k_ref[...].astype(bf16), (((1,), (1,)), ((), ())),
                        preferred_element_type=f32) * (DH_A ** -0.5)
    off = pl.multiple_of((nb - 1 - qb) * blk, blk)
    s = s + bias_ref[0, :, pl.ds(off, s_len)] + maskadd
    kidx = lax.broadcasted_iota(i32, s.shape, 1)
    qidx = qb * blk + lax.broadcasted_iota(i32, s.shape, 0)
    s = jnp.where(kidx <= qidx, s, NEG_INF)
    m = jnp.max(s, axis=1, keepdims=True)
    p = jnp.exp(s - m)
    l = jnp.sum(p, axis=1, keepdims=True)
    o = jnp.dot(p.astype(bf16), v_ref[...].astype(bf16), preferred_element_type=f32)
    o_ref[...] = o / l


def _bias_strip_body(f_ref, o_ref, *, width):
    row = jnp.broadcast_to(f_ref[0], (MOBA_BLOCK, f_ref.shape[2]))
    o_ref[0] = pltpu.roll(row, 0, 1, stride=1, stride_axis=0)[:, :width]


def _bias_strip(rel_bias, nb):
    blk = MOBA_BLOCK
    width = (2 * nb - 1) * blk
    padded = width + blk
    x = jnp.arange(padded)
    dist = jnp.where(x < width, (nb - 1) * blk - x, (nb - 1) * blk + padded - x)
    prof = rel_bias.astype(f32)[_t5_bucket(dist)].T.reshape(H_A, 1, padded)
    return pl.pallas_call(
        functools.partial(_bias_strip_body, width=width),
        grid=(H_A,),
        in_specs=[pl.BlockSpec((1, 1, padded), lambda h: (h, 0, 0))],
        out_specs=pl.BlockSpec((1, blk, width), lambda h: (h, 0, 0)),
        out_shape=jax.ShapeDtypeStruct((H_A, blk, width), f32),
        compiler_params=_cparams("parallel"),
        name="moba_bias_strip",
    )(prof)


def _moba_prompt(p, rel_bias, b, s_len):
    blk = MOBA_BLOCK
    assert s_len % blk == 0
    nb = s_len // blk
    strip = _bias_strip(rel_bias, nb)
    hd = C_A // DH_A
    return pl.pallas_call(
        functools.partial(_moba_prompt_body, nb=nb),
        grid=(b, H_A, nb),
        in_specs=[pl.BlockSpec((blk, DH_A), lambda bi, h, qb: (bi * nb + qb, h)),
                  pl.BlockSpec((s_len, DH_A), lambda bi, h, qb: (bi, hd + h)),
                  pl.BlockSpec((s_len, DH_A), lambda bi, h, qb: (bi, 2 * hd + h)),
                  pl.BlockSpec((1, blk, (2 * nb - 1) * blk), lambda bi, h, qb: (h, 0, 0))],
        out_specs=pl.BlockSpec((blk, DH_A), lambda bi, h, qb: (bi * nb + qb, h)),
        out_shape=jax.ShapeDtypeStruct((b * s_len, C_A), f32),
        scratch_shapes=[pltpu.VMEM((LANES, DH_A), f32)],
        compiler_params=_cparams("parallel", "parallel", "arbitrary"),
        name="moba_prompt",
    )(p, p, p, strip)


def _wkv_body(r_ref, w_ref, k_ref, kk_ref, b_ref, v_ref, s0_ref, y_ref, sf_ref, s_ref, *, tc):
    @pl.when(pl.program_id(1) == 0)
    def _():
        s_ref[...] = s0_ref[0]

    half = LANES // 2

    def step(t, c):
        r, w, k, kk, b = r_ref[0, t], w_ref[0, t], k_ref[0, t], kk_ref[0, t], b_ref[0, t]
        for i in range(DH_B):
            s = s_ref[i]
            vi = v_ref[0, t, pl.ds(i, 1), :]
            t1 = jnp.sum(s * kk, axis=0, keepdims=True)
            sa = -(t1 + pltpu.roll(t1, half, 1))
            s = s * w + sa * b + vi * k
            s_ref[i] = s
            t2 = jnp.sum(s * r, axis=0, keepdims=True)
            y_ref[0, t, pl.ds(i, 1), :] = t2 + pltpu.roll(t2, half, 1)
        return c

    lax.fori_loop(0, tc, step, 0)

    @pl.when(pl.program_id(1) == pl.num_programs(1) - 1)
    def _():
        sf_ref[0] = s_ref[...]


def _wkv_scan(r, w, k, kk, b, v, s0):
    bsz, t, h, n = r.shape
    assert n == DH_B and (bsz * h) % (LANES // 2) == 0
    grp = bsz * h // (LANES // 2)
    bg = bsz // grp
    hn = n // 2

    def vec(z):
        z = z.reshape(grp, bg, t, h, 2, hn)
        return z.transpose(0, 2, 5, 4, 1, 3).reshape(grp, t, hn, LANES)

    def row(z):
        z = z.reshape(grp, bg, t, h, n).transpose(0, 2, 4, 1, 3).reshape(grp, t, n, LANES // 2)
        return jnp.concatenate([z, z], axis=-1)

    s0k = s0.astype(f32).reshape(grp, bg, h, n, 2, hn).transpose(0, 3, 5, 4, 1, 2).reshape(grp, n, hn, LANES)
    tc = math.gcd(t, 64)
    vspec = pl.BlockSpec((1, tc, hn, LANES), lambda g, c: (g, c, 0, 0))
    rspec = pl.BlockSpec((1, tc, n, LANES), lambda g, c: (g, c, 0, 0))
    sspec = pl.BlockSpec((1, n, hn, LANES), lambda g, c: (g, 0, 0, 0))
    y, sf = pl.pallas_call(
        functools.partial(_wkv_body, tc=tc),
        grid=(grp, t // tc),
        in_specs=[vspec] * 5 + [rspec, sspec],
        out_specs=[rspec, sspec],
        out_shape=[jax.ShapeDtypeStruct((grp, t, n, LANES), f32), jax.ShapeDtypeStruct((grp, n, hn, LANES), f32)],
        scratch_shapes=[pltpu.VMEM((n, hn, LANES), f32)],
        compiler_params=_cparams("parallel", "arbitrary"),
        name="wkv_scan",
    )(vec(r), vec(w), vec(k), vec(kk), vec(b), row(v), s0k)
    y = y[..., :LANES // 2].reshape(grp, t, n, bg, h).transpose(0, 3, 1, 4, 2).reshape(bsz, t, h, n)
    sf = sf.reshape(grp, n, hn, 2, bg, h).transpose(0, 4, 5, 1, 3, 2).reshape(bsz, h, n, n)
    return y, sf


def _top_rows(vs, k, payloads=None):
    rows = lax.broadcasted_iota(i32, vs[0].shape, 0)
    n = vs[0].shape[0]
    vs = list(vs)
    vals = [[] for _ in vs]
    outs = [[] for _ in vs]
    for _ in range(k):
        ms = [jnp.max(v, axis=0, keepdims=True) for v in vs]
        ids = [jnp.min(jnp.where(v == m, rows, n), axis=0, keepdims=True) for v, m in zip(vs, ms)]
        picks = [rows == i for i in ids]
        vs = [jnp.where(p, -jnp.inf, v) for p, v in zip(picks, vs)]
        for j in range(len(vs)):
            vals[j].append(ms[j])
            outs[j].append(ids[j].astype(f32) if payloads is None else
                           jnp.sum(jnp.where(picks[j], payloads[j], 0.0), axis=0, keepdims=True))
    return [(jnp.concatenate(v, axis=0), jnp.concatenate(o, axis=0)) for v, o in zip(vals, outs)]


_PAIR_CNT = [PEER_TOPK // (m + 1) for m in range(PEER_TOPK)]
_PAIR_PAD = -sum(_PAIR_CNT) % SUBLANES


def _route_body(x_ref, g_ref, wq_ref, keys_ref, xn_ref, i1_ref, i2_ref, gate_ref, q_scr, e_scr, w_scr, *, tt):
    xn = _rms(x_ref[...], g_ref[...]).astype(bf16)
    xn_ref[...] = xn
    q_scr[...] = jnp.dot(xn, wq_ref[...], preferred_element_type=f32)
    hw = PEER_DK // 2
    nt = (((1,), (1,)), ((), ()))

    def head(h, c):
        row = pl.multiple_of(h * PEER_TOPK, PEER_TOPK)
        nsub = tt // LANES
        sts = []
        for sub in range(nsub):
            for p in range(2):
                col = pl.multiple_of(h * PEER_DK + p * hw, hw)
                qp = q_scr[sub * LANES:(sub + 1) * LANES, pl.ds(col, hw)]
                sts.append(lax.dot_general(keys_ref[p], qp, nt, preferred_element_type=f32))
        tops = _top_rows(sts, PEER_TOPK)
        cands, ceids = [], []
        for sub in range(nsub):
            (sv1, si1), (sv2, si2) = tops[2 * sub], tops[2 * sub + 1]
            si1 = si1 * float(N_KEYS)
            cands.append(jnp.concatenate([sv1[m:m + 1] + sv2[:_PAIR_CNT[m]] for m in range(PEER_TOPK)]
                                         + [jnp.full((_PAIR_PAD, LANES), -jnp.inf, f32)], axis=0))
            ceids.append(jnp.concatenate([si1[m:m + 1] + si2[:_PAIR_CNT[m]] for m in range(PEER_TOPK)]
                                         + [jnp.zeros((_PAIR_PAD, LANES), f32)], axis=0))
        joint = _top_rows(cands, PEER_TOPK, payloads=ceids)
        for sub in range(nsub):
            cv, eid = joint[sub]
            ex = jnp.exp(cv - cv[0:1])
            e_scr[pl.ds(row, PEER_TOPK), sub * LANES:(sub + 1) * LANES] = eid
            w_scr[pl.ds(row, PEER_TOPK), sub * LANES:(sub + 1) * LANES] = ex / jnp.sum(ex, axis=0, keepdims=True)
        return c

    lax.fori_loop(0, PEER_HEADS, head, 0)
    eid = e_scr[...].T
    i1 = jnp.floor(eid * (1.0 / N_KEYS))
    i1_ref[...] = i1
    i2_ref[...] = eid - i1 * float(N_KEYS)
    gate_ref[...] = w_scr[...].T


def _peer_route(x, gain, wq_bf, keys):
    t, d = x.shape
    tt = min(t, 256)
    assert t % tt == 0 and tt % LANES == 0
    nsel = PEER_HEADS * PEER_TOPK
    sel = jax.ShapeDtypeStruct((t, nsel), f32)
    sspec = pl.BlockSpec((tt, nsel), lambda i: (i, 0))
    return pl.pallas_call(
        functools.partial(_route_body, tt=tt),
        grid=(t // tt,),
        in_specs=[pl.BlockSpec((tt, d), lambda i: (i, 0)),
                  pl.BlockSpec((1, d), lambda i: (0, 0)),
                  pl.BlockSpec(wq_bf.shape, lambda i: (0, 0)),
                  pl.BlockSpec(keys.shape, lambda i: (0, 0, 0))],
        out_specs=[pl.BlockSpec((tt, d), lambda i: (i, 0)), sspec, sspec, sspec],
        out_shape=[jax.ShapeDtypeStruct((t, d), bf16), sel, sel, sel],
        scratch_shapes=[pltpu.VMEM((tt, wq_bf.shape[1]), f32), pltpu.VMEM((nsel, tt), f32),
                        pltpu.VMEM((nsel, tt), f32)],
        compiler_params=_cparams("parallel"),
        name="peer_route",
    )(x, gain.reshape(1, d), wq_bf, keys)


def _gate_build_body(i1_ref, i2_ref, g_ref, o_ref, *, tb):
    keyrow = lax.broadcasted_iota(i32, (N_KEYS, i1_ref.shape[1]), 0).astype(f32)
    nt = (((1,), (1,)), ((), ()))

    def toks(i, c):
        t0 = pl.multiple_of(i * SUBLANES, SUBLANES)
        i1 = i1_ref[pl.ds(t0, SUBLANES), :]
        i2 = i2_ref[pl.ds(t0, SUBLANES), :]
        g = g_ref[pl.ds(t0, SUBLANES), :]
        ghi = g.astype(bf16).astype(f32)
        glo = g - ghi
        ops = []
        for r in range(SUBLANES):
            a = jnp.where(keyrow == i1[r:r + 1], 1.0, 0.0).astype(bf16)
            m2 = keyrow == i2[r:r + 1]
            ops.append((a, jnp.where(m2, ghi[r:r + 1], 0.0).astype(bf16),
                        jnp.where(m2, glo[r:r + 1], 0.0).astype(bf16)))
        gs = [lax.dot_general(a, bhi, nt, preferred_element_type=f32)
              + lax.dot_general(a, blo, nt, preferred_element_type=f32) for a, bhi, blo in ops]
        for r in range(SUBLANES):
            o_ref[i, :, r, :] = gs[r]
        return c

    lax.fori_loop(0, tb // SUBLANES, toks, 0)


def _gate_build(i1, i2, gate):
    t, nsel = i1.shape
    tb = min(t, 32)
    assert t % tb == 0 and tb % SUBLANES == 0
    spec = pl.BlockSpec((tb, nsel), lambda i: (i, 0))
    return pl.pallas_call(
        functools.partial(_gate_build_body, tb=tb),
        grid=(t // tb,),
        in_specs=[spec, spec, spec],
        out_specs=pl.BlockSpec((tb // SUBLANES, N_KEYS, SUBLANES, N_KEYS), lambda i: (i, 0, 0, 0)),
        out_shape=jax.ShapeDtypeStruct((t // SUBLANES, N_KEYS, SUBLANES, N_KEYS), f32),
        compiler_params=_cparams("parallel"),
        name="peer_gate_build",
    )(i1, i2, gate)


def _peer_mix_body(x_ref, g_ref, u_ref, v_ref, o_ref, *, na, tt):
    j = pl.program_id(1)

    @pl.when(j == 0)
    def _():
        o_ref[...] = jnp.zeros_like(o_ref)

    h = lax.dot_general(x_ref[...], u_ref[...], (((1,), (1,)), ((), ())), preferred_element_type=f32)
    parts = []
    for a in range(na):
        ha = h[:, a * N_KEYS:(a + 1) * N_KEYS]
        act = 0.5 * ha * (1.0 + lax.erf(ha * (2.0 ** -0.5)))
        parts.append((g_ref[:, a].reshape(tt, N_KEYS) * act).astype(bf16))
    p = jnp.concatenate(parts, axis=1)
    o_ref[...] += jnp.dot(p, v_ref[...], preferred_element_type=f32)


def _peer_mix(xn, g4, u_bf, v_bf):
    t, d = xn.shape
    tt = min(t, 1024)
    na = 4
    et = na * N_KEYS
    assert t % tt == 0 and tt % SUBLANES == 0
    return pl.pallas_call(
        functools.partial(_peer_mix_body, na=na, tt=tt),
        grid=(t // tt, N_EXPERTS // et),
        in_specs=[pl.BlockSpec((tt, d), lambda i, j: (i, 0)),
                  pl.BlockSpec((tt // SUBLANES, na, SUBLANES, N_KEYS), lambda i, j: (i, j, 0, 0)),
                  pl.BlockSpec((et, d), lambda i, j: (j, 0)),
                  pl.BlockSpec((et, d), lambda i, j: (j, 0))],
        out_specs=pl.BlockSpec((tt, d), lambda i, j: (i, 0)),
        out_shape=jax.ShapeDtypeStruct((t, d), f32),
        compiler_params=_cparams("parallel", "arbitrary"),
        name="peer_mix",
    )(xn, g4, u_bf, v_bf)


def _peer(x, gain, wq_bf, keys, u_bf, v_bf):
    xn, i1, i2, gate = _peer_route(x, gain, wq_bf, keys)
    return x + _peer_mix(xn, _gate_build(i1, i2, gate), u_bf, v_bf)


def _l2norm(x, eps):
    return x * lax.rsqrt(jnp.sum(x * x, axis=-1, keepdims=True) + eps)


def _rwkv7(p_cur, p_prev, s0, v_first, prm, first):
    b, t, _ = p_cur.shape
    m = (p_cur + (p_prev - p_cur) * prm['mu']).astype(f32)
    r, k, v = m[..., :C_B], m[..., C_B:2 * C_B], m[..., 2 * C_B:3 * C_B]
    o = 3 * C_B
    wl = m[..., o:o + LORA_W]
    o += LORA_W
    al = m[..., o:o + LORA_A]
    o += LORA_A
    gl = m[..., o:o + LORA_G]
    o += LORA_G
    w_log = -jax.nn.softplus(-(prm['w0'] + jnp.tanh(wl) @ prm['w2'])) - 0.5
    decay = jnp.exp(-jnp.exp(w_log.astype(f32)))
    a = jax.nn.sigmoid((prm['a0'] + al @ prm['a2']).astype(f32))
    g = (jax.nn.sigmoid(gl) @ prm['g2']).astype(f32)
    if first:
        v_first = v
    else:
        vl = m[..., o:o + LORA_V]
        v = v + (v_first - v) * jax.nn.sigmoid((prm['v0'] + vl @ prm['v2']).astype(f32))

    def hd(z):
        return z.reshape(b, t, H_B, DH_B)

    kk = _l2norm(hd(k * prm['kk_scale']), 1e-24)
    k = k * (1.0 + (a - 1.0) * prm['ka_mix'])
    r_h, k_h, v_h = hd(r), hd(k), hd(v)
    b_h = kk * hd(a)
    y, s_fin = _wkv_scan(r_h, hd(decay), k_h, kk, b_h, v_h, s0)
    mu = jnp.mean(y, axis=-1, keepdims=True)
    var = jnp.mean(jnp.square(y - mu), axis=-1, keepdims=True)
    y = ((y - mu) * lax.rsqrt(var + LNX_EPS)).reshape(b, t, C_B) * prm['lnx_w'] + prm['lnx_b']
    bonus = jnp.sum(r_h * k_h * prm['rk_bonus'], axis=-1, keepdims=True) * v_h
    y = (y + bonus.reshape(b, t, C_B)) * g
    return y, s_fin.astype(s0.dtype), v_first


def _select_blocks(q, kmean, qblk):
    n_sel = min(MOBA_TOPK, kmean.shape[2])
    gate = jnp.einsum('bhqd,bhjd->bhqj', q.astype(f32), kmean, precision=lax.Precision.HIGHEST)
    past = jnp.arange(kmean.shape[2])[None, :] < qblk[:, None]
    _, idx = lax.top_k(jnp.where(past, gate, NEG_INF), n_sel)
    return idx, idx < qblk[:, None]


def _kmean_body(pt_ref, *refs):
    o_ref = refs[-1]
    acc = jnp.sum(refs[0][0], axis=0, keepdims=True)
    for r in refs[1:-1]:
        acc = acc + jnp.sum(r[0], axis=0, keepdims=True)
    o_ref[0, 0] = acc * (1.0 / MOBA_BLOCK)


def _paged_block_means(cache, page_table):
    bd, n_pages = page_table.shape
    ppb = MOBA_BLOCK // PAGE_SIZE
    nbp = n_pages // ppb
    c = cache.shape[-1]
    specs = [pl.BlockSpec((1, PAGE_SIZE, c), functools.partial(lambda b, j, pt, r: (pt[b, j * ppb + r], 0, 0), r=r))
             for r in range(ppb)]
    return pl.pallas_call(
        _kmean_body,
        grid_spec=pltpu.PrefetchScalarGridSpec(
            num_scalar_prefetch=1, grid=(bd, nbp), in_specs=specs,
            out_specs=pl.BlockSpec((1, 1, 1, c), lambda b, j, pt: (b, j, 0, 0))),
        out_shape=jax.ShapeDtypeStruct((bd, nbp, 1, c), f32),
        compiler_params=_cparams("parallel", "arbitrary"),
        name="moba_block_means",
    )(page_table, *([cache] * ppb))


def _moba_decode_body(pp_ref, lp_ref, ok_ref, q_ref, kn_ref, vn_ref, ob_ref, k_ref, v_ref, b_ref, o_ref,
                      m_ref, l_ref, acc_ref, *, ppb):
    b, h, s = pl.program_id(0), pl.program_id(1), pl.program_id(2)
    scale = DH_A ** -0.5
    q = q_ref[0]

    @pl.when(s == 0)
    def _():
        m_ref[...] = jnp.sum(q * kn_ref[0], axis=1, keepdims=True) * scale + ob_ref[0][:, 0:1]
        l_ref[...] = jnp.ones_like(l_ref)
        acc_ref[...] = vn_ref[0]

    sc = lax.dot_general(q.astype(bf16), k_ref[0].astype(bf16), (((1,), (1,)), ((), ())),
                         preferred_element_type=f32) * scale + b_ref[0, 0]
    sc = jnp.where(ok_ref[b, h, s // ppb] != 0, sc, NEG_INF)
    m_old = m_ref[...]
    m_new = jnp.maximum(m_old, jnp.max(sc, axis=1, keepdims=True))
    alpha = jnp.exp(m_old - m_new)
    p = jnp.exp(sc - m_new)
    m_ref[...] = m_new
    l_ref[...] = alpha * l_ref[...] + jnp.sum(p, axis=1, keepdims=True)
    acc_ref[...] = alpha * acc_ref[...] + jnp.dot(p.astype(bf16), v_ref[0].astype(bf16), preferred_element_type=f32)

    @pl.when(s == pl.num_programs(2) - 1)
    def _():
        o_ref[0] = acc_ref[...] / l_ref[...]


def _moba_sample(q, k, v, cache_k, cache_v, page_table, rel_bias):
    bd, tn = q.shape[:2]
    n_pages = page_table.shape[1]
    past = n_pages * PAGE_SIZE
    assert tn == 1 and past % MOBA_BLOCK == 0 and MOBA_BLOCK % PAGE_SIZE == 0
    ppb = MOBA_BLOCK // PAGE_SIZE
    nbp = past // MOBA_BLOCK
    ck = cache_k.reshape(cache_k.shape[0], PAGE_SIZE, C_A)
    cv = cache_v.reshape(cache_v.shape[0], PAGE_SIZE, C_A)
    km_past = _paged_block_means(ck, page_table)[:, :, 0]
    km_new = k.astype(f32).reshape(bd, 1, C_A) * (1.0 / MOBA_BLOCK)
    kmean = jnp.concatenate([km_past, km_new], axis=1).reshape(bd, nbp + 1, H_A, DH_A).transpose(0, 2, 1, 3)
    qpos = past + jnp.arange(tn)
    idx, ok = _select_blocks(q.transpose(0, 2, 1, 3), kmean, qpos // MOBA_BLOCK)
    idx, ok = idx[:, :, 0], ok[:, :, 0]
    n_sel = idx.shape[-1]
    lpage = jnp.clip((idx[..., None] * ppb + jnp.arange(ppb)).reshape(bd, H_A, n_sel * ppb), 0, n_pages - 1)
    ppage = jnp.take_along_axis(page_table[:, None, :], lpage, axis=2).astype(i32)
    okp = ok.astype(i32)
    pos = jnp.arange(past)
    bias_pos = rel_bias.astype(f32)[_t5_bucket(qpos[0] - pos)].T.reshape(H_A, n_pages, 1, PAGE_SIZE)
    own_bias = jnp.broadcast_to(rel_bias.astype(f32)[_t5_bucket(jnp.zeros((), i32))][:, None, None], (H_A, 1, LANES))
    rep8 = lambda z: jnp.broadcast_to(z.reshape(bd, 1, C_A).astype(f32), (bd, SUBLANES, C_A))
    hspec = pl.BlockSpec((1, SUBLANES, DH_A), lambda b, h, s, pp, lp, okr: (b, 0, h))
    o = pl.pallas_call(
        functools.partial(_moba_decode_body, ppb=ppb),
        grid_spec=pltpu.PrefetchScalarGridSpec(
            num_scalar_prefetch=3, grid=(bd, H_A, n_sel * ppb),
            in_specs=[hspec, hspec, hspec,
                      pl.BlockSpec((1, 1, LANES), lambda b, h, s, pp, lp, okr: (h, 0, 0)),
                      pl.BlockSpec((1, PAGE_SIZE, DH_A), lambda b, h, s, pp, lp, okr: (pp[b, h, s], 0, h)),
                      pl.BlockSpec((1, PAGE_SIZE, DH_A), lambda b, h, s, pp, lp, okr: (pp[b, h, s], 0, h)),
                      pl.BlockSpec((1, 1, 1, PAGE_SIZE), lambda b, h, s, pp, lp, okr: (h, lp[b, h, s], 0, 0))],
            out_specs=hspec,
            scratch_shapes=[pltpu.VMEM((SUBLANES, 1), f32), pltpu.VMEM((SUBLANES, 1), f32),
                            pltpu.VMEM((SUBLANES, DH_A), f32)]),
        out_shape=jax.ShapeDtypeStruct((bd, SUBLANES, C_A), f32),
        compiler_params=_cparams("parallel", "parallel", "arbitrary"),
        name="moba_decode",
    )(ppage, lpage.astype(i32), okp, rep8(q), rep8(k), rep8(v), own_bias, ck, cv, bias_pos)
    return o[:, 0].reshape(bd, tn, H_A, DH_A)


def _sigmoid(x):
    return 1.0 / (1.0 + jnp.exp(-x))


def _dot3(a, b):
    ah = a.astype(bf16)
    al = (a - ah.astype(f32)).astype(bf16)
    bh = b.astype(bf16)
    bl = (b - bh.astype(f32)).astype(bf16)
    return (jnp.dot(ah, bh, preferred_element_type=f32) + jnp.dot(ah, bl, preferred_element_type=f32)
            + jnp.dot(al, bh, preferred_element_type=f32))


def _col_to_row(col, eye):
    return jnp.sum(jnp.where(eye, col, 0.0), axis=0, keepdims=True)


def _gdn_prep_body(q_ref, k_ref, v_ref, bg_ref, cq_ref, ck_ref, cv_ref, wq_ref, wk_ref, wv_ref, gp_ref,
                   uin_ref, wcum_ref, qdec_ref, ktail_ref, attn_ref, gtot_ref, pq_ref, pk_ref, pv_ref, *, t_len, nsub):
    hk = pl.program_id(1)
    c = pl.program_id(2)
    cs = GDN_CHUNK
    rep = HV_C // HK_C
    rows = nsub * cs

    @pl.when(c == 0)
    def _():
        pq_ref[...] = cq_ref[0]
        pk_ref[...] = ck_ref[0]
        pv_ref[...] = cv_ref[0]

    def conv_silu(x_ref, prev_ref, w_ref):
        cur = x_ref[...]
        ext = jnp.concatenate([prev_ref[...], cur], axis=0)
        acc = cur * w_ref[CONV_W - 1:CONV_W, :]
        for i in range(CONV_W - 1):
            lo = SUBLANES - (CONV_W - 1) + i
            acc = acc + ext[lo:lo + rows] * w_ref[i:i + 1, :]
        prev_ref[...] = cur[rows - SUBLANES:]
        return acc * _sigmoid(acc)

    def l2n(x):
        return x * lax.rsqrt(jnp.sum(x * x, axis=-1, keepdims=True) + 1e-6)

    q_all = l2n(conv_silu(q_ref, pq_ref, wq_ref)) * (DK_C ** -0.5)
    k_all = l2n(conv_silu(k_ref, pk_ref, wk_ref))
    v_all = conv_silu(v_ref, pv_ref, wv_ref)

    bg = bg_ref[...]
    beta_all = _sigmoid(bg)
    xg = bg + gp_ref[1:2, :]
    g_all = -jnp.exp(gp_ref[0:1, :]) * (jnp.maximum(xg, 0.0) + jnp.log1p(jnp.exp(-jnp.abs(xg))))
    lane = lax.broadcasted_iota(i32, bg.shape, 1)
    if t_len % cs:
        live = (c * rows + lax.broadcasted_iota(i32, (rows, 1), 0)) < t_len
    hsel = []
    for e in range(rep):
        hv = hk * rep + e
        beta = jnp.sum(jnp.where(lane == hv, beta_all, 0.0), axis=1, keepdims=True)
        g = jnp.sum(jnp.where(lane == hv + HV_C, g_all, 0.0), axis=1, keepdims=True)
        if t_len % cs:
            beta = jnp.where(live, beta, 0.0)
            g = jnp.where(live, g, 0.0)
        hsel.append((beta, g))
    ri = lax.broadcasted_iota(i32, (cs, cs), 0)
    ci = lax.broadcasted_iota(i32, (cs, cs), 1)
    eye = ri == ci
    tril = ci <= ri
    strict = ci < ri
    nt = (((1,), (1,)), ((), ()))
    results = []

    chains = []
    for sub in range(nsub):
        rs = slice(sub * cs, (sub + 1) * cs)
        q, k = q_all[rs], k_all[rs]
        kb16 = k.astype(bf16)
        kk = lax.dot_general(kb16, kb16, nt, preferred_element_type=f32)
        qk = lax.dot_general(q.astype(bf16), kb16, nt, preferred_element_type=f32)
        for e in range(rep):
            beta, g = hsel[e][0][rs], hsel[e][1][rs]
            gc = jnp.sum(jnp.where(tril, _col_to_row(g, eye), 0.0), axis=1, keepdims=True)
            gc_row = _col_to_row(gc, eye)
            decay = jnp.where(tril, jnp.exp(jnp.where(tril, gc - gc_row, 0.0)), 0.0)
            pw = jnp.where(strict, -(kk * beta * decay), 0.0)
            chains.append(dict(sub=sub, e=e, q=q, k=k, qk=qk, beta=beta, gc=gc, decay=decay, pw=pw,
                               inv=jnp.where(eye, 1.0, pw)))
    for _ in range(5):
        for ch in chains:
            ch['pw'] = _dot3(ch['pw'], ch['pw'])
        for ch in chains:
            ch['inv'] = ch['inv'] + _dot3(ch['inv'], ch['pw'])
    for ch in chains:
        sub, e, q, k, beta, gc = ch['sub'], ch['e'], ch['q'], ch['k'], ch['beta'], ch['gc']
        e_col = jnp.exp(gc)
        v = v_all[sub * cs:(sub + 1) * cs, e * DV_C:(e + 1) * DV_C]
        rhs = jnp.concatenate([v * beta, k * (beta * e_col)], axis=1)
        sol = _dot3(ch['inv'], rhs)
        gl = jnp.sum(jnp.where(ri[:, 0:1] == cs - 1, gc, 0.0), axis=0, keepdims=True)
        results.append((sub, e, sol[:, :DV_C], sol[:, DV_C:], q * e_col, k * jnp.exp(gl - gc),
                        jnp.where(tril, ch['qk'] * ch['decay'], 0.0), jnp.broadcast_to(jnp.exp(gl), (cs, DV_C))))

    for sub, e, uin, wcum, qdec, ktail, attn, gtot in results:
        rs = slice(sub * cs, (sub + 1) * cs)
        sl = slice(e * DV_C, (e + 1) * DV_C)
        uin_ref[rs, sl] = uin
        wcum_ref[rs, sl] = wcum
        qdec_ref[rs, sl] = qdec
        ktail_ref[rs, sl] = ktail
        attn_ref[rs, e * cs:(e + 1) * cs] = attn
        gtot_ref[rs, sl] = gtot


def _gdn_scan_body(uin_ref, wcum_ref, qdec_ref, ktail_ref, attn_ref, gtot_ref, z_ref, ow_ref, s0_ref,
                   o_ref, sf_ref, s_ref, *, ng):
    c = pl.program_id(2)
    cs = GDN_CHUNK

    @pl.when(c == 0)
    def _():
        s_ref[...] = s0_ref[0]

    tn = (((0,), (0,)), ((), ()))
    sls = [slice(e * DV_C, (e + 1) * DV_C) for e in range(ng)]
    ss = [s_ref[e] for e in range(ng)]
    s16 = [s.astype(bf16) for s in ss]
    us = [uin_ref[:, sls[e]] - jnp.dot(wcum_ref[:, sls[e]].astype(bf16), s16[e], preferred_element_type=f32)
          for e in range(ng)]
    u16 = [u.astype(bf16) for u in us]
    os_ = [jnp.dot(qdec_ref[:, sls[e]].astype(bf16), s16[e], preferred_element_type=f32)
           + jnp.dot(attn_ref[:, e * cs:(e + 1) * cs].astype(bf16), u16[e], preferred_element_type=f32)
           for e in range(ng)]
    states = [ss[e] * jnp.concatenate([gtot_ref[:, sls[e]]] * (DK_C // cs), axis=0)
              + lax.dot_general(ktail_ref[:, sls[e]].astype(bf16), u16[e], tn, preferred_element_type=f32)
              for e in range(ng)]
    outs = []
    for e in range(ng):
        o = os_[e] * lax.rsqrt(jnp.mean(os_[e] * os_[e], axis=-1, keepdims=True) + RMS_EPS) * ow_ref[...]
        z = z_ref[:, sls[e]]
        outs.append(o * (z * _sigmoid(z)))
    for e in range(ng):
        s_ref[e] = states[e]
        o_ref[:, sls[e]] = outs[e]

    @pl.when(c == pl.num_programs(2) - 1)
    def _():
        sf_ref[0] = s_ref[...]


def _gdn_mixer(p, t_len, conv_buf, s0, prm):
    b, tp, wd = p.shape
    cs = GDN_CHUNK
    assert tp % cs == 0 and DK_C == DV_C and DK_C % cs == 0
    n = tp // cs
    rep = HV_C // HK_C
    p2 = p.reshape(b * tp, wd)
    cw = prm['conv_w'].astype(f32)
    cb = jnp.pad(conv_buf.astype(f32), ((0, 0), (SUBLANES - (CONV_W - 1), 0), (0, 0)))
    gp = jnp.zeros((2, LANES), f32).at[0, HV_C:2 * HV_C].set(prm['a_log'].astype(f32))
    gp = gp.at[1, HV_C:2 * HV_C].set(prm['dt_bias'].astype(f32))
    nq = QK_C // DK_C
    vw = rep * DV_C
    nsub = math.gcd(n, 4)
    rows = nsub * cs
    ns = n // nsub
    row = lambda bi, hk, c: bi * ns + c
    hv_out = jax.ShapeDtypeStruct((b * tp, V_C), f32)
    hspec = pl.BlockSpec((rows, vw), lambda bi, hk, c: (row(bi, hk, c), hk))
    uin, wcum, qdec, ktail, attn, gtot = pl.pallas_call(
        functools.partial(_gdn_prep_body, t_len=t_len, nsub=nsub),
        grid=(b, HK_C, ns),
        in_specs=[pl.BlockSpec((rows, DK_C), lambda bi, hk, c: (row(bi, hk, c), hk)),
                  pl.BlockSpec((rows, DK_C), lambda bi, hk, c: (row(bi, hk, c), nq + hk)),
                  pl.BlockSpec((rows, vw), lambda bi, hk, c: (row(bi, hk, c), 2 * QK_C // vw + hk)),
                  pl.BlockSpec((rows, LANES), lambda bi, hk, c: (row(bi, hk, c), (CONV_CH + V_C) // LANES)),
                  pl.BlockSpec((1, SUBLANES, DK_C), lambda bi, hk, c: (bi, 0, hk)),
                  pl.BlockSpec((1, SUBLANES, DK_C), lambda bi, hk, c: (bi, 0, nq + hk)),
                  pl.BlockSpec((1, SUBLANES, vw), lambda bi, hk, c: (bi, 0, 2 * QK_C // vw + hk)),
                  pl.BlockSpec((CONV_W, DK_C), lambda bi, hk, c: (0, hk)),
                  pl.BlockSpec((CONV_W, DK_C), lambda bi, hk, c: (0, nq + hk)),
                  pl.BlockSpec((CONV_W, vw), lambda bi, hk, c: (0, 2 * QK_C // vw + hk)),
                  pl.BlockSpec((2, LANES), lambda bi, hk, c: (0, 0))],
        out_specs=[hspec, hspec, hspec, hspec,
                   pl.BlockSpec((rows, rep * cs), lambda bi, hk, c: (row(bi, hk, c), hk)), hspec],
        out_shape=[hv_out, hv_out, hv_out, hv_out, jax.ShapeDtypeStruct((b * tp, HK_C * rep * cs), f32), hv_out],
        scratch_shapes=[pltpu.VMEM((SUBLANES, DK_C), f32), pltpu.VMEM((SUBLANES, DK_C), f32),
                        pltpu.VMEM((SUBLANES, vw), f32)],
        compiler_params=_cparams("parallel", "parallel", "arbitrary"),
        name="gdn_prep",
    )(p2, p2, p2, p2, cb, cb, cb, cw, cw, cw, gp)

    ng = 8
    gw = ng * DV_C
    gspec = pl.BlockSpec((cs, gw), lambda bi, hg, c: (bi * n + c, hg))
    sspec = pl.BlockSpec((1, ng, DK_C, DV_C), lambda bi, hg, c: (bi, hg, 0, 0))
    o, sf = pl.pallas_call(
        functools.partial(_gdn_scan_body, ng=ng),
        grid=(b, HV_C // ng, n),
        in_specs=[gspec, gspec, gspec, gspec,
                  pl.BlockSpec((cs, ng * cs), lambda bi, hg, c: (bi * n + c, hg)), gspec,
                  pl.BlockSpec((cs, gw), lambda bi, hg, c: (bi * n + c, CONV_CH // gw + hg)),
                  pl.BlockSpec((1, DV_C), lambda bi, hg, c: (0, 0)), sspec],
        out_specs=[gspec, sspec],
        out_shape=[hv_out, jax.ShapeDtypeStruct((b, HV_C, DK_C, DV_C), f32)],
        scratch_shapes=[pltpu.VMEM((ng, DK_C, DV_C), f32)],
        compiler_params=_cparams("parallel", "parallel", "arbitrary"),
        name="gdn_scan",
    )(uin, wcum, qdec, ktail, attn, gtot, p2, prm['onorm_w'].astype(f32).reshape(1, DV_C), s0.astype(f32))
    return o, sf


def _pad_rows(x, rows):
    return jnp.pad(x, ((0, rows - x.shape[0]), (0, 0)))


def kernel(x_prompt, x_sample, cache_k_0, cache_v_0, state_wkv_0, state_shift_0, state_gdn_1, state_conv_1, cache_k_2, cache_v_2, state_wkv_2, state_shift_2, state_gdn_3, state_conv_3, page_table, rel_bias, norm_mix_0, w_in_0, mu_0, w0_0, w2_0, a0_0, a2_0, g2_0, kk_scale_0, ka_mix_0, rk_bonus_0, lnx_w_0, lnx_b_0, w_out_0, norm_ffn_0, peer_wq_0, peer_keys_0, peer_u_0, peer_v_0, norm_mix_1, w_in_1, conv_w_1, a_log_1, dt_bias_1, onorm_w_1, w_out_1, norm_ffn_1, peer_wq_1, peer_keys_1, peer_u_1, peer_v_1, norm_mix_2, w_in_2, mu_2, w0_2, w2_2, a0_2, a2_2, v0_2, v2_2, g2_2, kk_scale_2, ka_mix_2, rk_bonus_2, lnx_w_2, lnx_b_2, w_out_2, norm_ffn_2, peer_wq_2, peer_keys_2, peer_u_2, peer_v_2, norm_mix_3, w_in_3, conv_w_3, a_log_3, dt_bias_3, onorm_w_3, w_out_3, norm_ffn_3, peer_wq_3, peer_keys_3, peer_u_3, peer_v_3, norm_final):
    mix_prm = [
        dict(norm=norm_mix_0, w_in=w_in_0, mu=mu_0, w0=w0_0, w2=w2_0, a0=a0_0, a2=a2_0, g2=g2_0,
             kk_scale=kk_scale_0, ka_mix=ka_mix_0, rk_bonus=rk_bonus_0, lnx_w=lnx_w_0, lnx_b=lnx_b_0,
             w_out=w_out_0),
        dict(norm=norm_mix_1, w_in=w_in_1, conv_w=conv_w_1, a_log=a_log_1, dt_bias=dt_bias_1,
             onorm_w=onorm_w_1, w_out=w_out_1),
        dict(norm=norm_mix_2, w_in=w_in_2, mu=mu_2, w0=w0_2, w2=w2_2, a0=a0_2, a2=a2_2, v0=v0_2, v2=v2_2,
             g2=g2_2, kk_scale=kk_scale_2, ka_mix=ka_mix_2, rk_bonus=rk_bonus_2, lnx_w=lnx_w_2,
             lnx_b=lnx_b_2, w_out=w_out_2),
        dict(norm=norm_mix_3, w_in=w_in_3, conv_w=conv_w_3, a_log=a_log_3, dt_bias=dt_bias_3,
             onorm_w=onorm_w_3, w_out=w_out_3),
    ]
    ffn_prm = [
        (norm_ffn_0, peer_wq_0, peer_keys_0, peer_u_0, peer_v_0),
        (norm_ffn_1, peer_wq_1, peer_keys_1, peer_u_1, peer_v_1),
        (norm_ffn_2, peer_wq_2, peer_keys_2, peer_u_2, peer_v_2),
        (norm_ffn_3, peer_wq_3, peer_keys_3, peer_u_3, peer_v_3),
    ]
    layer_state = [
        (cache_k_0, cache_v_0, state_wkv_0, state_shift_0),
        (state_gdn_1, state_conv_1),
        (cache_k_2, cache_v_2, state_wkv_2, state_shift_2),
        (state_gdn_3, state_conv_3),
    ]
    bp, sp, d = x_prompt.shape
    bs, ts, _ = x_sample.shape
    ns = bs * ts
    ns_mm = -(-ns // 16) * 16
    ns_peer = -(-ns // LANES) * LANES
    xp = x_prompt.reshape(bp * sp, d)
    xs = x_sample.reshape(ns, d)
    depth = len(mix_prm)
    vf_p = vf_s = None
    new = []
    for i in range(depth):
        prm = dict(mix_prm[i])
        prm['w_in'] = prm['w_in'].astype(bf16)
        prm['w_out'] = prm['w_out'].astype(bf16)
        xs_pad = _pad_rows(xs, ns_mm)
        pp = _matmul(xp, prm['w_in'], gain=prm['norm'])
        ps = _matmul(xs_pad, prm['w_in'], gain=prm['norm'])[:ns]
        if i % 2 == 0:
            ck, cv, wkv_in, shift_in = layer_state[i]
            first = i == 0
            rw = prm['w_in'].shape[1] - 3 * C_A
            o_a = _moba_prompt(pp, rel_bias, bp, sp)
            p_rw = pp[:, 3 * C_A:].reshape(bp, sp, rw)
            p_prev = jnp.concatenate([jnp.zeros((bp, 1, rw), f32), p_rw[:, :-1]], axis=1)
            o_b, wkv_p, vf_p = _rwkv7(p_rw, p_prev, jnp.zeros((bp, H_B, DH_B, DH_B), f32), vf_p, prm, first)
            cat = jnp.concatenate([o_a, o_b.reshape(bp * sp, C_B)], axis=-1)
            xp_new = _matmul(cat, prm['w_out'], res=xp)
            kp = pp[:, C_A:2 * C_A].reshape(bp, sp // PAGE_SIZE, PAGE_SIZE, H_A, DH_A)
            vp = pp[:, 2 * C_A:3 * C_A].reshape(bp, sp // PAGE_SIZE, PAGE_SIZE, H_A, DH_A)
            hp_last = _rms(xp.reshape(bp, sp, d)[:, -1], prm['norm'])
            qs = ps[:, :C_A].reshape(bs, ts, H_A, DH_A)
            ks = ps[:, C_A:2 * C_A].reshape(bs, ts, H_A, DH_A)
            vs = ps[:, 2 * C_A:3 * C_A].reshape(bs, ts, H_A, DH_A)
            o_as = _moba_sample(qs, ks, vs, ck, cv, page_table, rel_bias).reshape(ns, C_A)
            ps_rw = ps[:, 3 * C_A:].reshape(bs, ts, rw)
            prev0 = _matmul(_pad_rows(shift_in, ns_mm), prm['w_in'], col0=3 * C_A)[:bs]
            ps_prev = jnp.concatenate([prev0[:, None], ps_rw[:, :-1]], axis=1)
            o_bs, wkv_s, vf_s = _rwkv7(ps_rw, ps_prev, wkv_in, vf_s, prm, first)
            cat_s = jnp.concatenate([o_as, o_bs.reshape(ns, C_B)], axis=-1)
            xs_new = _matmul(_pad_rows(cat_s, ns_mm), prm['w_out'], res=xs_pad)[:ns]
            hs_last = _rms(xs.reshape(bs, ts, d)[:, -1], prm['norm'])
            new.append((kp, vp, ks, vs, wkv_p, wkv_s, hp_last, hs_last))
        else:
            gdn_in, conv_in = layer_state[i]
            wd = prm['w_in'].shape[1]

            def gdn(p3, conv_buf, s0):
                bb, tt, _ = p3.shape
                tpad = -(-tt // GDN_CHUNK) * GDN_CHUNK
                o, sf = _gdn_mixer(jnp.pad(p3, ((0, 0), (0, tpad - tt), (0, 0))), tt, conv_buf, s0, prm)
                conv = jnp.concatenate([conv_buf.astype(f32), p3[..., :CONV_CH]], axis=1)[:, -(CONV_W - 1):]
                return o.reshape(bb, tpad, V_C)[:, :tt].reshape(bb * tt, V_C), conv, sf

            o_p, conv_p, gdn_p = gdn(pp.reshape(bp, sp, wd), jnp.zeros((bp, CONV_W - 1, CONV_CH), f32),
                                     jnp.zeros((bp, HV_C, DK_C, DV_C), f32))
            o_s, conv_s, gdn_s = gdn(ps.reshape(bs, ts, wd), conv_in, gdn_in)
            xp_new = _matmul(o_p, prm['w_out'], res=xp)
            xs_new = _matmul(_pad_rows(o_s, ns_mm), prm['w_out'], res=xs_pad)[:ns]
            new.append((gdn_p, gdn_s, conv_p, conv_s))
        xp, xs = xp_new, xs_new
        g_ffn, wq, sub_keys, u_tab, v_tab = ffn_prm[i]
        wq_bf, u_bf, v_bf = wq.astype(bf16), u_tab.astype(bf16), v_tab.astype(bf16)
        keys = sub_keys.astype(f32)
        xp = _peer(xp, g_ffn, wq_bf, keys, u_bf, v_bf)
        xs = _peer(_pad_rows(xs, ns_peer), g_ffn, wq_bf, keys, u_bf, v_bf)[:ns]
    y_prompt = _rms(xp, norm_final).reshape(bp, sp, d)
    y_sample = _rms(xs, norm_final).reshape(bs, ts, d)
    out = [y_prompt, y_sample]
    for layer in new:
        out.extend(layer)
    return tuple(out)
```

```python
import functools
import math

import jax
import jax.numpy as jnp
from jax import lax
from jax.experimental import pallas as pl
from jax.experimental.pallas import tpu as pltpu

f32 = jnp.float32
bf16 = jnp.bfloat16
i32 = jnp.int32

RMS_EPS = 1e-6
NEG_INF = -1e30

PAGE_SIZE = 128
H_A, DH_A = 8, 128
C_A = H_A * DH_A
MOBA_BLOCK, MOBA_TOPK = 256, 3
NUM_BUCKETS, MAX_DISTANCE = 32, 128
H_B, DH_B = 16, 64
C_B = H_B * DH_B
LORA_W, LORA_A, LORA_G, LORA_V = 64, 64, 160, 32
LNX_EPS = 64e-5
HK_C, HV_C, DK_C, DV_C = 16, 32, 128, 128
QK_C, V_C = HK_C * DK_C, HV_C * DV_C
CONV_W = 4
CONV_CH = 2 * QK_C + V_C
GDN_CHUNK = 64
PEER_HEADS, N_KEYS, PEER_DK, PEER_TOPK = 8, 128, 256, 16
N_EXPERTS = N_KEYS * N_KEYS

LANES = 128
SUBLANES = 8
VMEM_LIMIT = 56 * 1024 * 1024


def _cparams(*sem):
    return pltpu.CompilerParams(dimension_semantics=sem, vmem_limit_bytes=VMEM_LIMIT)


def _rms(x, g):
    return x * lax.rsqrt(jnp.mean(x * x, axis=-1, keepdims=True) + RMS_EPS) * g


def _mm_body(*refs, norm, has_res):
    if has_res:
        x_ref, g_ref, w_ref, res_ref, o_ref, h_ref = refs
    else:
        x_ref, g_ref, w_ref, o_ref, h_ref = refs

    @pl.when(pl.program_id(1) == 0)
    def _():
        x = x_ref[...]
        if norm:
            x = _rms(x, g_ref[...])
        h_ref[...] = x.astype(bf16)

    acc = jnp.dot(h_ref[...], w_ref[...], preferred_element_type=f32)
    if has_res:
        acc = res_ref[...] + acc
    o_ref[...] = acc


def _matmul(x, w, gain=None, res=None, col0=0, ncols=None, tn=512):
    m, k = x.shape
    ncols = w.shape[1] - col0 if ncols is None else ncols
    assert col0 % tn == 0 and m % 16 == 0 and w.dtype == bf16
    tm = min(m, 1024 if k <= 2048 else 512)
    assert m % tm == 0
    nj = pl.cdiv(ncols, tn)
    j0 = col0 // tn
    norm = gain is not None
    g = (gain if norm else jnp.ones((k,), f32)).reshape(1, k)
    in_specs = [pl.BlockSpec((tm, k), lambda i, j: (i, 0)),
                pl.BlockSpec((1, k), lambda i, j: (0, 0)),
                pl.BlockSpec((k, tn), lambda i, j: (0, j + j0))]
    args = [x, g, w]
    if res is not None:
        in_specs.append(pl.BlockSpec((tm, tn), lambda i, j: (i, j)))
        args.append(res)
    return pl.pallas_call(
        functools.partial(_mm_body, norm=norm, has_res=res is not None),
        grid=(m // tm, nj),
        in_specs=in_specs,
        out_specs=pl.BlockSpec((tm, tn), lambda i, j: (i, j)),
        out_shape=jax.ShapeDtypeStruct((m, ncols), f32),
        scratch_shapes=[pltpu.VMEM((tm, k), bf16)],
        compiler_params=_cparams("parallel", "arbitrary"),
        name="matmul",
    )(*args)


def _t5_bucket(dist):
    n = jnp.maximum(dist, 0)
    max_exact = NUM_BUCKETS // 2
    ratio = jnp.log(jnp.maximum(n, 1).astype(f32) / max_exact) / math.log(MAX_DISTANCE / max_exact)
    large = max_exact + (ratio * (NUM_BUCKETS - max_exact)).astype(i32)
    return jnp.where(n < max_exact, n, jnp.minimum(large, NUM_BUCKETS - 1))


def _moba_prompt_body(q_ref, k_ref, v_ref, bias_ref, o_ref, kmh_ref, kml_ref, kb_ref, vb_ref, *, nb, hp):
    qb = pl.program_id(2)
    blk = MOBA_BLOCK
    scale = DH_A ** -0.5
    nt = (((1,), (1,)), ((), ()))
    hs = [slice(e * DH_A, (e + 1) * DH_A) for e in range(hp)]
    nbr = -(-nb // SUBLANES) * SUBLANES

    @pl.when(qb == 0)
    def _():
        k = k_ref[...]
        kmh_ref[...] = jnp.zeros_like(kmh_ref)
        kml_ref[...] = jnp.zeros_like(kml_ref)
        for e in range(hp):
            km = jnp.mean(k[:, hs[e]].reshape(nb, blk, DH_A), axis=1)
            hi = km.astype(bf16)
            kmh_ref[e, 0:nb, :] = hi
            kml_ref[e, 0:nb, :] = (km - hi.astype(f32)).astype(bf16)
        kb_ref[...] = k.astype(bf16)
        vb_ref[...] = v_ref[...].astype(bf16)

    q = q_ref[...]
    q16 = [q[:, hs[e]].astype(bf16) for e in range(hp)]
    rowj = lax.broadcasted_iota(i32, (nbr, blk), 0)
    past = rowj < qb
    lhs = []
    for e in range(hp):
        qlo = (q[:, hs[e]] - q16[e].astype(f32)).astype(bf16)
        gate = (lax.dot_general(kmh_ref[e], q16[e], nt, preferred_element_type=f32)
                + lax.dot_general(kmh_ref[e], qlo, nt, preferred_element_type=f32)
                + lax.dot_general(kml_ref[e], q16[e], nt, preferred_element_type=f32))[0:nbr]
        g = jnp.where(past, gate, NEG_INF)
        sel = jnp.zeros(g.shape, f32)
        for _ in range(MOBA_TOPK):
            m = jnp.max(g, axis=0, keepdims=True)
            idx = jnp.min(jnp.where(g == m, rowj, nbr), axis=0, keepdims=True)
            pick = rowj == idx
            sel = jnp.where(pick, jnp.where(past, 1.0, 0.0), sel)
            g = jnp.where(pick, -jnp.inf, g)
        sel = jnp.where(rowj == qb, 1.0, sel)
        selneg = jnp.where(sel > 0.0, 0.0, NEG_INF / scale)
        selneg = jnp.concatenate([selneg, jnp.zeros((LANES - nbr, blk), f32)], axis=0).T
        lhs.append(jnp.concatenate([q16[e], selneg.astype(bf16)], axis=1))

    rowi = qb * blk + lax.broadcasted_iota(i32, (blk, blk), 0)
    coli = lax.broadcasted_iota(i32, (blk, blk), 1)
    lane = lax.broadcasted_iota(i32, (blk, LANES), 1)

    def tile(j, carry):
        ks = pl.multiple_of(j * blk, blk)
        boff = pl.multiple_of((nb - 1 - qb + j) * blk, blk)
        causal = coli + j * blk <= rowi
        onehot = jnp.where(lane == j, 1.0, 0.0).astype(bf16)
        ss = [lax.dot_general(lhs[e], jnp.concatenate([kb_ref[pl.ds(ks, blk), hs[e]], onehot], axis=1), nt,
                              preferred_element_type=f32) * scale + bias_ref[e, :, pl.ds(boff, blk)]
              for e in range(hp)]
        ss = [jnp.where(causal, s, NEG_INF) for s in ss]
        ms = [jnp.maximum(carry[e][0], jnp.max(ss[e], axis=1, keepdims=True)) for e in range(hp)]
        ps = [jnp.exp(ss[e] - ms[e]) for e in range(hp)]
        als = [jnp.exp(carry[e][0] - ms[e]) for e in range(hp)]
        return tuple((ms[e], als[e] * carry[e][1] + jnp.sum(ps[e], axis=1, keepdims=True),
                      als[e] * carry[e][2] + jnp.dot(ps[e].astype(bf16), vb_ref[pl.ds(ks, blk), hs[e]],
                                                     preferred_element_type=f32)) for e in range(hp))

    init = tuple((jnp.full((blk, 1), -jnp.inf, f32), jnp.zeros((blk, 1), f32), jnp.zeros((blk, DH_A), f32))
                 for _ in range(hp))
    res = lax.fori_loop(0, qb + 1, tile, init)
    o_ref[...] = jnp.concatenate([acc / l for _, l, acc in res], axis=1)


def _bias_strip_body(f_ref, o_ref, *, width):
    row = jnp.broadcast_to(f_ref[0], (MOBA_BLOCK, f_ref.shape[2]))
    o_ref[0] = pltpu.roll(row, 0, 1, stride=1, stride_axis=0)[:, :width]


def _bias_strip(rel_bias, nb):
    blk = MOBA_BLOCK
    width = (2 * nb - 1) * blk
    padded = width + blk
    x = jnp.arange(padded)
    dist = jnp.where(x < width, (nb - 1) * blk - x, (nb - 1) * blk + padded - x)
    prof = rel_bias.astype(f32)[_t5_bucket(dist)].T.reshape(H_A, 1, padded)
    return pl.pallas_call(
        functools.partial(_bias_strip_body, width=width),
        grid=(H_A,),
        in_specs=[pl.BlockSpec((1, 1, padded), lambda h: (h, 0, 0))],
        out_specs=pl.BlockSpec((1, blk, width), lambda h: (h, 0, 0)),
        out_shape=jax.ShapeDtypeStruct((H_A, blk, width), f32),
        compiler_params=_cparams("parallel"),
        name="moba_bias_strip",
    )(prof)


def _moba_prompt(p, rel_bias, b, s_len):
    blk = MOBA_BLOCK
    assert s_len % blk == 0
    nb = s_len // blk
    strip = _bias_strip(rel_bias, nb)
    hp = 2
    hw = hp * DH_A
    ng = C_A // hw
    return pl.pallas_call(
        functools.partial(_moba_prompt_body, nb=nb, hp=hp),
        grid=(b, ng, nb),
        in_specs=[pl.BlockSpec((blk, hw), lambda bi, h, qb: (bi * nb + qb, h)),
                  pl.BlockSpec((s_len, hw), lambda bi, h, qb: (bi, ng + h)),
                  pl.BlockSpec((s_len, hw), lambda bi, h, qb: (bi, 2 * ng + h)),
                  pl.BlockSpec((hp, blk, (2 * nb - 1) * blk), lambda bi, h, qb: (h, 0, 0))],
        out_specs=pl.BlockSpec((blk, hw), lambda bi, h, qb: (bi * nb + qb, h)),
        out_shape=jax.ShapeDtypeStruct((b * s_len, C_A), f32),
        scratch_shapes=[pltpu.VMEM((hp, LANES, DH_A), bf16), pltpu.VMEM((hp, LANES, DH_A), bf16),
                        pltpu.VMEM((s_len, hw), bf16), pltpu.VMEM((s_len, hw), bf16)],
        compiler_params=_cparams("parallel", "parallel", "arbitrary"),
        name="moba_prompt",
    )(p, p, p, strip)


def _wkv_body(r_ref, w_ref, k_ref, kk_ref, b_ref, v_ref, s0_ref, y_ref, sf_ref, s_ref, *, tc):
    @pl.when(pl.program_id(1) == 0)
    def _():
        s_ref[...] = s0_ref[0]

    half = LANES // 2

    def step(t, c):
        r, w, k, kk, b = r_ref[0, t], w_ref[0, t], k_ref[0, t], kk_ref[0, t], b_ref[0, t]
        for i in range(DH_B):
            s = s_ref[i]
            vi = v_ref[0, t, pl.ds(i, 1), :]
            t1 = jnp.sum(s * kk, axis=0, keepdims=True)
            sa = -(t1 + pltpu.roll(t1, half, 1))
            s = s * w + sa * b + vi * k
            s_ref[i] = s
            t2 = jnp.sum(s * r, axis=0, keepdims=True)
            y_ref[0, t, pl.ds(i, 1), :] = t2 + pltpu.roll(t2, half, 1)
        return c

    lax.fori_loop(0, tc, step, 0)

    @pl.when(pl.program_id(1) == pl.num_programs(1) - 1)
    def _():
        sf_ref[0] = s_ref[...]


def _wkv_scan(r, w, k, kk, b, v, s0):
    bsz, t, h, n = r.shape
    assert n == DH_B and (bsz * h) % (LANES // 2) == 0
    grp = bsz * h // (LANES // 2)
    bg = bsz // grp
    hn = n // 2

    def vec(z):
        z = z.reshape(grp, bg, t, h, 2, hn)
        return z.transpose(0, 2, 5, 4, 1, 3).reshape(grp, t, hn, LANES)

    def row(z):
        z = z.reshape(grp, bg, t, h, n).transpose(0, 2, 4, 1, 3).reshape(grp, t, n, LANES // 2)
        return jnp.concatenate([z, z], axis=-1)

    s0k = s0.astype(f32).reshape(grp, bg, h, n, 2, hn).transpose(0, 3, 5, 4, 1, 2).reshape(grp, n, hn, LANES)
    tc = math.gcd(t, 64)
    vspec = pl.BlockSpec((1, tc, hn, LANES), lambda g, c: (g, c, 0, 0))
    rspec = pl.BlockSpec((1, tc, n, LANES), lambda g, c: (g, c, 0, 0))
    sspec = pl.BlockSpec((1, n, hn, LANES), lambda g, c: (g, 0, 0, 0))
    y, sf = pl.pallas_call(
        functools.partial(_wkv_body, tc=tc),
        grid=(grp, t // tc),
        in_specs=[vspec] * 5 + [rspec, sspec],
        out_specs=[rspec, sspec],
        out_shape=[jax.ShapeDtypeStruct((grp, t, n, LANES), f32), jax.ShapeDtypeStruct((grp, n, hn, LANES), f32)],
        scratch_shapes=[pltpu.VMEM((n, hn, LANES), f32)],
        compiler_params=_cparams("parallel", "arbitrary"),
        name="wkv_scan",
    )(vec(r), vec(w), vec(k), vec(kk), vec(b), row(v), s0k)
    y = y[..., :LANES // 2].reshape(grp, t, n, bg, h).transpose(0, 3, 1, 4, 2).reshape(bsz, t, h, n)
    sf = sf.reshape(grp, n, hn, 2, bg, h).transpose(0, 4, 5, 1, 3, 2).reshape(bsz, h, n, n)
    return y, sf


def _top_rows(vs, k, payloads=None):
    rows = lax.broadcasted_iota(i32, vs[0].shape, 0)
    n = vs[0].shape[0]
    vs = list(vs)
    vals = [[] for _ in vs]
    outs = [[] for _ in vs]
    for _ in range(k):
        ms = [jnp.max(v, axis=0, keepdims=True) for v in vs]
        ids = [jnp.min(jnp.where(v == m, rows, n), axis=0, keepdims=True) for v, m in zip(vs, ms)]
        picks = [rows == i for i in ids]
        vs = [jnp.where(p, -jnp.inf, v) for p, v in zip(picks, vs)]
        for j in range(len(vs)):
            vals[j].append(ms[j])
            outs[j].append(ids[j].astype(f32) if payloads is None else
                           jnp.sum(jnp.where(picks[j], payloads[j], 0.0), axis=0, keepdims=True))
    return [(jnp.concatenate(v, axis=0), jnp.concatenate(o, axis=0)) for v, o in zip(vals, outs)]


_PAIR_CNT = [PEER_TOPK // (m + 1) for m in range(PEER_TOPK)]
_PAIR_PAD = -sum(_PAIR_CNT) % SUBLANES


def _route_body(x_ref, g_ref, wq_ref, keys_ref, xn_ref, i1_ref, i2_ref, gate_ref, q_scr, e_scr, w_scr, *, tt):
    xn = _rms(x_ref[...], g_ref[...]).astype(bf16)
    xn_ref[...] = xn
    q_scr[...] = jnp.dot(xn, wq_ref[...], preferred_element_type=f32)
    hw = PEER_DK // 2
    nt = (((1,), (1,)), ((), ()))

    def head(h, c):
        row = pl.multiple_of(h * PEER_TOPK, PEER_TOPK)
        nsub = tt // LANES
        sts = []
        for sub in range(nsub):
            for p in range(2):
                col = pl.multiple_of(h * PEER_DK + p * hw, hw)
                qp = q_scr[sub * LANES:(sub + 1) * LANES, pl.ds(col, hw)]
                sts.append(lax.dot_general(keys_ref[p], qp, nt, preferred_element_type=f32))
        tops = _top_rows(sts, PEER_TOPK)
        cands, ceids = [], []
        for sub in range(nsub):
            (sv1, si1), (sv2, si2) = tops[2 * sub], tops[2 * sub + 1]
            si1 = si1 * float(N_KEYS)
            cands.append(jnp.concatenate([sv1[m:m + 1] + sv2[:_PAIR_CNT[m]] for m in range(PEER_TOPK)]
                                         + [jnp.full((_PAIR_PAD, LANES), -jnp.inf, f32)], axis=0))
            ceids.append(jnp.concatenate([si1[m:m + 1] + si2[:_PAIR_CNT[m]] for m in range(PEER_TOPK)]
                                         + [jnp.zeros((_PAIR_PAD, LANES), f32)], axis=0))
        joint = _top_rows(cands, PEER_TOPK, payloads=ceids)
        for sub in range(nsub):
            cv, eid = joint[sub]
            ex = jnp.exp(cv - cv[0:1])
            e_scr[pl.ds(row, PEER_TOPK), sub * LANES:(sub + 1) * LANES] = eid
            w_scr[pl.ds(row, PEER_TOPK), sub * LANES:(sub + 1) * LANES] = ex / jnp.sum(ex, axis=0, keepdims=True)
        return c

    lax.fori_loop(0, PEER_HEADS, head, 0)
    eid = e_scr[...].T
    i1 = jnp.floor(eid * (1.0 / N_KEYS))
    i1_ref[...] = i1
    i2_ref[...] = eid - i1 * float(N_KEYS)
    gate_ref[...] = w_scr[...].T


def _peer_route(x, gain, wq_bf, keys):
    t, d = x.shape
    tt = min(t, 256)
    assert t % tt == 0 and tt % LANES == 0
    nsel = PEER_HEADS * PEER_TOPK
    sel = jax.ShapeDtypeStruct((t, nsel), f32)
    sspec = pl.BlockSpec((tt, nsel), lambda i: (i, 0))
    return pl.pallas_call(
        functools.partial(_route_body, tt=tt),
        grid=(t // tt,),
        in_specs=[pl.BlockSpec((tt, d), lambda i: (i, 0)),
                  pl.BlockSpec((1, d), lambda i: (0, 0)),
                  pl.BlockSpec(wq_bf.shape, lambda i: (0, 0)),
                  pl.BlockSpec(keys.shape, lambda i: (0, 0, 0))],
        out_specs=[pl.BlockSpec((tt, d), lambda i: (i, 0)), sspec, sspec, sspec],
        out_shape=[jax.ShapeDtypeStruct((t, d), bf16), sel, sel, sel],
        scratch_shapes=[pltpu.VMEM((tt, wq_bf.shape[1]), f32), pltpu.VMEM((nsel, tt), f32),
                        pltpu.VMEM((nsel, tt), f32)],
        compiler_params=_cparams("parallel"),
        name="peer_route",
    )(x, gain.reshape(1, d), wq_bf, keys)


def _gate_build_body(i1_ref, i2_ref, g_ref, o_ref, *, tb):
    keyrow = lax.broadcasted_iota(i32, (N_KEYS, i1_ref.shape[1]), 0).astype(f32)
    nt = (((1,), (1,)), ((), ()))

    def toks(i, c):
        t0 = pl.multiple_of(i * SUBLANES, SUBLANES)
        i1 = i1_ref[pl.ds(t0, SUBLANES), :]
        i2 = i2_ref[pl.ds(t0, SUBLANES), :]
        g = g_ref[pl.ds(t0, SUBLANES), :]
        ghi = g.astype(bf16).astype(f32)
        glo = g - ghi
        ops = []
        for r in range(SUBLANES):
            a = jnp.where(keyrow == i1[r:r + 1], 1.0, 0.0).astype(bf16)
            m2 = keyrow == i2[r:r + 1]
            bhi = jnp.where(m2, ghi[r:r + 1], 0.0).astype(bf16)
            blo = jnp.where(m2, glo[r:r + 1], 0.0).astype(bf16)
            ops.append((jnp.concatenate([a, a], axis=1), jnp.concatenate([bhi, blo], axis=1)))
        gs = [lax.dot_general(a2, b2, nt, preferred_element_type=f32) for a2, b2 in ops]
        for r in range(SUBLANES):
            o_ref[i, :, r, :] = gs[r]
        return c

    lax.fori_loop(0, tb // SUBLANES, toks, 0)


def _gate_build(i1, i2, gate):
    t, nsel = i1.shape
    tb = min(t, 128)
    assert t % tb == 0 and tb % SUBLANES == 0
    spec = pl.BlockSpec((tb, nsel), lambda i: (i, 0))
    return pl.pallas_call(
        functools.partial(_gate_build_body, tb=tb),
        grid=(t // tb,),
        in_specs=[spec, spec, spec],
        out_specs=pl.BlockSpec((tb // SUBLANES, N_KEYS, SUBLANES, N_KEYS), lambda i: (i, 0, 0, 0)),
        out_shape=jax.ShapeDtypeStruct((t // SUBLANES, N_KEYS, SUBLANES, N_KEYS), f32),
        compiler_params=_cparams("parallel"),
        name="peer_gate_build",
    )(i1, i2, gate)


def _peer_mix_body(x_ref, g_ref, u_ref, v_ref, o_ref, *, na, tt):
    j = pl.program_id(1)

    @pl.when(j == 0)
    def _():
        o_ref[...] = jnp.zeros_like(o_ref)

    h = lax.dot_general(x_ref[...], u_ref[...], (((1,), (1,)), ((), ())), preferred_element_type=f32)
    parts = []
    for a in range(na):
        ha = h[:, a * N_KEYS:(a + 1) * N_KEYS]
        act = 0.5 * ha * (1.0 + lax.erf(ha * (2.0 ** -0.5)))
        parts.append((g_ref[:, a].reshape(tt, N_KEYS) * act).astype(bf16))
    p = jnp.concatenate(parts, axis=1)
    o_ref[...] += jnp.dot(p, v_ref[...], preferred_element_type=f32)


def _peer_mix(xn, g4, u_bf, v_bf):
    t, d = xn.shape
    tt = min(t, 1024)
    na = 4
    et = na * N_KEYS
    assert t % tt == 0 and tt % SUBLANES == 0
    return pl.pallas_call(
        functools.partial(_peer_mix_body, na=na, tt=tt),
        grid=(t // tt, N_EXPERTS // et),
        in_specs=[pl.BlockSpec((tt, d), lambda i, j: (i, 0)),
                  pl.BlockSpec((tt // SUBLANES, na, SUBLANES, N_KEYS), lambda i, j: (i, j, 0, 0)),
                  pl.BlockSpec((et, d), lambda i, j: (j, 0)),
                  pl.BlockSpec((et, d), lambda i, j: (j, 0))],
        out_specs=pl.BlockSpec((tt, d), lambda i, j: (i, 0)),
        out_shape=jax.ShapeDtypeStruct((t, d), f32),
        compiler_params=_cparams("parallel", "arbitrary"),
        name="peer_mix",
    )(xn, g4, u_bf, v_bf)


def _peer(x, gain, wq_bf, keys, u_bf, v_bf):
    xn, i1, i2, gate = _peer_route(x, gain, wq_bf, keys)
    return x + _peer_mix(xn, _gate_build(i1, i2, gate), u_bf, v_bf)


def _l2norm(x, eps):
    return x * lax.rsqrt(jnp.sum(x * x, axis=-1, keepdims=True) + eps)


def _rwkv7(p_cur, p_prev, s0, v_first, prm, first):
    b, t, _ = p_cur.shape
    m = (p_cur + (p_prev - p_cur) * prm['mu']).astype(f32)
    r, k, v = m[..., :C_B], m[..., C_B:2 * C_B], m[..., 2 * C_B:3 * C_B]
    o = 3 * C_B
    wl = m[..., o:o + LORA_W]
    o += LORA_W
    al = m[..., o:o + LORA_A]
    o += LORA_A
    gl = m[..., o:o + LORA_G]
    o += LORA_G
    w_log = -jax.nn.softplus(-(prm['w0'] + jnp.tanh(wl) @ prm['w2'])) - 0.5
    decay = jnp.exp(-jnp.exp(w_log.astype(f32)))
    a = jax.nn.sigmoid((prm['a0'] + al @ prm['a2']).astype(f32))
    g = (jax.nn.sigmoid(gl) @ prm['g2']).astype(f32)
    if first:
        v_first = v
    else:
        vl = m[..., o:o + LORA_V]
        v = v + (v_first - v) * jax.nn.sigmoid((prm['v0'] + vl @ prm['v2']).astype(f32))

    def hd(z):
        return z.reshape(b, t, H_B, DH_B)

    kk = _l2norm(hd(k * prm['kk_scale']), 1e-24)
    k = k * (1.0 + (a - 1.0) * prm['ka_mix'])
    r_h, k_h, v_h = hd(r), hd(k), hd(v)
    b_h = kk * hd(a)
    y, s_fin = _wkv_scan(r_h, hd(decay), k_h, kk, b_h, v_h, s0)
    mu = jnp.mean(y, axis=-1, keepdims=True)
    var = jnp.mean(jnp.square(y - mu), axis=-1, keepdims=True)
    y = ((y - mu) * lax.rsqrt(var + LNX_EPS)).reshape(b, t, C_B) * prm['lnx_w'] + prm['lnx_b']
    bonus = jnp.sum(r_h * k_h * prm['rk_bonus'], axis=-1, keepdims=True) * v_h
    y = (y + bonus.reshape(b, t, C_B)) * g
    return y, s_fin.astype(s0.dtype), v_first


def _select_blocks(q, kmean, qblk):
    n_sel = min(MOBA_TOPK, kmean.shape[2])
    gate = jnp.einsum('bhqd,bhjd->bhqj', q.astype(f32), kmean, precision=lax.Precision.HIGHEST)
    past = jnp.arange(kmean.shape[2])[None, :] < qblk[:, None]
    _, idx = lax.top_k(jnp.where(past, gate, NEG_INF), n_sel)
    return idx, idx < qblk[:, None]


def _kmean_body(pt_ref, *refs):
    o_ref = refs[-1]
    acc = jnp.sum(refs[0][0], axis=0)
    for r in refs[1:-1]:
        acc = acc + jnp.sum(r[0], axis=0)
    o_ref[0, 0] = acc * (1.0 / MOBA_BLOCK)


def _paged_block_means(cache, page_table):
    bd, n_pages = page_table.shape
    ppb = MOBA_BLOCK // PAGE_SIZE
    nbp = n_pages // ppb
    hd = cache.shape[2:]
    specs = [pl.BlockSpec((1, PAGE_SIZE) + hd, functools.partial(lambda b, j, pt, r: (pt[b, j * ppb + r], 0, 0, 0), r=r))
             for r in range(ppb)]
    return pl.pallas_call(
        _kmean_body,
        grid_spec=pltpu.PrefetchScalarGridSpec(
            num_scalar_prefetch=1, grid=(bd, nbp), in_specs=specs,
            out_specs=pl.BlockSpec((1, 1) + hd, lambda b, j, pt: (b, j, 0, 0))),
        out_shape=jax.ShapeDtypeStruct((bd, nbp) + hd, f32),
        compiler_params=_cparams("parallel", "arbitrary"),
        name="moba_block_means",
    )(page_table, *([cache] * ppb))


def _moba_decode_body(pp_ref, lp_ref, ok_ref, q_ref, kn_ref, vn_ref, ob_ref, k_ref, v_ref, b_ref, o_ref,
                      m_ref, l_ref, acc_ref, *, ppb):
    b, h, s = pl.program_id(0), pl.program_id(1), pl.program_id(2)
    scale = DH_A ** -0.5
    q = q_ref[0]

    @pl.when(s == 0)
    def _():
        m_ref[...] = jnp.sum(q * kn_ref[0], axis=1, keepdims=True) * scale + ob_ref[0][:, 0:1]
        l_ref[...] = jnp.ones_like(l_ref)
        acc_ref[...] = vn_ref[0]

    def update(kp, vp):
        sc = lax.dot_general(q.astype(bf16), kp.astype(bf16), (((1,), (1,)), ((), ())),
                             preferred_element_type=f32) * scale + b_ref[0, 0]
        sc = jnp.where(ok_ref[b, h, s // ppb] != 0, sc, NEG_INF)
        m_old = m_ref[...]
        m_new = jnp.maximum(m_old, jnp.max(sc, axis=1, keepdims=True))
        alpha = jnp.exp(m_old - m_new)
        p = jnp.exp(sc - m_new)
        m_ref[...] = m_new
        l_ref[...] = alpha * l_ref[...] + jnp.sum(p, axis=1, keepdims=True)
        acc_ref[...] = alpha * acc_ref[...] + jnp.dot(p.astype(bf16), vp.astype(bf16), preferred_element_type=f32)

    for hh in range(H_A):
        @pl.when(h == hh)
        def _(hh=hh):
            update(k_ref[0, :, hh, :], v_ref[0, :, hh, :])

    @pl.when(s == pl.num_programs(2) - 1)
    def _():
        o_ref[0] = acc_ref[...] / l_ref[...]


def _moba_sample(q, k, v, cache_k, cache_v, page_table, rel_bias):
    bd, tn = q.shape[:2]
    n_pages = page_table.shape[1]
    past = n_pages * PAGE_SIZE
    assert tn == 1 and past % MOBA_BLOCK == 0 and MOBA_BLOCK % PAGE_SIZE == 0
    ppb = MOBA_BLOCK // PAGE_SIZE
    km_past = _paged_block_means(cache_k, page_table)
    km_new = k.astype(f32).reshape(bd, 1, H_A, DH_A) * (1.0 / MOBA_BLOCK)
    kmean = jnp.concatenate([km_past, km_new], axis=1).transpose(0, 2, 1, 3)
    qpos = past + jnp.arange(tn)
    idx, ok = _select_blocks(q.transpose(0, 2, 1, 3), kmean, qpos // MOBA_BLOCK)
    idx, ok = idx[:, :, 0], ok[:, :, 0]
    n_sel = idx.shape[-1]
    lpage = jnp.clip((idx[..., None] * ppb + jnp.arange(ppb)).reshape(bd, H_A, n_sel * ppb), 0, n_pages - 1)
    ppage = jnp.take_along_axis(page_table[:, None, :], lpage, axis=2).astype(i32)
    okp = ok.astype(i32)
    pos = jnp.arange(past)
    bias_pos = rel_bias.astype(f32)[_t5_bucket(qpos[0] - pos)].T.reshape(H_A, n_pages, 1, PAGE_SIZE)
    own_bias = jnp.broadcast_to(rel_bias.astype(f32)[_t5_bucket(jnp.zeros((), i32))][:, None, None], (H_A, 1, LANES))
    rep8 = lambda z: jnp.broadcast_to(z.reshape(bd, 1, C_A).astype(f32), (bd, SUBLANES, C_A))
    hspec = pl.BlockSpec((1, SUBLANES, DH_A), lambda b, h, s, pp, lp, okr: (b, 0, h))
    page = pl.BlockSpec((1, PAGE_SIZE, H_A, DH_A), lambda b, h, s, pp, lp, okr: (pp[b, h, s], 0, 0, 0))
    o = pl.pallas_call(
        functools.partial(_moba_decode_body, ppb=ppb),
        grid_spec=pltpu.PrefetchScalarGridSpec(
            num_scalar_prefetch=3, grid=(bd, H_A, n_sel * ppb),
            in_specs=[hspec, hspec, hspec,
                      pl.BlockSpec((1, 1, LANES), lambda b, h, s, pp, lp, okr: (h, 0, 0)),
                      page, page,
                      pl.BlockSpec((1, 1, 1, PAGE_SIZE), lambda b, h, s, pp, lp, okr: (h, lp[b, h, s], 0, 0))],
            out_specs=hspec,
            scratch_shapes=[pltpu.VMEM((SUBLANES, 1), f32), pltpu.VMEM((SUBLANES, 1), f32),
                            pltpu.VMEM((SUBLANES, DH_A), f32)]),
        out_shape=jax.ShapeDtypeStruct((bd, SUBLANES, C_A), f32),
        compiler_params=_cparams("parallel", "parallel", "arbitrary"),
        name="moba_decode",
    )(ppage, lpage.astype(i32), okp, rep8(q), rep8(k), rep8(v), own_bias, cache_k, cache_v, bias_pos)
    return o[:, 0].reshape(bd, tn, H_A, DH_A)


def _sigmoid(x):
    return 1.0 / (1.0 + jnp.exp(-x))


def _dot3(a, b):
    ah = a.astype(bf16)
    al = (a - ah.astype(f32)).astype(bf16)
    bh = b.astype(bf16)
    bl = (b - bh.astype(f32)).astype(bf16)
    return (jnp.dot(ah, bh, preferred_element_type=f32) + jnp.dot(ah, bl, preferred_element_type=f32)
            + jnp.dot(al, bh, preferred_element_type=f32))


def _col_to_row(col, eye):
    return jnp.sum(jnp.where(eye, col, 0.0), axis=0, keepdims=True)


def _gdn_prep_body(q_ref, k_ref, v_ref, bg_ref, cq_ref, ck_ref, cv_ref, wq_ref, wk_ref, wv_ref, gp_ref,
                   uin_ref, wcum_ref, qdec_ref, ktail_ref, attn_ref, gtot_ref, pq_ref, pk_ref, pv_ref, *, t_len, nsub):
    hk = pl.program_id(1)
    c = pl.program_id(2)
    cs = GDN_CHUNK
    rep = HV_C // HK_C
    rows = nsub * cs

    @pl.when(c == 0)
    def _():
        pq_ref[...] = cq_ref[0]
        pk_ref[...] = ck_ref[0]
        pv_ref[...] = cv_ref[0]

    def conv_silu(x_ref, prev_ref, w_ref):
        cur = x_ref[...]
        ext = jnp.concatenate([prev_ref[...], cur], axis=0)
        acc = cur * w_ref[CONV_W - 1:CONV_W, :]
        for i in range(CONV_W - 1):
            lo = SUBLANES - (CONV_W - 1) + i
            acc = acc + ext[lo:lo + rows] * w_ref[i:i + 1, :]
        prev_ref[...] = cur[rows - SUBLANES:]
        return acc * _sigmoid(acc)

    def l2n(x):
        return x * lax.rsqrt(jnp.sum(x * x, axis=-1, keepdims=True) + 1e-6)

    q_all = l2n(conv_silu(q_ref, pq_ref, wq_ref)) * (DK_C ** -0.5)
    k_all = l2n(conv_silu(k_ref, pk_ref, wk_ref))
    v_all = conv_silu(v_ref, pv_ref, wv_ref)

    bg = bg_ref[...]
    beta_all = _sigmoid(bg)
    xg = bg + gp_ref[1:2, :]
    g_all = -jnp.exp(gp_ref[0:1, :]) * (jnp.maximum(xg, 0.0) + jnp.log1p(jnp.exp(-jnp.abs(xg))))
    lane = lax.broadcasted_iota(i32, bg.shape, 1)
    if t_len % cs:
        live = (c * rows + lax.broadcasted_iota(i32, (rows, 1), 0)) < t_len
    hsel = []
    for e in range(rep):
        hv = hk * rep + e
        beta = jnp.sum(jnp.where(lane == hv, beta_all, 0.0), axis=1, keepdims=True)
        g = jnp.sum(jnp.where(lane == hv + HV_C, g_all, 0.0), axis=1, keepdims=True)
        if t_len % cs:
            beta = jnp.where(live, beta, 0.0)
            g = jnp.where(live, g, 0.0)
        hsel.append((beta, g))
    ri = lax.broadcasted_iota(i32, (cs, cs), 0)
    ci = lax.broadcasted_iota(i32, (cs, cs), 1)
    eye = ri == ci
    tril = ci <= ri
    strict = ci < ri
    nt = (((1,), (1,)), ((), ()))
    results = []

    chains = []
    for sub in range(nsub):
        rs = slice(sub * cs, (sub + 1) * cs)
        q, k = q_all[rs], k_all[rs]
        kb16 = k.astype(bf16)
        kk = lax.dot_general(kb16, kb16, nt, preferred_element_type=f32)
        qk = lax.dot_general(q.astype(bf16), kb16, nt, preferred_element_type=f32)
        for e in range(rep):
            beta, g = hsel[e][0][rs], hsel[e][1][rs]
            gc = jnp.sum(jnp.where(tril, _col_to_row(g, eye), 0.0), axis=1, keepdims=True)
            gc_row = _col_to_row(gc, eye)
            decay = jnp.where(tril, jnp.exp(jnp.where(tril, gc - gc_row, 0.0)), 0.0)
            pw = jnp.where(strict, -(kk * beta * decay), 0.0)
            chains.append(dict(sub=sub, e=e, q=q, k=k, qk=qk, beta=beta, gc=gc, decay=decay, pw=pw,
                               inv=jnp.where(eye, 1.0, pw)))
    for _ in range(5):
        for ch in chains:
            ch['pw'] = _dot3(ch['pw'], ch['pw'])
        for ch in chains:
            ch['inv'] = ch['inv'] + _dot3(ch['inv'], ch['pw'])
    for ch in chains:
        sub, e, q, k, beta, gc = ch['sub'], ch['e'], ch['q'], ch['k'], ch['beta'], ch['gc']
        e_col = jnp.exp(gc)
        v = v_all[sub * cs:(sub + 1) * cs, e * DV_C:(e + 1) * DV_C]
        rhs = jnp.concatenate([v * beta, k * (beta * e_col)], axis=1)
        sol = _dot3(ch['inv'], rhs)
        gl = jnp.sum(jnp.where(ri[:, 0:1] == cs - 1, gc, 0.0), axis=0, keepdims=True)
        results.append((sub, e, sol[:, :DV_C], sol[:, DV_C:], q * e_col, k * jnp.exp(gl - gc),
                        jnp.where(tril, ch['qk'] * ch['decay'], 0.0), jnp.broadcast_to(jnp.exp(gl), (cs, DV_C))))

    for sub, e, uin, wcum, qdec, ktail, attn, gtot in results:
        rs = slice(sub * cs, (sub + 1) * cs)
        sl = slice(e * DV_C, (e + 1) * DV_C)
        uin_ref[rs, sl] = uin
        wcum_ref[rs, sl] = wcum
        qdec_ref[rs, sl] = qdec
        ktail_ref[rs, sl] = ktail
        attn_ref[rs, e * cs:(e + 1) * cs] = attn
        gtot_ref[rs, sl] = gtot


def _gdn_scan_body(uin_ref, wcum_ref, qdec_ref, ktail_ref, attn_ref, gtot_ref, z_ref, ow_ref, s0_ref,
                   o_ref, sf_ref, s_ref, *, ng):
    c = pl.program_id(2)
    cs = GDN_CHUNK

    @pl.when(c == 0)
    def _():
        s_ref[...] = s0_ref[0]

    tn = (((0,), (0,)), ((), ()))
    sls = [slice(e * DV_C, (e + 1) * DV_C) for e in range(ng)]
    ss = [s_ref[e] for e in range(ng)]
    s16 = [s.astype(bf16) for s in ss]
    us = [uin_ref[:, sls[e]] - jnp.dot(wcum_ref[:, sls[e]].astype(bf16), s16[e], preferred_element_type=f32)
          for e in range(ng)]
    u16 = [u.astype(bf16) for u in us]
    os_ = [jnp.dot(qdec_ref[:, sls[e]].astype(bf16), s16[e], preferred_element_type=f32)
           + jnp.dot(attn_ref[:, e * cs:(e + 1) * cs].astype(bf16), u16[e], preferred_element_type=f32)
           for e in range(ng)]
    states = [ss[e] * jnp.concatenate([gtot_ref[:, sls[e]]] * (DK_C // cs), axis=0)
              + lax.dot_general(ktail_ref[:, sls[e]].astype(bf16), u16[e], tn, preferred_element_type=f32)
              for e in range(ng)]
    outs = []
    for e in range(ng):
        o = os_[e] * lax.rsqrt(jnp.mean(os_[e] * os_[e], axis=-1, keepdims=True) + RMS_EPS) * ow_ref[...]
        z = z_ref[:, sls[e]]
        outs.append(o * (z * _sigmoid(z)))
    for e in range(ng):
        s_ref[e] = states[e]
        o_ref[:, sls[e]] = outs[e]

    @pl.when(c == pl.num_programs(2) - 1)
    def _():
        sf_ref[0] = s_ref[...]


def _gdn_mixer(p, t_len, conv_buf, s0, prm):
    b, tp, wd = p.shape
    cs = GDN_CHUNK
    assert tp % cs == 0 and DK_C == DV_C and DK_C % cs == 0
    n = tp // cs
    rep = HV_C // HK_C
    p2 = p.reshape(b * tp, wd)
    cw = prm['conv_w'].astype(f32)
    cb = jnp.pad(conv_buf.astype(f32), ((0, 0), (SUBLANES - (CONV_W - 1), 0), (0, 0)))
    gp = jnp.zeros((2, LANES), f32).at[0, HV_C:2 * HV_C].set(prm['a_log'].astype(f32))
    gp = gp.at[1, HV_C:2 * HV_C].set(prm['dt_bias'].astype(f32))
    nq = QK_C // DK_C
    vw = rep * DV_C
    nsub = math.gcd(n, 4)
    rows = nsub * cs
    ns = n // nsub
    row = lambda bi, hk, c: bi * ns + c
    hv_out = jax.ShapeDtypeStruct((b * tp, V_C), f32)
    hspec = pl.BlockSpec((rows, vw), lambda bi, hk, c: (row(bi, hk, c), hk))
    uin, wcum, qdec, ktail, attn, gtot = pl.pallas_call(
        functools.partial(_gdn_prep_body, t_len=t_len, nsub=nsub),
        grid=(b, HK_C, ns),
        in_specs=[pl.BlockSpec((rows, DK_C), lambda bi, hk, c: (row(bi, hk, c), hk)),
                  pl.BlockSpec((rows, DK_C), lambda bi, hk, c: (row(bi, hk, c), nq + hk)),
                  pl.BlockSpec((rows, vw), lambda bi, hk, c: (row(bi, hk, c), 2 * QK_C // vw + hk)),
                  pl.BlockSpec((rows, LANES), lambda bi, hk, c: (row(bi, hk, c), (CONV_CH + V_C) // LANES)),
                  pl.BlockSpec((1, SUBLANES, DK_C), lambda bi, hk, c: (bi, 0, hk)),
                  pl.BlockSpec((1, SUBLANES, DK_C), lambda bi, hk, c: (bi, 0, nq + hk)),
                  pl.BlockSpec((1, SUBLANES, vw), lambda bi, hk, c: (bi, 0, 2 * QK_C // vw + hk)),
                  pl.BlockSpec((CONV_W, DK_C), lambda bi, hk, c: (0, hk)),
                  pl.BlockSpec((CONV_W, DK_C), lambda bi, hk, c: (0, nq + hk)),
                  pl.BlockSpec((CONV_W, vw), lambda bi, hk, c: (0, 2 * QK_C // vw + hk)),
                  pl.BlockSpec((2, LANES), lambda bi, hk, c: (0, 0))],
        out_specs=[hspec, hspec, hspec, hspec,
                   pl.BlockSpec((rows, rep * cs), lambda bi, hk, c: (row(bi, hk, c), hk)), hspec],
        out_shape=[hv_out, hv_out, hv_out, hv_out, jax.ShapeDtypeStruct((b * tp, HK_C * rep * cs), f32), hv_out],
        scratch_shapes=[pltpu.VMEM((SUBLANES, DK_C), f32), pltpu.VMEM((SUBLANES, DK_C), f32),
                        pltpu.VMEM((SUBLANES, vw), f32)],
        compiler_params=_cparams("parallel", "parallel", "arbitrary"),
        name="gdn_prep",
    )(p2, p2, p2, p2, cb, cb, cb, cw, cw, cw, gp)

    ng = 8
    gw = ng * DV_C
    gspec = pl.BlockSpec((cs, gw), lambda bi, hg, c: (bi * n + c, hg))
    sspec = pl.BlockSpec((1, ng, DK_C, DV_C), lambda bi, hg, c: (bi, hg, 0, 0))
    o, sf = pl.pallas_call(
        functools.partial(_gdn_scan_body, ng=ng),
        grid=(b, HV_C // ng, n),
        in_specs=[gspec, gspec, gspec, gspec,
                  pl.BlockSpec((cs, ng * cs), lambda bi, hg, c: (bi * n + c, hg)), gspec,
                  pl.BlockSpec((cs, gw), lambda bi, hg, c: (bi * n + c, CONV_CH // gw + hg)),
                  pl.BlockSpec((1, DV_C), lambda bi, hg, c: (0, 0)), sspec],
        out_specs=[gspec, sspec],
        out_shape=[hv_out, jax.ShapeDtypeStruct((b, HV_C, DK_C, DV_C), f32)],
        scratch_shapes=[pltpu.VMEM((ng, DK_C, DV_C), f32)],
        compiler_params=_cparams("parallel", "parallel", "arbitrary"),
        name="gdn_scan",
    )(uin, wcum, qdec, ktail, attn, gtot, p2, prm['onorm_w'].astype(f32).reshape(1, DV_C), s0.astype(f32))
    return o, sf


def _pad_rows(x, rows):
    return jnp.pad(x, ((0, rows - x.shape[0]), (0, 0)))


def kernel(x_prompt, x_sample, cache_k_0, cache_v_0, state_wkv_0, state_shift_0, state_gdn_1, state_conv_1, cache_k_2, cache_v_2, state_wkv_2, state_shift_2, state_gdn_3, state_conv_3, page_table, rel_bias, norm_mix_0, w_in_0, mu_0, w0_0, w2_0, a0_0, a2_0, g2_0, kk_scale_0, ka_mix_0, rk_bonus_0, lnx_w_0, lnx_b_0, w_out_0, norm_ffn_0, peer_wq_0, peer_keys_0, peer_u_0, peer_v_0, norm_mix_1, w_in_1, conv_w_1, a_log_1, dt_bias_1, onorm_w_1, w_out_1, norm_ffn_1, peer_wq_1, peer_keys_1, peer_u_1, peer_v_1, norm_mix_2, w_in_2, mu_2, w0_2, w2_2, a0_2, a2_2, v0_2, v2_2, g2_2, kk_scale_2, ka_mix_2, rk_bonus_2, lnx_w_2, lnx_b_2, w_out_2, norm_ffn_2, peer_wq_2, peer_keys_2, peer_u_2, peer_v_2, norm_mix_3, w_in_3, conv_w_3, a_log_3, dt_bias_3, onorm_w_3, w_out_3, norm_ffn_3, peer_wq_3, peer_keys_3, peer_u_3, peer_v_3, norm_final):
    mix_prm = [
        dict(norm=norm_mix_0, w_in=w_in_0, mu=mu_0, w0=w0_0, w2=w2_0, a0=a0_0, a2=a2_0, g2=g2_0,
             kk_scale=kk_scale_0, ka_mix=ka_mix_0, rk_bonus=rk_bonus_0, lnx_w=lnx_w_0, lnx_b=lnx_b_0,
             w_out=w_out_0),
        dict(norm=norm_mix_1, w_in=w_in_1, conv_w=conv_w_1, a_log=a_log_1, dt_bias=dt_bias_1,
             onorm_w=onorm_w_1, w_out=w_out_1),
        dict(norm=norm_mix_2, w_in=w_in_2, mu=mu_2, w0=w0_2, w2=w2_2, a0=a0_2, a2=a2_2, v0=v0_2, v2=v2_2,
             g2=g2_2, kk_scale=kk_scale_2, ka_mix=ka_mix_2, rk_bonus=rk_bonus_2, lnx_w=lnx_w_2,
             lnx_b=lnx_b_2, w_out=w_out_2),
        dict(norm=norm_mix_3, w_in=w_in_3, conv_w=conv_w_3, a_log=a_log_3, dt_bias=dt_bias_3,
             onorm_w=onorm_w_3, w_out=w_out_3),
    ]
    ffn_prm = [
        (norm_ffn_0, peer_wq_0, peer_keys_0, peer_u_0, peer_v_0),
        (norm_ffn_1, peer_wq_1, peer_keys_1, peer_u_1, peer_v_1),
        (norm_ffn_2, peer_wq_2, peer_keys_2, peer_u_2, peer_v_2),
        (norm_ffn_3, peer_wq_3, peer_keys_3, peer_u_3, peer_v_3),
    ]
    layer_state = [
        (cache_k_0, cache_v_0, state_wkv_0, state_shift_0),
        (state_gdn_1, state_conv_1),
        (cache_k_2, cache_v_2, state_wkv_2, state_shift_2),
        (state_gdn_3, state_conv_3),
    ]
    bp, sp, d = x_prompt.shape
    bs, ts, _ = x_sample.shape
    ns = bs * ts
    ns_mm = -(-ns // 16) * 16
    ns_peer = -(-ns // LANES) * LANES
    xp = x_prompt.reshape(bp * sp, d)
    xs = x_sample.reshape(ns, d)
    depth = len(mix_prm)
    vf_p = vf_s = None
    new = []
    for i in range(depth):
        prm = dict(mix_prm[i])
        prm['w_in'] = prm['w_in'].astype(bf16)
        prm['w_out'] = prm['w_out'].astype(bf16)
        xs_pad = _pad_rows(xs, ns_mm)
        pp = _matmul(xp, prm['w_in'], gain=prm['norm'])
        ps = _matmul(xs_pad, prm['w_in'], gain=prm['norm'])[:ns]
        if i % 2 == 0:
            ck, cv, wkv_in, shift_in = layer_state[i]
            first = i == 0
            rw = prm['w_in'].shape[1] - 3 * C_A
            o_a = _moba_prompt(pp, rel_bias, bp, sp)
            p_rw = pp[:, 3 * C_A:].reshape(bp, sp, rw)
            p_prev = jnp.concatenate([jnp.zeros((bp, 1, rw), f32), p_rw[:, :-1]], axis=1)
            o_b, wkv_p, vf_p = _rwkv7(p_rw, p_prev, jnp.zeros((bp, H_B, DH_B, DH_B), f32), vf_p, prm, first)
            cat = jnp.concatenate([o_a, o_b.reshape(bp * sp, C_B)], axis=-1)
            xp_new = _matmul(cat, prm['w_out'], res=xp)
            kp = pp[:, C_A:2 * C_A].reshape(bp, sp // PAGE_SIZE, PAGE_SIZE, H_A, DH_A)
            vp = pp[:, 2 * C_A:3 * C_A].reshape(bp, sp // PAGE_SIZE, PAGE_SIZE, H_A, DH_A)
            hp_last = _rms(xp.reshape(bp, sp, d)[:, -1], prm['norm'])
            qs = ps[:, :C_A].reshape(bs, ts, H_A, DH_A)
            ks = ps[:, C_A:2 * C_A].reshape(bs, ts, H_A, DH_A)
            vs = ps[:, 2 * C_A:3 * C_A].reshape(bs, ts, H_A, DH_A)
            o_as = _moba_sample(qs, ks, vs, ck, cv, page_table, rel_bias).reshape(ns, C_A)
            ps_rw = ps[:, 3 * C_A:].reshape(bs, ts, rw)
            prev0 = _matmul(_pad_rows(shift_in, ns_mm), prm['w_in'], col0=3 * C_A)[:bs]
            ps_prev = jnp.concatenate([prev0[:, None], ps_rw[:, :-1]], axis=1)
            o_bs, wkv_s, vf_s = _rwkv7(ps_rw, ps_prev, wkv_in, vf_s, prm, first)
            cat_s = jnp.concatenate([o_as, o_bs.reshape(ns, C_B)], axis=-1)
            xs_new = _matmul(_pad_rows(cat_s, ns_mm), prm['w_out'], res=xs_pad)[:ns]
            hs_last = _rms(xs.reshape(bs, ts, d)[:, -1], prm['norm'])
            new.append((kp, vp, ks, vs, wkv_p, wkv_s, hp_last, hs_last))
        else:
            gdn_in, conv_in = layer_state[i]
            wd = prm['w_in'].shape[1]

            def gdn(p3, conv_buf, s0):
                bb, tt, _ = p3.shape
                tpad = -(-tt // GDN_CHUNK) * GDN_CHUNK
                o, sf = _gdn_mixer(jnp.pad(p3, ((0, 0), (0, tpad - tt), (0, 0))), tt, conv_buf, s0, prm)
                conv = jnp.concatenate([conv_buf.astype(f32), p3[..., :CONV_CH]], axis=1)[:, -(CONV_W - 1):]
                return o.reshape(bb, tpad, V_C)[:, :tt].reshape(bb * tt, V_C), conv, sf

            o_p, conv_p, gdn_p = gdn(pp.reshape(bp, sp, wd), jnp.zeros((bp, CONV_W - 1, CONV_CH), f32),
                                     jnp.zeros((bp, HV_C, DK_C, DV_C), f32))
            o_s, conv_s, gdn_s = gdn(ps.reshape(bs, ts, wd), conv_in, gdn_in)
            xp_new = _matmul(o_p, prm['w_out'], res=xp)
            xs_new = _matmul(_pad_rows(o_s, ns_mm), prm['w_out'], res=xs_pad)[:ns]
            new.append((gdn_p, gdn_s, conv_p, conv_s))
        xp, xs = xp_new, xs_new
        g_ffn, wq, sub_keys, u_tab, v_tab = ffn_prm[i]
        wq_bf, u_bf, v_bf = wq.astype(bf16), u_tab.astype(bf16), v_tab.astype(bf16)
        keys = sub_keys.astype(f32)
        xp = _peer(xp, g_ffn, wq_bf, keys, u_bf, v_bf)
        xs = _peer(_pad_rows(xs, ns_peer), g_ffn, wq_bf, keys, u_bf, v_bf)[:ns]
    y_prompt = _rms(xp, norm_final).reshape(bp, sp, d)
    y_sample = _rms(xs, norm_final).reshape(bs, ts, d)
    out = [y_prompt, y_sample]
    for layer in new:
        out.extend(layer)
    return tuple(out)
```

```python
import functools
import math

import jax
import jax.numpy as jnp
from jax import lax
from jax.experimental import pallas as pl
from jax.experimental.pallas import tpu as pltpu

f32 = jnp.float32
bf16 = jnp.bfloat16
i32 = jnp.int32

RMS_EPS = 1e-6
NEG_INF = -1e30

PAGE_SIZE = 128
H_A, DH_A = 8, 128
C_A = H_A * DH_A
MOBA_BLOCK, MOBA_TOPK = 256, 3
NUM_BUCKETS, MAX_DISTANCE = 32, 128
H_B, DH_B = 16, 64
C_B = H_B * DH_B
LORA_W, LORA_A, LORA_G, LORA_V = 64, 64, 160, 32
LNX_EPS = 64e-5
HK_C, HV_C, DK_C, DV_C = 16, 32, 128, 128
QK_C, V_C = HK_C * DK_C, HV_C * DV_C
CONV_W = 4
CONV_CH = 2 * QK_C + V_C
GDN_CHUNK = 64
PEER_HEADS, N_KEYS, PEER_DK, PEER_TOPK = 8, 128, 256, 16
N_EXPERTS = N_KEYS * N_KEYS

LANES = 128
SUBLANES = 8
VMEM_LIMIT = 56 * 1024 * 1024


def _cparams(*sem):
    return pltpu.CompilerParams(dimension_semantics=sem, vmem_limit_bytes=VMEM_LIMIT)


def _rms(x, g):
    return x * lax.rsqrt(jnp.mean(x * x, axis=-1, keepdims=True) + RMS_EPS) * g


def _mm_body(*refs, norm, has_res):
    if has_res:
        x_ref, g_ref, w_ref, res_ref, o_ref, h_ref = refs
    else:
        x_ref, g_ref, w_ref, o_ref, h_ref = refs

    @pl.when(pl.program_id(1) == 0)
    def _():
        x = x_ref[...]
        if norm:
            x = _rms(x, g_ref[...])
        h_ref[...] = x.astype(bf16)

    acc = jnp.dot(h_ref[...], w_ref[...], preferred_element_type=f32)
    if has_res:
        acc = res_ref[...] + acc
    o_ref[...] = acc


def _matmul(x, w, gain=None, res=None, col0=0, ncols=None, tn=512):
    m, k = x.shape
    ncols = w.shape[1] - col0 if ncols is None else ncols
    assert col0 % tn == 0 and m % 16 == 0 and w.dtype == bf16
    tm = min(m, 1024 if k <= 2048 else 512)
    assert m % tm == 0
    nj = pl.cdiv(ncols, tn)
    j0 = col0 // tn
    norm = gain is not None
    g = (gain if norm else jnp.ones((k,), f32)).reshape(1, k)
    in_specs = [pl.BlockSpec((tm, k), lambda i, j: (i, 0)),
                pl.BlockSpec((1, k), lambda i, j: (0, 0)),
                pl.BlockSpec((k, tn), lambda i, j: (0, j + j0))]
    args = [x, g, w]
    if res is not None:
        in_specs.append(pl.BlockSpec((tm, tn), lambda i, j: (i, j)))
        args.append(res)
    return pl.pallas_call(
        functools.partial(_mm_body, norm=norm, has_res=res is not None),
        grid=(m // tm, nj),
        in_specs=in_specs,
        out_specs=pl.BlockSpec((tm, tn), lambda i, j: (i, j)),
        out_shape=jax.ShapeDtypeStruct((m, ncols), f32),
        scratch_shapes=[pltpu.VMEM((tm, k), bf16)],
        compiler_params=_cparams("parallel", "arbitrary"),
        name="matmul",
    )(*args)


def _t5_bucket(dist):
    n = jnp.maximum(dist, 0)
    max_exact = NUM_BUCKETS // 2
    ratio = jnp.log(jnp.maximum(n, 1).astype(f32) / max_exact) / math.log(MAX_DISTANCE / max_exact)
    large = max_exact + (ratio * (NUM_BUCKETS - max_exact)).astype(i32)
    return jnp.where(n < max_exact, n, jnp.minimum(large, NUM_BUCKETS - 1))


def _moba_prompt_body(q_ref, k_ref, v_ref, bias_ref, o_ref, kmh_ref, kml_ref, kb_ref, vb_ref, *, nb, hp):
    qb = pl.program_id(2)
    blk = MOBA_BLOCK
    scale = DH_A ** -0.5
    nt = (((1,), (1,)), ((), ()))
    hs = [slice(e * DH_A, (e + 1) * DH_A) for e in range(hp)]
    nbr = -(-nb // SUBLANES) * SUBLANES

    @pl.when(qb == 0)
    def _():
        k = k_ref[...]
        kmh_ref[...] = jnp.zeros_like(kmh_ref)
        kml_ref[...] = jnp.zeros_like(kml_ref)
        for e in range(hp):
            km = jnp.mean(k[:, hs[e]].reshape(nb, blk, DH_A), axis=1)
            hi = km.astype(bf16)
            kmh_ref[e, 0:nb, :] = hi
            kml_ref[e, 0:nb, :] = (km - hi.astype(f32)).astype(bf16)
        kb_ref[...] = k.astype(bf16)
        vb_ref[...] = v_ref[...].astype(bf16)

    q = q_ref[...]
    q16 = [q[:, hs[e]].astype(bf16) for e in range(hp)]
    rowj = lax.broadcasted_iota(i32, (nbr, blk), 0)
    past = rowj < qb
    lhs = []
    for e in range(hp):
        qlo = (q[:, hs[e]] - q16[e].astype(f32)).astype(bf16)
        gate = (lax.dot_general(kmh_ref[e], q16[e], nt, preferred_element_type=f32)
                + lax.dot_general(kmh_ref[e], qlo, nt, preferred_element_type=f32)
                + lax.dot_general(kml_ref[e], q16[e], nt, preferred_element_type=f32))[0:nbr]
        g = jnp.where(past, gate, NEG_INF)
        sel = jnp.zeros(g.shape, f32)
        for _ in range(MOBA_TOPK):
            m = jnp.max(g, axis=0, keepdims=True)
            idx = jnp.min(jnp.where(g == m, rowj, nbr), axis=0, keepdims=True)
            pick = rowj == idx
            sel = jnp.where(pick, jnp.where(past, 1.0, 0.0), sel)
            g = jnp.where(pick, -jnp.inf, g)
        sel = jnp.where(rowj == qb, 1.0, sel)
        selneg = jnp.where(sel > 0.0, 0.0, NEG_INF / scale)
        selneg = jnp.concatenate([selneg, jnp.zeros((LANES - nbr, blk), f32)], axis=0).T
        lhs.append(jnp.concatenate([q16[e], selneg.astype(bf16)], axis=1))

    rowi = qb * blk + lax.broadcasted_iota(i32, (blk, blk), 0)
    coli = lax.broadcasted_iota(i32, (blk, blk), 1)
    lane = lax.broadcasted_iota(i32, (blk, LANES), 1)

    def tile(j, carry):
        ks = pl.multiple_of(j * blk, blk)
        boff = pl.multiple_of((nb - 1 - qb + j) * blk, blk)
        causal = coli + j * blk <= rowi
        onehot = jnp.where(lane == j, 1.0, 0.0).astype(bf16)
        ss = [lax.dot_general(lhs[e], jnp.concatenate([kb_ref[pl.ds(ks, blk), hs[e]], onehot], axis=1), nt,
                              preferred_element_type=f32) * scale + bias_ref[e, :, pl.ds(boff, blk)]
              for e in range(hp)]
        ss = [jnp.where(causal, s, NEG_INF) for s in ss]
        ms = [jnp.maximum(carry[e][0], jnp.max(ss[e], axis=1, keepdims=True)) for e in range(hp)]
        ps = [jnp.exp(ss[e] - ms[e]) for e in range(hp)]
        als = [jnp.exp(carry[e][0] - ms[e]) for e in range(hp)]
        return tuple((ms[e], als[e] * carry[e][1] + jnp.sum(ps[e], axis=1, keepdims=True),
                      als[e] * carry[e][2] + jnp.dot(ps[e].astype(bf16), vb_ref[pl.ds(ks, blk), hs[e]],
                                                     preferred_element_type=f32)) for e in range(hp))

    init = tuple((jnp.full((blk, 1), -jnp.inf, f32), jnp.zeros((blk, 1), f32), jnp.zeros((blk, DH_A), f32))
                 for _ in range(hp))
    res = lax.fori_loop(0, qb + 1, tile, init)
    o_ref[...] = jnp.concatenate([acc / l for _, l, acc in res], axis=1)


def _bias_strip_body(f_ref, o_ref, *, width):
    row = jnp.broadcast_to(f_ref[0], (MOBA_BLOCK, f_ref.shape[2]))
    o_ref[0] = pltpu.roll(row, 0, 1, stride=1, stride_axis=0)[:, :width]


def _bias_strip(rel_bias, nb):
    blk = MOBA_BLOCK
    width = (2 * nb - 1) * blk
    padded = width + blk
    x = jnp.arange(padded)
    dist = jnp.where(x < width, (nb - 1) * blk - x, (nb - 1) * blk + padded - x)
    prof = rel_bias.astype(f32)[_t5_bucket(dist)].T.reshape(H_A, 1, padded)
    return pl.pallas_call(
        functools.partial(_bias_strip_body, width=width),
        grid=(H_A,),
        in_specs=[pl.BlockSpec((1, 1, padded), lambda h: (h, 0, 0))],
        out_specs=pl.BlockSpec((1, blk, width), lambda h: (h, 0, 0)),
        out_shape=jax.ShapeDtypeStruct((H_A, blk, width), f32),
        compiler_params=_cparams("parallel"),
        name="moba_bias_strip",
    )(prof)


def _moba_prompt(p, rel_bias, b, s_len):
    blk = MOBA_BLOCK
    assert s_len % blk == 0
    nb = s_len // blk
    strip = _bias_strip(rel_bias, nb)
    hp = 2
    hw = hp * DH_A
    ng = C_A // hw
    return pl.pallas_call(
        functools.partial(_moba_prompt_body, nb=nb, hp=hp),
        grid=(b, ng, nb),
        in_specs=[pl.BlockSpec((blk, hw), lambda bi, h, qb: (bi * nb + qb, h)),
                  pl.BlockSpec((s_len, hw), lambda bi, h, qb: (bi, ng + h)),
                  pl.BlockSpec((s_len, hw), lambda bi, h, qb: (bi, 2 * ng + h)),
                  pl.BlockSpec((hp, blk, (2 * nb - 1) * blk), lambda bi, h, qb: (h, 0, 0))],
        out_specs=pl.BlockSpec((blk, hw), lambda bi, h, qb: (bi * nb + qb, h)),
        out_shape=jax.ShapeDtypeStruct((b * s_len, C_A), f32),
        scratch_shapes=[pltpu.VMEM((hp, LANES, DH_A), bf16), pltpu.VMEM((hp, LANES, DH_A), bf16),
                        pltpu.VMEM((s_len, hw), bf16), pltpu.VMEM((s_len, hw), bf16)],
        compiler_params=_cparams("parallel", "parallel", "arbitrary"),
        name="moba_prompt",
    )(p, p, p, strip)


def _tree_sum(xs):
    xs = list(xs)
    while len(xs) > 1:
        xs = [xs[i] + xs[i + 1] for i in range(0, len(xs) - 1, 2)] + ([xs[-1]] if len(xs) % 2 else [])
    return xs[0]


def _wkv_body(r_ref, w_ref, k_ref, kk_ref, b_ref, v_ref, s0_ref, y_ref, sf_ref, s_ref, *, tc):
    @pl.when(pl.program_id(1) == 0)
    def _():
        s_ref[...] = s0_ref[0]

    half = LANES // 2
    ni, hn = s_ref.shape[0], s_ref.shape[1]
    pair = 2

    def step(t, c):
        def rows(ref):
            return [ref[0, t, pl.ds(j, 1), :] for j in range(hn)]

        r, w, k, kk, b = rows(r_ref), rows(w_ref), rows(k_ref), rows(kk_ref), rows(b_ref)
        for i0 in range(0, ni, pair):
            its = range(i0, i0 + pair)
            vs = [v_ref[0, t, pl.ds(it * SUBLANES, SUBLANES), :] for it in its]
            ss = [[s_ref[it, j] for j in range(hn)] for it in its]
            t1 = [_tree_sum(s[j] * kk[j] for j in range(hn)) for s in ss]
            sa = [-(x + pltpu.roll(x, half, 1)) for x in t1]
            ss = [[s[j] * w[j] + sa_ * b[j] + v_ * k[j] for j in range(hn)] for s, sa_, v_ in zip(ss, sa, vs)]
            t2 = [_tree_sum(s[j] * r[j] for j in range(hn)) for s in ss]
            ys = [x + pltpu.roll(x, half, 1) for x in t2]
            for it, s, y in zip(its, ss, ys):
                for j in range(hn):
                    s_ref[it, j] = s[j]
                y_ref[0, t, pl.ds(it * SUBLANES, SUBLANES), :] = y
        return c

    lax.fori_loop(0, tc, step, 0)

    @pl.when(pl.program_id(1) == pl.num_programs(1) - 1)
    def _():
        sf_ref[0] = s_ref[...]


def _wkv_scan(r, w, k, kk, b, v, s0):
    bsz, t, h, n = r.shape
    assert n == DH_B and (bsz * h) % (LANES // 2) == 0 and n % SUBLANES == 0
    grp = bsz * h // (LANES // 2)
    bg = bsz // grp
    hn = n // 2
    ni = n // SUBLANES

    def vec(z):
        z = z.reshape(grp, bg, t, h, 2, hn)
        return z.transpose(0, 2, 5, 4, 1, 3).reshape(grp, t, hn, LANES)

    def row(z):
        z = z.reshape(grp, bg, t, h, n).transpose(0, 2, 4, 1, 3).reshape(grp, t, n, LANES // 2)
        return jnp.concatenate([z, z], axis=-1)

    s0k = s0.astype(f32).reshape(grp, bg, h, ni, SUBLANES, 2, hn).transpose(0, 3, 6, 4, 5, 1, 2)
    s0k = s0k.reshape(grp, ni, hn, SUBLANES, LANES)
    tc = math.gcd(t, 64)
    vspec = pl.BlockSpec((1, tc, hn, LANES), lambda g, c: (g, c, 0, 0))
    rspec = pl.BlockSpec((1, tc, n, LANES), lambda g, c: (g, c, 0, 0))
    sspec = pl.BlockSpec((1, ni, hn, SUBLANES, LANES), lambda g, c: (g, 0, 0, 0, 0))
    y, sf = pl.pallas_call(
        functools.partial(_wkv_body, tc=tc),
        grid=(grp, t // tc),
        in_specs=[vspec] * 5 + [rspec, sspec],
        out_specs=[rspec, sspec],
        out_shape=[jax.ShapeDtypeStruct((grp, t, n, LANES), f32),
                   jax.ShapeDtypeStruct((grp, ni, hn, SUBLANES, LANES), f32)],
        scratch_shapes=[pltpu.VMEM((ni, hn, SUBLANES, LANES), f32)],
        compiler_params=_cparams("parallel", "arbitrary"),
        name="wkv_scan",
    )(vec(r), vec(w), vec(k), vec(kk), vec(b), row(v), s0k)
    y = y[..., :LANES // 2].reshape(grp, t, n, bg, h).transpose(0, 3, 1, 4, 2).reshape(bsz, t, h, n)
    sf = sf.reshape(grp, ni, hn, SUBLANES, 2, bg, h).transpose(0, 5, 6, 1, 3, 4, 2).reshape(bsz, h, n, n)
    return y, sf


def _top_rows(vs, k, payloads=None):
    rows = lax.broadcasted_iota(i32, vs[0].shape, 0)
    n = vs[0].shape[0]
    vs = list(vs)
    vals = [[] for _ in vs]
    outs = [[] for _ in vs]
    for _ in range(k):
        ms = [jnp.max(v, axis=0, keepdims=True) for v in vs]
        ids = [jnp.min(jnp.where(v == m, rows, n), axis=0, keepdims=True) for v, m in zip(vs, ms)]
        picks = [rows == i for i in ids]
        vs = [jnp.where(p, -jnp.inf, v) for p, v in zip(picks, vs)]
        for j in range(len(vs)):
            vals[j].append(ms[j])
            outs[j].append(ids[j].astype(f32) if payloads is None else
                           jnp.sum(jnp.where(picks[j], payloads[j], 0.0), axis=0, keepdims=True))
    return [(jnp.concatenate(v, axis=0), jnp.concatenate(o, axis=0)) for v, o in zip(vals, outs)]


_PAIR_CNT = [PEER_TOPK // (m + 1) for m in range(PEER_TOPK)]
_PAIR_PAD = -sum(_PAIR_CNT) % SUBLANES


def _route_body(x_ref, g_ref, wq_ref, keys_ref, xn_ref, i1_ref, i2_ref, gate_ref, q_scr, e_scr, w_scr, *, tt):
    xn = _rms(x_ref[...], g_ref[...]).astype(bf16)
    xn_ref[...] = xn
    q_scr[...] = jnp.dot(xn, wq_ref[...], preferred_element_type=f32)
    hw = PEER_DK // 2
    nt = (((1,), (1,)), ((), ()))

    def head(h, c):
        row = pl.multiple_of(h * PEER_TOPK, PEER_TOPK)
        nsub = tt // LANES
        sts = []
        for sub in range(nsub):
            for p in range(2):
                col = pl.multiple_of(h * PEER_DK + p * hw, hw)
                qp = q_scr[sub * LANES:(sub + 1) * LANES, pl.ds(col, hw)]
                sts.append(lax.dot_general(keys_ref[p], qp, nt, preferred_element_type=f32))
        tops = _top_rows(sts, PEER_TOPK)
        cands, ceids = [], []
        for sub in range(nsub):
            (sv1, si1), (sv2, si2) = tops[2 * sub], tops[2 * sub + 1]
            si1 = si1 * float(N_KEYS)
            cands.append(jnp.concatenate([sv1[m:m + 1] + sv2[:_PAIR_CNT[m]] for m in range(PEER_TOPK)]
                                         + [jnp.full((_PAIR_PAD, LANES), -jnp.inf, f32)], axis=0))
            ceids.append(jnp.concatenate([si1[m:m + 1] + si2[:_PAIR_CNT[m]] for m in range(PEER_TOPK)]
                                         + [jnp.zeros((_PAIR_PAD, LANES), f32)], axis=0))
        joint = _top_rows(cands, PEER_TOPK, payloads=ceids)
        for sub in range(nsub):
            cv, eid = joint[sub]
            ex = jnp.exp(cv - cv[0:1])
            e_scr[pl.ds(row, PEER_TOPK), sub * LANES:(sub + 1) * LANES] = eid
            w_scr[pl.ds(row, PEER_TOPK), sub * LANES:(sub + 1) * LANES] = ex / jnp.sum(ex, axis=0, keepdims=True)
        return c

    lax.fori_loop(0, PEER_HEADS, head, 0)
    eid = e_scr[...].T
    i1 = jnp.floor(eid * (1.0 / N_KEYS))
    i1_ref[...] = i1
    i2_ref[...] = eid - i1 * float(N_KEYS)
    gate_ref[...] = w_scr[...].T


def _peer_route(x, gain, wq_bf, keys):
    t, d = x.shape
    tt = min(t, 256)
    assert t % tt == 0 and tt % LANES == 0
    nsel = PEER_HEADS * PEER_TOPK
    sel = jax.ShapeDtypeStruct((t, nsel), f32)
    sspec = pl.BlockSpec((tt, nsel), lambda i: (i, 0))
    return pl.pallas_call(
        functools.partial(_route_body, tt=tt),
        grid=(t // tt,),
        in_specs=[pl.BlockSpec((tt, d), lambda i: (i, 0)),
                  pl.BlockSpec((1, d), lambda i: (0, 0)),
                  pl.BlockSpec(wq_bf.shape, lambda i: (0, 0)),
                  pl.BlockSpec(keys.shape, lambda i: (0, 0, 0))],
        out_specs=[pl.BlockSpec((tt, d), lambda i: (i, 0)), sspec, sspec, sspec],
        out_shape=[jax.ShapeDtypeStruct((t, d), bf16), sel, sel, sel],
        scratch_shapes=[pltpu.VMEM((tt, wq_bf.shape[1]), f32), pltpu.VMEM((nsel, tt), f32),
                        pltpu.VMEM((nsel, tt), f32)],
        compiler_params=_cparams("parallel"),
        name="peer_route",
    )(x, gain.reshape(1, d), wq_bf, keys)


def _gate_build_body(i1_ref, i2_ref, g_ref, o_ref, *, tb):
    keyrow = lax.broadcasted_iota(i32, (N_KEYS, i1_ref.shape[1]), 0).astype(f32)
    nt = (((1,), (1,)), ((), ()))

    def toks(i, c):
        t0 = pl.multiple_of(i * SUBLANES, SUBLANES)
        i1 = i1_ref[pl.ds(t0, SUBLANES), :]
        i2 = i2_ref[pl.ds(t0, SUBLANES), :]
        g = g_ref[pl.ds(t0, SUBLANES), :]
        ghi = g.astype(bf16).astype(f32)
        glo = g - ghi
        ops = []
        for r in range(SUBLANES):
            a = jnp.where(keyrow == i1[r:r + 1], 1.0, 0.0).astype(bf16)
            m2 = keyrow == i2[r:r + 1]
            bhi = jnp.where(m2, ghi[r:r + 1], 0.0).astype(bf16)
            blo = jnp.where(m2, glo[r:r + 1], 0.0).astype(bf16)
            ops.append((jnp.concatenate([a, a], axis=1), jnp.concatenate([bhi, blo], axis=1)))
        gs = [lax.dot_general(a2, b2, nt, preferred_element_type=f32) for a2, b2 in ops]
        o_ref[i] = jnp.swapaxes(jnp.stack(gs, axis=0), 0, 1)
        return c

    lax.fori_loop(0, tb // SUBLANES, toks, 0)


def _gate_build(i1, i2, gate):
    t, nsel = i1.shape
    tb = min(t, 128)
    assert t % tb == 0 and tb % SUBLANES == 0
    spec = pl.BlockSpec((tb, nsel), lambda i: (i, 0))
    return pl.pallas_call(
        functools.partial(_gate_build_body, tb=tb),
        grid=(t // tb,),
        in_specs=[spec, spec, spec],
        out_specs=pl.BlockSpec((tb // SUBLANES, N_KEYS, SUBLANES, N_KEYS), lambda i: (i, 0, 0, 0)),
        out_shape=jax.ShapeDtypeStruct((t // SUBLANES, N_KEYS, SUBLANES, N_KEYS), f32),
        compiler_params=_cparams("parallel"),
        name="peer_gate_build",
    )(i1, i2, gate)


def _peer_mix_body(x_ref, g_ref, u_ref, v_ref, o_ref, *, na, tt):
    j = pl.program_id(1)

    @pl.when(j == 0)
    def _():
        o_ref[...] = jnp.zeros_like(o_ref)

    h = lax.dot_general(x_ref[...], u_ref[...], (((1,), (1,)), ((), ())), preferred_element_type=f32)
    parts = []
    for a in range(na):
        ha = h[:, a * N_KEYS:(a + 1) * N_KEYS]
        act = 0.5 * ha * (1.0 + lax.erf(ha * (2.0 ** -0.5)))
        parts.append((g_ref[:, a].reshape(tt, N_KEYS) * act).astype(bf16))
    p = jnp.concatenate(parts, axis=1)
    o_ref[...] += jnp.dot(p, v_ref[...], preferred_element_type=f32)


def _peer_mix(xn, g4, u_bf, v_bf):
    t, d = xn.shape
    tt = min(t, 1024)
    na = 4
    et = na * N_KEYS
    assert t % tt == 0 and tt % SUBLANES == 0
    return pl.pallas_call(
        functools.partial(_peer_mix_body, na=na, tt=tt),
        grid=(t // tt, N_EXPERTS // et),
        in_specs=[pl.BlockSpec((tt, d), lambda i, j: (i, 0)),
                  pl.BlockSpec((tt // SUBLANES, na, SUBLANES, N_KEYS), lambda i, j: (i, j, 0, 0)),
                  pl.BlockSpec((et, d), lambda i, j: (j, 0)),
                  pl.BlockSpec((et, d), lambda i, j: (j, 0))],
        out_specs=pl.BlockSpec((tt, d), lambda i, j: (i, 0)),
        out_shape=jax.ShapeDtypeStruct((t, d), f32),
        compiler_params=_cparams("parallel", "arbitrary"),
        name="peer_mix",
    )(xn, g4, u_bf, v_bf)


def _peer(x, gain, wq_bf, keys, u_bf, v_bf):
    xn, i1, i2, gate = _peer_route(x, gain, wq_bf, keys)
    return x + _peer_mix(xn, _gate_build(i1, i2, gate), u_bf, v_bf)


def _l2norm(x, eps):
    return x * lax.rsqrt(jnp.sum(x * x, axis=-1, keepdims=True) + eps)


def _rwkv7(p_cur, p_prev, s0, v_first, prm, first):
    b, t, _ = p_cur.shape
    m = (p_cur + (p_prev - p_cur) * prm['mu']).astype(f32)
    r, k, v = m[..., :C_B], m[..., C_B:2 * C_B], m[..., 2 * C_B:3 * C_B]
    o = 3 * C_B
    wl = m[..., o:o + LORA_W]
    o += LORA_W
    al = m[..., o:o + LORA_A]
    o += LORA_A
    gl = m[..., o:o + LORA_G]
    o += LORA_G
    w_log = -jax.nn.softplus(-(prm['w0'] + jnp.tanh(wl) @ prm['w2'])) - 0.5
    decay = jnp.exp(-jnp.exp(w_log.astype(f32)))
    a = jax.nn.sigmoid((prm['a0'] + al @ prm['a2']).astype(f32))
    g = (jax.nn.sigmoid(gl) @ prm['g2']).astype(f32)
    if first:
        v_first = v
    else:
        vl = m[..., o:o + LORA_V]
        v = v + (v_first - v) * jax.nn.sigmoid((prm['v0'] + vl @ prm['v2']).astype(f32))

    def hd(z):
        return z.reshape(b, t, H_B, DH_B)

    kk = _l2norm(hd(k * prm['kk_scale']), 1e-24)
    k = k * (1.0 + (a - 1.0) * prm['ka_mix'])
    r_h, k_h, v_h = hd(r), hd(k), hd(v)
    b_h = kk * hd(a)
    y, s_fin = _wkv_scan(r_h, hd(decay), k_h, kk, b_h, v_h, s0)
    mu = jnp.mean(y, axis=-1, keepdims=True)
    var = jnp.mean(jnp.square(y - mu), axis=-1, keepdims=True)
    y = ((y - mu) * lax.rsqrt(var + LNX_EPS)).reshape(b, t, C_B) * prm['lnx_w'] + prm['lnx_b']
    bonus = jnp.sum(r_h * k_h * prm['rk_bonus'], axis=-1, keepdims=True) * v_h
    y = (y + bonus.reshape(b, t, C_B)) * g
    return y, s_fin.astype(s0.dtype), v_first


def _select_blocks(q, kmean, qblk):
    n_sel = min(MOBA_TOPK, kmean.shape[2])
    gate = jnp.einsum('bhqd,bhjd->bhqj', q.astype(f32), kmean, precision=lax.Precision.HIGHEST)
    past = jnp.arange(kmean.shape[2])[None, :] < qblk[:, None]
    _, idx = lax.top_k(jnp.where(past, gate, NEG_INF), n_sel)
    return idx, idx < qblk[:, None]


def _kmean_body(pt_ref, *refs, ppb):
    o_ref = refs[-1]
    pages = refs[:-1]
    for g in range(len(pages) // ppb):
        acc = jnp.sum(pages[g * ppb][0], axis=0)
        for r in range(1, ppb):
            acc = acc + jnp.sum(pages[g * ppb + r][0], axis=0)
        o_ref[0, g] = acc * (1.0 / MOBA_BLOCK)


def _paged_block_means(cache, page_table):
    bd, n_pages = page_table.shape
    ppb = MOBA_BLOCK // PAGE_SIZE
    nbp = n_pages // ppb
    bps = math.gcd(nbp, 4)
    hd = cache.shape[2:]
    specs = [pl.BlockSpec((1, PAGE_SIZE) + hd,
                          functools.partial(lambda b, j, pt, r: (pt[b, j * (bps * ppb) + r], 0, 0, 0), r=r))
             for r in range(bps * ppb)]
    return pl.pallas_call(
        functools.partial(_kmean_body, ppb=ppb),
        grid_spec=pltpu.PrefetchScalarGridSpec(
            num_scalar_prefetch=1, grid=(bd, nbp // bps), in_specs=specs,
            out_specs=pl.BlockSpec((1, bps) + hd, lambda b, j, pt: (b, j, 0, 0))),
        out_shape=jax.ShapeDtypeStruct((bd, nbp) + hd, f32),
        compiler_params=_cparams("parallel", "arbitrary"),
        name="moba_block_means",
    )(page_table, *([cache] * (bps * ppb)))


def _moba_decode_body(pp_ref, lp_ref, ok_ref, q_ref, kn_ref, vn_ref, ob_ref, k_ref, v_ref, b_ref, o_ref,
                      m_ref, l_ref, acc_ref, *, ppb):
    b, h, s = pl.program_id(0), pl.program_id(1), pl.program_id(2)
    scale = DH_A ** -0.5
    q = q_ref[0]

    @pl.when(s == 0)
    def _():
        m_ref[...] = jnp.sum(q * kn_ref[0], axis=1, keepdims=True) * scale + ob_ref[0][:, 0:1]
        l_ref[...] = jnp.ones_like(l_ref)
        acc_ref[...] = vn_ref[0]

    def update(kp, vp):
        sc = lax.dot_general(q.astype(bf16), kp.astype(bf16), (((1,), (1,)), ((), ())),
                             preferred_element_type=f32) * scale + b_ref[0, 0]
        sc = jnp.where(ok_ref[b, h, s // ppb] != 0, sc, NEG_INF)
        m_old = m_ref[...]
        m_new = jnp.maximum(m_old, jnp.max(sc, axis=1, keepdims=True))
        alpha = jnp.exp(m_old - m_new)
        p = jnp.exp(sc - m_new)
        m_ref[...] = m_new
        l_ref[...] = alpha * l_ref[...] + jnp.sum(p, axis=1, keepdims=True)
        acc_ref[...] = alpha * acc_ref[...] + jnp.dot(p.astype(bf16), vp.astype(bf16), preferred_element_type=f32)

    for hh in range(H_A):
        @pl.when(h == hh)
        def _(hh=hh):
            update(k_ref[0, :, hh, :], v_ref[0, :, hh, :])

    @pl.when(s == pl.num_programs(2) - 1)
    def _():
        o_ref[0] = acc_ref[...] / l_ref[...]


def _moba_sample(q, k, v, cache_k, cache_v, page_table, rel_bias):
    bd, tn = q.shape[:2]
    n_pages = page_table.shape[1]
    past = n_pages * PAGE_SIZE
    assert tn == 1 and past % MOBA_BLOCK == 0 and MOBA_BLOCK % PAGE_SIZE == 0
    ppb = MOBA_BLOCK // PAGE_SIZE
    km_past = _paged_block_means(cache_k, page_table)
    km_new = k.astype(f32).reshape(bd, 1, H_A, DH_A) * (1.0 / MOBA_BLOCK)
    kmean = jnp.concatenate([km_past, km_new], axis=1).transpose(0, 2, 1, 3)
    qpos = past + jnp.arange(tn)
    idx, ok = _select_blocks(q.transpose(0, 2, 1, 3), kmean, qpos // MOBA_BLOCK)
    idx, ok = idx[:, :, 0], ok[:, :, 0]
    n_sel = idx.shape[-1]
    lpage = jnp.clip((idx[..., None] * ppb + jnp.arange(ppb)).reshape(bd, H_A, n_sel * ppb), 0, n_pages - 1)
    ppage = jnp.take_along_axis(page_table[:, None, :], lpage, axis=2).astype(i32)
    okp = ok.astype(i32)
    pos = jnp.arange(past)
    bias_pos = rel_bias.astype(f32)[_t5_bucket(qpos[0] - pos)].T.reshape(H_A, n_pages, 1, PAGE_SIZE)
    own_bias = jnp.broadcast_to(rel_bias.astype(f32)[_t5_bucket(jnp.zeros((), i32))][:, None, None], (H_A, 1, LANES))
    rep8 = lambda z: jnp.broadcast_to(z.reshape(bd, 1, C_A).astype(f32), (bd, SUBLANES, C_A))
    hspec = pl.BlockSpec((1, SUBLANES, DH_A), lambda b, h, s, pp, lp, okr: (b, 0, h))
    page = pl.BlockSpec((1, PAGE_SIZE, H_A, DH_A), lambda b, h, s, pp, lp, okr: (pp[b, h, s], 0, 0, 0))
    o = pl.pallas_call(
        functools.partial(_moba_decode_body, ppb=ppb),
        grid_spec=pltpu.PrefetchScalarGridSpec(
            num_scalar_prefetch=3, grid=(bd, H_A, n_sel * ppb),
            in_specs=[hspec, hspec, hspec,
                      pl.BlockSpec((1, 1, LANES), lambda b, h, s, pp, lp, okr: (h, 0, 0)),
                      page, page,
                      pl.BlockSpec((1, 1, 1, PAGE_SIZE), lambda b, h, s, pp, lp, okr: (h, lp[b, h, s], 0, 0))],
            out_specs=hspec,
            scratch_shapes=[pltpu.VMEM((SUBLANES, 1), f32), pltpu.VMEM((SUBLANES, 1), f32),
                            pltpu.VMEM((SUBLANES, DH_A), f32)]),
        out_shape=jax.ShapeDtypeStruct((bd, SUBLANES, C_A), f32),
        compiler_params=_cparams("parallel", "parallel", "arbitrary"),
        name="moba_decode",
    )(ppage, lpage.astype(i32), okp, rep8(q), rep8(k), rep8(v), own_bias, cache_k, cache_v, bias_pos)
    return o[:, 0].reshape(bd, tn, H_A, DH_A)


def _sigmoid(x):
    return 1.0 / (1.0 + jnp.exp(-x))


def _dot3(a, b):
    ah = a.astype(bf16)
    al = (a - ah.astype(f32)).astype(bf16)
    bh = b.astype(bf16)
    bl = (b - bh.astype(f32)).astype(bf16)
    return (jnp.dot(ah, bh, preferred_element_type=f32) + jnp.dot(ah, bl, preferred_element_type=f32)
            + jnp.dot(al, bh, preferred_element_type=f32))


def _col_to_row(col, eye):
    return jnp.sum(jnp.where(eye, col, 0.0), axis=0, keepdims=True)


def _gdn_prep_body(q_ref, k_ref, v_ref, bg_ref, cq_ref, ck_ref, cv_ref, wq_ref, wk_ref, wv_ref, gp_ref,
                   uin_ref, wcum_ref, qdec_ref, ktail_ref, attn_ref, gtot_ref, pq_ref, pk_ref, pv_ref, *, t_len, nsub):
    hk = pl.program_id(1)
    c = pl.program_id(2)
    cs = GDN_CHUNK
    rep = HV_C // HK_C
    rows = nsub * cs

    @pl.when(c == 0)
    def _():
        pq_ref[...] = cq_ref[0]
        pk_ref[...] = ck_ref[0]
        pv_ref[...] = cv_ref[0]

    def conv_silu(x_ref, prev_ref, w_ref):
        cur = x_ref[...]
        ext = jnp.concatenate([prev_ref[...], cur], axis=0)
        acc = cur * w_ref[CONV_W - 1:CONV_W, :]
        for i in range(CONV_W - 1):
            lo = SUBLANES - (CONV_W - 1) + i
            acc = acc + ext[lo:lo + rows] * w_ref[i:i + 1, :]
        prev_ref[...] = cur[rows - SUBLANES:]
        return acc * _sigmoid(acc)

    def l2n(x):
        return x * lax.rsqrt(jnp.sum(x * x, axis=-1, keepdims=True) + 1e-6)

    q_all = l2n(conv_silu(q_ref, pq_ref, wq_ref)) * (DK_C ** -0.5)
    k_all = l2n(conv_silu(k_ref, pk_ref, wk_ref))
    v_all = conv_silu(v_ref, pv_ref, wv_ref)

    bg = bg_ref[...]
    beta_all = _sigmoid(bg)
    xg = bg + gp_ref[1:2, :]
    g_all = -jnp.exp(gp_ref[0:1, :]) * (jnp.maximum(xg, 0.0) + jnp.log1p(jnp.exp(-jnp.abs(xg))))
    lane = lax.broadcasted_iota(i32, bg.shape, 1)
    if t_len % cs:
        live = (c * rows + lax.broadcasted_iota(i32, (rows, 1), 0)) < t_len
    hsel = []
    for e in range(rep):
        hv = hk * rep + e
        beta = jnp.sum(jnp.where(lane == hv, beta_all, 0.0), axis=1, keepdims=True)
        g = jnp.sum(jnp.where(lane == hv + HV_C, g_all, 0.0), axis=1, keepdims=True)
        if t_len % cs:
            beta = jnp.where(live, beta, 0.0)
            g = jnp.where(live, g, 0.0)
        hsel.append((beta, g))
    ri = lax.broadcasted_iota(i32, (cs, cs), 0)
    ci = lax.broadcasted_iota(i32, (cs, cs), 1)
    eye = ri == ci
    tril = ci <= ri
    strict = ci < ri
    nt = (((1,), (1,)), ((), ()))
    results = []

    chains = []
    for sub in range(nsub):
        rs = slice(sub * cs, (sub + 1) * cs)
        q, k = q_all[rs], k_all[rs]
        kb16 = k.astype(bf16)
        kk = lax.dot_general(kb16, kb16, nt, preferred_element_type=f32)
        qk = lax.dot_general(q.astype(bf16), kb16, nt, preferred_element_type=f32)
        for e in range(rep):
            beta, g = hsel[e][0][rs], hsel[e][1][rs]
            gc = jnp.sum(jnp.where(tril, _col_to_row(g, eye), 0.0), axis=1, keepdims=True)
            gc_row = _col_to_row(gc, eye)
            decay = jnp.where(tril, jnp.exp(jnp.where(tril, gc - gc_row, 0.0)), 0.0)
            pw = jnp.where(strict, -(kk * beta * decay), 0.0)
            chains.append(dict(sub=sub, e=e, q=q, k=k, qk=qk, beta=beta, gc=gc, decay=decay, pw=pw,
                               inv=jnp.where(eye, 1.0, pw)))
    for _ in range(5):
        for ch in chains:
            ch['pw'] = _dot3(ch['pw'], ch['pw'])
        for ch in chains:
            ch['inv'] = ch['inv'] + _dot3(ch['inv'], ch['pw'])
    for ch in chains:
        sub, e, q, k, beta, gc = ch['sub'], ch['e'], ch['q'], ch['k'], ch['beta'], ch['gc']
        e_col = jnp.exp(gc)
        v = v_all[sub * cs:(sub + 1) * cs, e * DV_C:(e + 1) * DV_C]
        rhs = jnp.concatenate([v * beta, k * (beta * e_col)], axis=1)
        sol = _dot3(ch['inv'], rhs)
        gl = jnp.sum(jnp.where(ri[:, 0:1] == cs - 1, gc, 0.0), axis=0, keepdims=True)
        results.append((sub, e, sol[:, :DV_C], sol[:, DV_C:], q * e_col, k * jnp.exp(gl - gc),
                        jnp.where(tril, ch['qk'] * ch['decay'], 0.0), jnp.broadcast_to(jnp.exp(gl), (cs, DV_C))))

    for sub, e, uin, wcum, qdec, ktail, attn, gtot in results:
        rs = slice(sub * cs, (sub + 1) * cs)
        sl = slice(e * DV_C, (e + 1) * DV_C)
        uin_ref[rs, sl] = uin
        wcum_ref[rs, sl] = wcum
        qdec_ref[rs, sl] = qdec
        ktail_ref[rs, sl] = ktail
        attn_ref[rs, e * cs:(e + 1) * cs] = attn
        gtot_ref[rs, sl] = gtot


def _gdn_scan_body(uin_ref, wcum_ref, qdec_ref, ktail_ref, attn_ref, gtot_ref, z_ref, ow_ref, s0_ref,
                   o_ref, sf_ref, s_ref, *, ng):
    c = pl.program_id(2)
    cs = GDN_CHUNK

    @pl.when(c == 0)
    def _():
        s_ref[...] = s0_ref[0]

    tn = (((0,), (0,)), ((), ()))
    sls = [slice(e * DV_C, (e + 1) * DV_C) for e in range(ng)]
    ss = [s_ref[e] for e in range(ng)]
    s16 = [s.astype(bf16) for s in ss]
    us = [uin_ref[:, sls[e]] - jnp.dot(wcum_ref[:, sls[e]].astype(bf16), s16[e], preferred_element_type=f32)
          for e in range(ng)]
    u16 = [u.astype(bf16) for u in us]
    os_ = [jnp.dot(qdec_ref[:, sls[e]].astype(bf16), s16[e], preferred_element_type=f32)
           + jnp.dot(attn_ref[:, e * cs:(e + 1) * cs].astype(bf16), u16[e], preferred_element_type=f32)
           for e in range(ng)]
    states = [ss[e] * jnp.concatenate([gtot_ref[:, sls[e]]] * (DK_C // cs), axis=0)
              + lax.dot_general(ktail_ref[:, sls[e]].astype(bf16), u16[e], tn, preferred_element_type=f32)
              for e in range(ng)]
    outs = []
    for e in range(ng):
        o = os_[e] * lax.rsqrt(jnp.mean(os_[e] * os_[e], axis=-1, keepdims=True) + RMS_EPS) * ow_ref[...]
        z = z_ref[:, sls[e]]
        outs.append(o * (z * _sigmoid(z)))
    for e in range(ng):
        s_ref[e] = states[e]
        o_ref[:, sls[e]] = outs[e]

    @pl.when(c == pl.num_programs(2) - 1)
    def _():
        sf_ref[0] = s_ref[...]


def _gdn_mixer(p, t_len, conv_buf, s0, prm):
    b, tp, wd = p.shape
    cs = GDN_CHUNK
    assert tp % cs == 0 and DK_C == DV_C and DK_C % cs == 0
    n = tp // cs
    rep = HV_C // HK_C
    p2 = p.reshape(b * tp, wd)
    cw = prm['conv_w'].astype(f32)
    cb = jnp.pad(conv_buf.astype(f32), ((0, 0), (SUBLANES - (CONV_W - 1), 0), (0, 0)))
    gp = jnp.zeros((2, LANES), f32).at[0, HV_C:2 * HV_C].set(prm['a_log'].astype(f32))
    gp = gp.at[1, HV_C:2 * HV_C].set(prm['dt_bias'].astype(f32))
    nq = QK_C // DK_C
    vw = rep * DV_C
    nsub = math.gcd(n, 8)
    rows = nsub * cs
    ns = n // nsub
    row = lambda bi, hk, c: bi * ns + c
    hv_out = jax.ShapeDtypeStruct((b * tp, V_C), f32)
    hspec = pl.BlockSpec((rows, vw), lambda bi, hk, c: (row(bi, hk, c), hk))
    uin, wcum, qdec, ktail, attn, gtot = pl.pallas_call(
        functools.partial(_gdn_prep_body, t_len=t_len, nsub=nsub),
        grid=(b, HK_C, ns),
        in_specs=[pl.BlockSpec((rows, DK_C), lambda bi, hk, c: (row(bi, hk, c), hk)),
                  pl.BlockSpec((rows, DK_C), lambda bi, hk, c: (row(bi, hk, c), nq + hk)),
                  pl.BlockSpec((rows, vw), lambda bi, hk, c: (row(bi, hk, c), 2 * QK_C // vw + hk)),
                  pl.BlockSpec((rows, LANES), lambda bi, hk, c: (row(bi, hk, c), (CONV_CH + V_C) // LANES)),
                  pl.BlockSpec((1, SUBLANES, DK_C), lambda bi, hk, c: (bi, 0, hk)),
                  pl.BlockSpec((1, SUBLANES, DK_C), lambda bi, hk, c: (bi, 0, nq + hk)),
                  pl.BlockSpec((1, SUBLANES, vw), lambda bi, hk, c: (bi, 0, 2 * QK_C // vw + hk)),
                  pl.BlockSpec((CONV_W, DK_C), lambda bi, hk, c: (0, hk)),
                  pl.BlockSpec((CONV_W, DK_C), lambda bi, hk, c: (0, nq + hk)),
                  pl.BlockSpec((CONV_W, vw), lambda bi, hk, c: (0, 2 * QK_C // vw + hk)),
                  pl.BlockSpec((2, LANES), lambda bi, hk, c: (0, 0))],
        out_specs=[hspec, hspec, hspec, hspec,
                   pl.BlockSpec((rows, rep * cs), lambda bi, hk, c: (row(bi, hk, c), hk)), hspec],
        out_shape=[hv_out, hv_out, hv_out, hv_out, jax.ShapeDtypeStruct((b * tp, HK_C * rep * cs), f32), hv_out],
        scratch_shapes=[pltpu.VMEM((SUBLANES, DK_C), f32), pltpu.VMEM((SUBLANES, DK_C), f32),
                        pltpu.VMEM((SUBLANES, vw), f32)],
        compiler_params=_cparams("parallel", "parallel", "arbitrary"),
        name="gdn_prep",
    )(p2, p2, p2, p2, cb, cb, cb, cw, cw, cw, gp)

    ng = 8
    gw = ng * DV_C
    gspec = pl.BlockSpec((cs, gw), lambda bi, hg, c: (bi * n + c, hg))
    sspec = pl.BlockSpec((1, ng, DK_C, DV_C), lambda bi, hg, c: (bi, hg, 0, 0))
    o, sf = pl.pallas_call(
        functools.partial(_gdn_scan_body, ng=ng),
        grid=(b, HV_C // ng, n),
        in_specs=[gspec, gspec, gspec, gspec,
                  pl.BlockSpec((cs, ng * cs), lambda bi, hg, c: (bi * n + c, hg)), gspec,
                  pl.BlockSpec((cs, gw), lambda bi, hg, c: (bi * n + c, CONV_CH // gw + hg)),
                  pl.BlockSpec((1, DV_C), lambda bi, hg, c: (0, 0)), sspec],
        out_specs=[gspec, sspec],
        out_shape=[hv_out, jax.ShapeDtypeStruct((b, HV_C, DK_C, DV_C), f32)],
        scratch_shapes=[pltpu.VMEM((ng, DK_C, DV_C), f32)],
        compiler_params=_cparams("parallel", "parallel", "arbitrary"),
        name="gdn_scan",
    )(uin, wcum, qdec, ktail, attn, gtot, p2, prm['onorm_w'].astype(f32).reshape(1, DV_C), s0.astype(f32))
    return o, sf


def _pad_rows(x, rows):
    return jnp.pad(x, ((0, rows - x.shape[0]), (0, 0)))


def kernel(x_prompt, x_sample, cache_k_0, cache_v_0, state_wkv_0, state_shift_0, state_gdn_1, state_conv_1, cache_k_2, cache_v_2, state_wkv_2, state_shift_2, state_gdn_3, state_conv_3, page_table, rel_bias, norm_mix_0, w_in_0, mu_0, w0_0, w2_0, a0_0, a2_0, g2_0, kk_scale_0, ka_mix_0, rk_bonus_0, lnx_w_0, lnx_b_0, w_out_0, norm_ffn_0, peer_wq_0, peer_keys_0, peer_u_0, peer_v_0, norm_mix_1, w_in_1, conv_w_1, a_log_1, dt_bias_1, onorm_w_1, w_out_1, norm_ffn_1, peer_wq_1, peer_keys_1, peer_u_1, peer_v_1, norm_mix_2, w_in_2, mu_2, w0_2, w2_2, a0_2, a2_2, v0_2, v2_2, g2_2, kk_scale_2, ka_mix_2, rk_bonus_2, lnx_w_2, lnx_b_2, w_out_2, norm_ffn_2, peer_wq_2, peer_keys_2, peer_u_2, peer_v_2, norm_mix_3, w_in_3, conv_w_3, a_log_3, dt_bias_3, onorm_w_3, w_out_3, norm_ffn_3, peer_wq_3, peer_keys_3, peer_u_3, peer_v_3, norm_final):
    mix_prm = [
        dict(norm=norm_mix_0, w_in=w_in_0, mu=mu_0, w0=w0_0, w2=w2_0, a0=a0_0, a2=a2_0, g2=g2_0,
             kk_scale=kk_scale_0, ka_mix=ka_mix_0, rk_bonus=rk_bonus_0, lnx_w=lnx_w_0, lnx_b=lnx_b_0,
             w_out=w_out_0),
        dict(norm=norm_mix_1, w_in=w_in_1, conv_w=conv_w_1, a_log=a_log_1, dt_bias=dt_bias_1,
             onorm_w=onorm_w_1, w_out=w_out_1),
        dict(norm=norm_mix_2, w_in=w_in_2, mu=mu_2, w0=w0_2, w2=w2_2, a0=a0_2, a2=a2_2, v0=v0_2, v2=v2_2,
             g2=g2_2, kk_scale=kk_scale_2, ka_mix=ka_mix_2, rk_bonus=rk_bonus_2, lnx_w=lnx_w_2,
             lnx_b=lnx_b_2, w_out=w_out_2),
        dict(norm=norm_mix_3, w_in=w_in_3, conv_w=conv_w_3, a_log=a_log_3, dt_bias=dt_bias_3,
             onorm_w=onorm_w_3, w_out=w_out_3),
    ]
    ffn_prm = [
        (norm_ffn_0, peer_wq_0, peer_keys_0, peer_u_0, peer_v_0),
        (norm_ffn_1, peer_wq_1, peer_keys_1, peer_u_1, peer_v_1),
        (norm_ffn_2, peer_wq_2, peer_keys_2, peer_u_2, peer_v_2),
        (norm_ffn_3, peer_wq_3, peer_keys_3, peer_u_3, peer_v_3),
    ]
    layer_state = [
        (cache_k_0, cache_v_0, state_wkv_0, state_shift_0),
        (state_gdn_1, state_conv_1),
        (cache_k_2, cache_v_2, state_wkv_2, state_shift_2),
        (state_gdn_3, state_conv_3),
    ]
    bp, sp, d = x_prompt.shape
    bs, ts, _ = x_sample.shape
    ns = bs * ts
    ns_mm = -(-ns // 16) * 16
    ns_peer = -(-ns // LANES) * LANES
    xp = x_prompt.reshape(bp * sp, d)
    xs = x_sample.reshape(ns, d)
    depth = len(mix_prm)
    vf_p = vf_s = None
    new = []
    for i in range(depth):
        prm = dict(mix_prm[i])
        prm['w_in'] = prm['w_in'].astype(bf16)
        prm['w_out'] = prm['w_out'].astype(bf16)
        xs_pad = _pad_rows(xs, ns_mm)
        pp = _matmul(xp, prm['w_in'], gain=prm['norm'])
        ps = _matmul(xs_pad, prm['w_in'], gain=prm['norm'])[:ns]
        if i % 2 == 0:
            ck, cv, wkv_in, shift_in = layer_state[i]
            first = i == 0
            rw = prm['w_in'].shape[1] - 3 * C_A
            o_a = _moba_prompt(pp, rel_bias, bp, sp)
            p_rw = pp[:, 3 * C_A:].reshape(bp, sp, rw)
            p_prev = jnp.concatenate([jnp.zeros((bp, 1, rw), f32), p_rw[:, :-1]], axis=1)
            o_b, wkv_p, vf_p = _rwkv7(p_rw, p_prev, jnp.zeros((bp, H_B, DH_B, DH_B), f32), vf_p, prm, first)
            cat = jnp.concatenate([o_a, o_b.reshape(bp * sp, C_B)], axis=-1)
            xp_new = _matmul(cat, prm['w_out'], res=xp)
            kp = pp[:, C_A:2 * C_A].reshape(bp, sp // PAGE_SIZE, PAGE_SIZE, H_A, DH_A)
            vp = pp[:, 2 * C_A:3 * C_A].reshape(bp, sp // PAGE_SIZE, PAGE_SIZE, H_A, DH_A)
            hp_last = _rms(xp.reshape(bp, sp, d)[:, -1], prm['norm'])
            qs = ps[:, :C_A].reshape(bs, ts, H_A, DH_A)
            ks = ps[:, C_A:2 * C_A].reshape(bs, ts, H_A, DH_A)
            vs = ps[:, 2 * C_A:3 * C_A].reshape(bs, ts, H_A, DH_A)
            o_as = _moba_sample(qs, ks, vs, ck, cv, page_table, rel_bias).reshape(ns, C_A)
            ps_rw = ps[:, 3 * C_A:].reshape(bs, ts, rw)
            prev0 = _matmul(_pad_rows(shift_in, ns_mm), prm['w_in'], col0=3 * C_A)[:bs]
            ps_prev = jnp.concatenate([prev0[:, None], ps_rw[:, :-1]], axis=1)
            o_bs, wkv_s, vf_s = _rwkv7(ps_rw, ps_prev, wkv_in, vf_s, prm, first)
            cat_s = jnp.concatenate([o_as, o_bs.reshape(ns, C_B)], axis=-1)
            xs_new = _matmul(_pad_rows(cat_s, ns_mm), prm['w_out'], res=xs_pad)[:ns]
            hs_last = _rms(xs.reshape(bs, ts, d)[:, -1], prm['norm'])
            new.append((kp, vp, ks, vs, wkv_p, wkv_s, hp_last, hs_last))
        else:
            gdn_in, conv_in = layer_state[i]
            wd = prm['w_in'].shape[1]

            def gdn(p3, conv_buf, s0):
                bb, tt, _ = p3.shape
                tpad = -(-tt // GDN_CHUNK) * GDN_CHUNK
                o, sf = _gdn_mixer(jnp.pad(p3, ((0, 0), (0, tpad - tt), (0, 0))), tt, conv_buf, s0, prm)
                conv = jnp.concatenate([conv_buf.astype(f32), p3[..., :CONV_CH]], axis=1)[:, -(CONV_W - 1):]
                return o.reshape(bb, tpad, V_C)[:, :tt].reshape(bb * tt, V_C), conv, sf

            o_p, conv_p, gdn_p = gdn(pp.reshape(bp, sp, wd), jnp.zeros((bp, CONV_W - 1, CONV_CH), f32),
                                     jnp.zeros((bp, HV_C, DK_C, DV_C), f32))
            o_s, conv_s, gdn_s = gdn(ps.reshape(bs, ts, wd), conv_in, gdn_in)
            xp_new = _matmul(o_p, prm['w_out'], res=xp)
            xs_new = _matmul(_pad_rows(o_s, ns_mm), prm['w_out'], res=xs_pad)[:ns]
            new.append((gdn_p, gdn_s, conv_p, conv_s))
        xp, xs = xp_new, xs_new
        g_ffn, wq, sub_keys, u_tab, v_tab = ffn_prm[i]
        wq_bf, u_bf, v_bf = wq.astype(bf16), u_tab.astype(bf16), v_tab.astype(bf16)
        keys = sub_keys.astype(f32)
        xp = _peer(xp, g_ffn, wq_bf, keys, u_bf, v_bf)
        xs = _peer(_pad_rows(xs, ns_peer), g_ffn, wq_bf, keys, u_bf, v_bf)[:ns]
    y_prompt = _rms(xp, norm_final).reshape(bp, sp, d)
    y_sample = _rms(xs, norm_final).reshape(bs, ts, d)
    out = [y_prompt, y_sample]
    for layer in new:
        out.extend(layer)
    return tuple(out)
```

```python
import functools
import math

import jax
import jax.numpy as jnp
from jax import lax
from jax.experimental import pallas as pl
from jax.experimental.pallas import tpu as pltpu

f32 = jnp.float32
bf16 = jnp.bfloat16
i32 = jnp.int32

RMS_EPS = 1e-6
NEG_INF = -1e30

PAGE_SIZE = 128
H_A, DH_A = 8, 128
C_A = H_A * DH_A
MOBA_BLOCK, MOBA_TOPK = 256, 3
NUM_BUCKETS, MAX_DISTANCE = 32, 128
H_B, DH_B = 16, 64
C_B = H_B * DH_B
LORA_W, LORA_A, LORA_G, LORA_V = 64, 64, 160, 32
LNX_EPS = 64e-5
HK_C, HV_C, DK_C, DV_C = 16, 32, 128, 128
QK_C, V_C = HK_C * DK_C, HV_C * DV_C
CONV_W = 4
CONV_CH = 2 * QK_C + V_C
GDN_CHUNK = 64
PEER_HEADS, N_KEYS, PEER_DK, PEER_TOPK = 8, 128, 256, 16
N_EXPERTS = N_KEYS * N_KEYS

LANES = 128
SUBLANES = 8
VMEM_LIMIT = 56 * 1024 * 1024


def _cparams(*sem):
    return pltpu.CompilerParams(dimension_semantics=sem, vmem_limit_bytes=VMEM_LIMIT)


def _rms(x, g):
    return x * lax.rsqrt(jnp.mean(x * x, axis=-1, keepdims=True) + RMS_EPS) * g


def _mm_body(*refs, norm, has_res):
    if has_res:
        x_ref, g_ref, w_ref, res_ref, o_ref, h_ref = refs
    else:
        x_ref, g_ref, w_ref, o_ref, h_ref = refs

    @pl.when(pl.program_id(1) == 0)
    def _():
        x = x_ref[...]
        if norm:
            x = _rms(x, g_ref[...])
        h_ref[...] = x.astype(bf16)

    acc = jnp.dot(h_ref[...], w_ref[...], preferred_element_type=f32)
    if has_res:
        acc = res_ref[...] + acc
    o_ref[...] = acc


def _matmul(x, w, gain=None, res=None, col0=0, ncols=None, tn=512):
    m, k = x.shape
    ncols = w.shape[1] - col0 if ncols is None else ncols
    assert col0 % tn == 0 and m % 16 == 0 and w.dtype == bf16
    tm = min(m, 1024 if k <= 2048 else 512)
    assert m % tm == 0
    nj = pl.cdiv(ncols, tn)
    j0 = col0 // tn
    norm = gain is not None
    g = (gain if norm else jnp.ones((k,), f32)).reshape(1, k)
    in_specs = [pl.BlockSpec((tm, k), lambda i, j: (i, 0)),
                pl.BlockSpec((1, k), lambda i, j: (0, 0)),
                pl.BlockSpec((k, tn), lambda i, j: (0, j + j0))]
    args = [x, g, w]
    if res is not None:
        in_specs.append(pl.BlockSpec((tm, tn), lambda i, j: (i, j)))
        args.append(res)
    return pl.pallas_call(
        functools.partial(_mm_body, norm=norm, has_res=res is not None),
        grid=(m // tm, nj),
        in_specs=in_specs,
        out_specs=pl.BlockSpec((tm, tn), lambda i, j: (i, j)),
        out_shape=jax.ShapeDtypeStruct((m, ncols), f32),
        scratch_shapes=[pltpu.VMEM((tm, k), bf16)],
        compiler_params=_cparams("parallel", "arbitrary"),
        name="matmul",
    )(*args)


def _t5_bucket(dist):
    n = jnp.maximum(dist, 0)
    max_exact = NUM_BUCKETS // 2
    ratio = jnp.log(jnp.maximum(n, 1).astype(f32) / max_exact) / math.log(MAX_DISTANCE / max_exact)
    large = max_exact + (ratio * (NUM_BUCKETS - max_exact)).astype(i32)
    return jnp.where(n < max_exact, n, jnp.minimum(large, NUM_BUCKETS - 1))


def _moba_prompt_body(q_ref, k_ref, v_ref, bias_ref, o_ref, kmh_ref, kml_ref, kb_ref, vb_ref, *, nb, hp):
    qb = pl.program_id(2)
    blk = MOBA_BLOCK
    scale = DH_A ** -0.5
    nt = (((1,), (1,)), ((), ()))
    hs = [slice(e * DH_A, (e + 1) * DH_A) for e in range(hp)]
    nbr = -(-nb // SUBLANES) * SUBLANES

    @pl.when(qb == 0)
    def _():
        k = k_ref[...]
        kmh_ref[...] = jnp.zeros_like(kmh_ref)
        kml_ref[...] = jnp.zeros_like(kml_ref)
        for e in range(hp):
            km = jnp.mean(k[:, hs[e]].reshape(nb, blk, DH_A), axis=1)
            hi = km.astype(bf16)
            kmh_ref[e, 0:nb, :] = hi
            kml_ref[e, 0:nb, :] = (km - hi.astype(f32)).astype(bf16)
        kb_ref[...] = k.astype(bf16)
        vb_ref[...] = v_ref[...].astype(bf16)

    q = q_ref[...]
    q16 = [q[:, hs[e]].astype(bf16) for e in range(hp)]
    rowj = lax.broadcasted_iota(i32, (nbr, blk), 0)
    past = rowj < qb
    lhs = []
    for e in range(hp):
        qlo = (q[:, hs[e]] - q16[e].astype(f32)).astype(bf16)
        gate = (lax.dot_general(kmh_ref[e], q16[e], nt, preferred_element_type=f32)
                + lax.dot_general(kmh_ref[e], qlo, nt, preferred_element_type=f32)
                + lax.dot_general(kml_ref[e], q16[e], nt, preferred_element_type=f32))[0:nbr]
        g = jnp.where(past, gate, NEG_INF)
        sel = jnp.zeros(g.shape, f32)
        for _ in range(MOBA_TOPK):
            m = jnp.max(g, axis=0, keepdims=True)
            idx = jnp.min(jnp.where(g == m, rowj, nbr), axis=0, keepdims=True)
            pick = rowj == idx
            sel = jnp.where(pick, jnp.where(past, 1.0, 0.0), sel)
            g = jnp.where(pick, -jnp.inf, g)
        sel = jnp.where(rowj == qb, 1.0, sel)
        selneg = jnp.where(sel > 0.0, 0.0, NEG_INF / scale)
        selneg = jnp.concatenate([selneg, jnp.zeros((LANES - nbr, blk), f32)], axis=0).T
        lhs.append(jnp.concatenate([q16[e], selneg.astype(bf16)], axis=1))

    rowi = qb * blk + lax.broadcasted_iota(i32, (blk, blk), 0)
    coli = lax.broadcasted_iota(i32, (blk, blk), 1)
    lane = lax.broadcasted_iota(i32, (blk, LANES), 1)

    def tile(j, carry):
        ks = pl.multiple_of(j * blk, blk)
        boff = pl.multiple_of((nb - 1 - qb + j) * blk, blk)
        causal = coli + j * blk <= rowi
        onehot = jnp.where(lane == j, 1.0, 0.0).astype(bf16)
        ss = [lax.dot_general(lhs[e], jnp.concatenate([kb_ref[pl.ds(ks, blk), hs[e]], onehot], axis=1), nt,
                              preferred_element_type=f32) * scale + bias_ref[e, :, pl.ds(boff, blk)]
              for e in range(hp)]
        ss = [jnp.where(causal, s, NEG_INF) for s in ss]
        ms = [jnp.maximum(carry[e][0], jnp.max(ss[e], axis=1, keepdims=True)) for e in range(hp)]
        ps = [jnp.exp(ss[e] - ms[e]) for e in range(hp)]
        als = [jnp.exp(carry[e][0] - ms[e]) for e in range(hp)]
        return tuple((ms[e], als[e] * carry[e][1] + jnp.sum(ps[e], axis=1, keepdims=True),
                      als[e] * carry[e][2] + jnp.dot(ps[e].astype(bf16), vb_ref[pl.ds(ks, blk), hs[e]],
                                                     preferred_element_type=f32)) for e in range(hp))

    init = tuple((jnp.full((blk, 1), -jnp.inf, f32), jnp.zeros((blk, 1), f32), jnp.zeros((blk, DH_A), f32))
                 for _ in range(hp))
    res = lax.fori_loop(0, qb + 1, tile, init)
    o_ref[...] = jnp.concatenate([acc / l for _, l, acc in res], axis=1)


def _bias_strip_body(f_ref, o_ref, *, width):
    row = jnp.broadcast_to(f_ref[0], (MOBA_BLOCK, f_ref.shape[2]))
    o_ref[0] = pltpu.roll(row, 0, 1, stride=1, stride_axis=0)[:, :width]


def _bias_strip(rel_bias, nb):
    blk = MOBA_BLOCK
    width = (2 * nb - 1) * blk
    padded = width + blk
    x = jnp.arange(padded)
    dist = jnp.where(x < width, (nb - 1) * blk - x, (nb - 1) * blk + padded - x)
    prof = rel_bias.astype(f32)[_t5_bucket(dist)].T.reshape(H_A, 1, padded)
    return pl.pallas_call(
        functools.partial(_bias_strip_body, width=width),
        grid=(H_A,),
        in_specs=[pl.BlockSpec((1, 1, padded), lambda h: (h, 0, 0))],
        out_specs=pl.BlockSpec((1, blk, width), lambda h: (h, 0, 0)),
        out_shape=jax.ShapeDtypeStruct((H_A, blk, width), f32),
        compiler_params=_cparams("parallel"),
        name="moba_bias_strip",
    )(prof)


def _moba_prompt(p, rel_bias, b, s_len):
    blk = MOBA_BLOCK
    assert s_len % blk == 0
    nb = s_len // blk
    strip = _bias_strip(rel_bias, nb)
    hp = 2
    hw = hp * DH_A
    ng = C_A // hw
    return pl.pallas_call(
        functools.partial(_moba_prompt_body, nb=nb, hp=hp),
        grid=(b, ng, nb),
        in_specs=[pl.BlockSpec((blk, hw), lambda bi, h, qb: (bi * nb + qb, h)),
                  pl.BlockSpec((s_len, hw), lambda bi, h, qb: (bi, ng + h)),
                  pl.BlockSpec((s_len, hw), lambda bi, h, qb: (bi, 2 * ng + h)),
                  pl.BlockSpec((hp, blk, (2 * nb - 1) * blk), lambda bi, h, qb: (h, 0, 0))],
        out_specs=pl.BlockSpec((blk, hw), lambda bi, h, qb: (bi * nb + qb, h)),
        out_shape=jax.ShapeDtypeStruct((b * s_len, C_A), f32),
        scratch_shapes=[pltpu.VMEM((hp, LANES, DH_A), bf16), pltpu.VMEM((hp, LANES, DH_A), bf16),
                        pltpu.VMEM((s_len, hw), bf16), pltpu.VMEM((s_len, hw), bf16)],
        compiler_params=_cparams("parallel", "parallel", "arbitrary"),
        name="moba_prompt",
    )(p, p, p, strip)


def _tree_sum(xs):
    xs = list(xs)
    while len(xs) > 1:
        xs = [xs[i] + xs[i + 1] for i in range(0, len(xs) - 1, 2)] + ([xs[-1]] if len(xs) % 2 else [])
    return xs[0]


def _wkv_body(r_ref, w_ref, k_ref, kk_ref, b_ref, v_ref, s0_ref, y_ref, sf_ref, s_ref, *, tc):
    @pl.when(pl.program_id(1) == 0)
    def _():
        s_ref[...] = s0_ref[0]

    half = LANES // 2
    ni, hn = s_ref.shape[0], s_ref.shape[1]
    pair = 4

    def step(t, c):
        def rows(ref):
            return [ref[0, t, pl.ds(j, 1), :] for j in range(hn)]

        r, w, k, kk, b = rows(r_ref), rows(w_ref), rows(k_ref), rows(kk_ref), rows(b_ref)
        for i0 in range(0, ni, pair):
            its = range(i0, i0 + pair)
            vs = [v_ref[0, t, pl.ds(it * SUBLANES, SUBLANES), :] for it in its]
            ss = [[s_ref[it, j] for j in range(hn)] for it in its]
            t1 = [_tree_sum(s[j] * kk[j] for j in range(hn)) for s in ss]
            sa = [-(x + pltpu.roll(x, half, 1)) for x in t1]
            ss = [[s[j] * w[j] + sa_ * b[j] + v_ * k[j] for j in range(hn)] for s, sa_, v_ in zip(ss, sa, vs)]
            t2 = [_tree_sum(s[j] * r[j] for j in range(hn)) for s in ss]
            ys = [x + pltpu.roll(x, half, 1) for x in t2]
            for it, s, y in zip(its, ss, ys):
                for j in range(hn):
                    s_ref[it, j] = s[j]
                y_ref[0, t, pl.ds(it * SUBLANES, SUBLANES), :] = y
        return c

    lax.fori_loop(0, tc, step, 0)

    @pl.when(pl.program_id(1) == pl.num_programs(1) - 1)
    def _():
        sf_ref[0] = s_ref[...]


def _wkv_scan(r, w, k, kk, b, v, s0):
    bsz, t, h, n = r.shape
    assert n == DH_B and (bsz * h) % (LANES // 2) == 0 and n % SUBLANES == 0
    grp = bsz * h // (LANES // 2)
    bg = bsz // grp
    hn = n // 2
    ni = n // SUBLANES

    def vec(z):
        z = z.reshape(grp, bg, t, h, 2, hn)
        return z.transpose(0, 2, 5, 4, 1, 3).reshape(grp, t, hn, LANES)

    def row(z):
        z = z.reshape(grp, bg, t, h, n).transpose(0, 2, 4, 1, 3).reshape(grp, t, n, LANES // 2)
        return jnp.concatenate([z, z], axis=-1)

    s0k = s0.astype(f32).reshape(grp, bg, h, ni, SUBLANES, 2, hn).transpose(0, 3, 6, 4, 5, 1, 2)
    s0k = s0k.reshape(grp, ni, hn, SUBLANES, LANES)
    tc = math.gcd(t, 64)
    vspec = pl.BlockSpec((1, tc, hn, LANES), lambda g, c: (g, c, 0, 0))
    rspec = pl.BlockSpec((1, tc, n, LANES), lambda g, c: (g, c, 0, 0))
    sspec = pl.BlockSpec((1, ni, hn, SUBLANES, LANES), lambda g, c: (g, 0, 0, 0, 0))
    y, sf = pl.pallas_call(
        functools.partial(_wkv_body, tc=tc),
        grid=(grp, t // tc),
        in_specs=[vspec] * 5 + [rspec, sspec],
        out_specs=[rspec, sspec],
        out_shape=[jax.ShapeDtypeStruct((grp, t, n, LANES), f32),
                   jax.ShapeDtypeStruct((grp, ni, hn, SUBLANES, LANES), f32)],
        scratch_shapes=[pltpu.VMEM((ni, hn, SUBLANES, LANES), f32)],
        compiler_params=_cparams("parallel", "arbitrary"),
        name="wkv_scan",
    )(vec(r), vec(w), vec(k), vec(kk), vec(b), row(v), s0k)
    y = y[..., :LANES // 2].reshape(grp, t, n, bg, h).transpose(0, 3, 1, 4, 2).reshape(bsz, t, h, n)
    sf = sf.reshape(grp, ni, hn, SUBLANES, 2, bg, h).transpose(0, 5, 6, 1, 3, 4, 2).reshape(bsz, h, n, n)
    return y, sf


def _top_rows(vs, k, payloads=None):
    rows = lax.broadcasted_iota(i32, vs[0].shape, 0)
    n = vs[0].shape[0]
    vs = list(vs)
    vals = [[] for _ in vs]
    outs = [[] for _ in vs]
    for _ in range(k):
        ms = [jnp.max(v, axis=0, keepdims=True) for v in vs]
        ids = [jnp.min(jnp.where(v == m, rows, n), axis=0, keepdims=True) for v, m in zip(vs, ms)]
        picks = [rows == i for i in ids]
        vs = [jnp.where(p, -jnp.inf, v) for p, v in zip(picks, vs)]
        for j in range(len(vs)):
            vals[j].append(ms[j])
            outs[j].append(ids[j].astype(f32) if payloads is None else
                           jnp.sum(jnp.where(picks[j], payloads[j], 0.0), axis=0, keepdims=True))
    return [(jnp.concatenate(v, axis=0), jnp.concatenate(o, axis=0)) for v, o in zip(vals, outs)]


_PAIR_CNT = [PEER_TOPK // (m + 1) for m in range(PEER_TOPK)]
_PAIR_PAD = -sum(_PAIR_CNT) % SUBLANES


def _route_body(x_ref, g_ref, wq_ref, keys_ref, xn_ref, i1_ref, i2_ref, gate_ref, q_scr, e_scr, w_scr, *, tt):
    xn = _rms(x_ref[...], g_ref[...]).astype(bf16)
    xn_ref[...] = xn
    q_scr[...] = jnp.dot(xn, wq_ref[...], preferred_element_type=f32)
    hw = PEER_DK // 2
    nt = (((1,), (1,)), ((), ()))

    def head(h, c):
        row = pl.multiple_of(h * PEER_TOPK, PEER_TOPK)
        nsub = tt // LANES
        sts = []
        for sub in range(nsub):
            for p in range(2):
                col = pl.multiple_of(h * PEER_DK + p * hw, hw)
                qp = q_scr[sub * LANES:(sub + 1) * LANES, pl.ds(col, hw)]
                sts.append(lax.dot_general(keys_ref[p], qp, nt, preferred_element_type=f32))
        tops = _top_rows(sts, PEER_TOPK)
        cands, ceids = [], []
        for sub in range(nsub):
            (sv1, si1), (sv2, si2) = tops[2 * sub], tops[2 * sub + 1]
            si1 = si1 * float(N_KEYS)
            cands.append(jnp.concatenate([sv1[m:m + 1] + sv2[:_PAIR_CNT[m]] for m in range(PEER_TOPK)]
                                         + [jnp.full((_PAIR_PAD, LANES), -jnp.inf, f32)], axis=0))
            ceids.append(jnp.concatenate([si1[m:m + 1] + si2[:_PAIR_CNT[m]] for m in range(PEER_TOPK)]
                                         + [jnp.zeros((_PAIR_PAD, LANES), f32)], axis=0))
        joint = _top_rows(cands, PEER_TOPK, payloads=ceids)
        for sub in range(nsub):
            cv, eid = joint[sub]
            ex = jnp.exp(cv - cv[0:1])
            e_scr[pl.ds(row, PEER_TOPK), sub * LANES:(sub + 1) * LANES] = eid
            w_scr[pl.ds(row, PEER_TOPK), sub * LANES:(sub + 1) * LANES] = ex / jnp.sum(ex, axis=0, keepdims=True)
        return c

    lax.fori_loop(0, PEER_HEADS, head, 0)
    eid = e_scr[...].T
    i1 = jnp.floor(eid * (1.0 / N_KEYS))
    i1_ref[...] = i1
    i2_ref[...] = eid - i1 * float(N_KEYS)
    gate_ref[...] = w_scr[...].T


def _peer_route(x, gain, wq_bf, keys):
    t, d = x.shape
    tt = min(t, 256)
    assert t % tt == 0 and tt % LANES == 0
    nsel = PEER_HEADS * PEER_TOPK
    sel = jax.ShapeDtypeStruct((t, nsel), f32)
    sspec = pl.BlockSpec((tt, nsel), lambda i: (i, 0))
    return pl.pallas_call(
        functools.partial(_route_body, tt=tt),
        grid=(t // tt,),
        in_specs=[pl.BlockSpec((tt, d), lambda i: (i, 0)),
                  pl.BlockSpec((1, d), lambda i: (0, 0)),
                  pl.BlockSpec(wq_bf.shape, lambda i: (0, 0)),
                  pl.BlockSpec(keys.shape, lambda i: (0, 0, 0))],
        out_specs=[pl.BlockSpec((tt, d), lambda i: (i, 0)), sspec, sspec, sspec],
        out_shape=[jax.ShapeDtypeStruct((t, d), bf16), sel, sel, sel],
        scratch_shapes=[pltpu.VMEM((tt, wq_bf.shape[1]), f32), pltpu.VMEM((nsel, tt), f32),
                        pltpu.VMEM((nsel, tt), f32)],
        compiler_params=_cparams("parallel"),
        name="peer_route",
    )(x, gain.reshape(1, d), wq_bf, keys)


def _gate_build_body(i1_ref, i2_ref, g_ref, o_ref, *, tb):
    keyrow = lax.broadcasted_iota(i32, (N_KEYS, i1_ref.shape[1]), 0).astype(f32).astype(bf16)
    one, zero = jnp.ones((), bf16), jnp.zeros((), bf16)
    nt = (((1,), (1,)), ((), ()))

    ngrp = 2

    def toks(i, c):
        t0 = pl.multiple_of(i * (ngrp * SUBLANES), ngrp * SUBLANES)
        i1 = i1_ref[pl.ds(t0, ngrp * SUBLANES), :]
        i2 = i2_ref[pl.ds(t0, ngrp * SUBLANES), :]
        g = g_ref[pl.ds(t0, ngrp * SUBLANES), :]
        ghi = g.astype(bf16).astype(f32)
        glo = g - ghi
        ops = []
        for r in range(ngrp * SUBLANES):
            a = jnp.where(keyrow == i1[r:r + 1].astype(bf16), one, zero)
            m2 = keyrow == i2[r:r + 1].astype(bf16)
            bhi = jnp.where(m2, ghi[r:r + 1].astype(bf16), zero)
            blo = jnp.where(m2, glo[r:r + 1].astype(bf16), zero)
            ops.append((jnp.concatenate([a, a], axis=1), jnp.concatenate([bhi, blo], axis=1)))
        gs = [lax.dot_general(a2, b2, nt, preferred_element_type=f32) for a2, b2 in ops]
        for q in range(ngrp):
            o_ref[i * ngrp + q] = jnp.swapaxes(jnp.stack(gs[q * SUBLANES:(q + 1) * SUBLANES], axis=0), 0, 1)
        return c

    lax.fori_loop(0, tb // (ngrp * SUBLANES), toks, 0)


def _gate_build(i1, i2, gate):
    t, nsel = i1.shape
    tb = min(t, 128)
    assert t % tb == 0 and tb % (2 * SUBLANES) == 0
    spec = pl.BlockSpec((tb, nsel), lambda i: (i, 0))
    return pl.pallas_call(
        functools.partial(_gate_build_body, tb=tb),
        grid=(t // tb,),
        in_specs=[spec, spec, spec],
        out_specs=pl.BlockSpec((tb // SUBLANES, N_KEYS, SUBLANES, N_KEYS), lambda i: (i, 0, 0, 0)),
        out_shape=jax.ShapeDtypeStruct((t // SUBLANES, N_KEYS, SUBLANES, N_KEYS), f32),
        compiler_params=_cparams("parallel"),
        name="peer_gate_build",
    )(i1, i2, gate)


def _peer_mix_body(x_ref, g_ref, u_ref, v_ref, o_ref, *, na, tt):
    j = pl.program_id(1)

    @pl.when(j == 0)
    def _():
        o_ref[...] = jnp.zeros_like(o_ref)

    h = lax.dot_general(x_ref[...], u_ref[...].astype(bf16), (((1,), (1,)), ((), ())),
                        preferred_element_type=f32)
    parts = []
    for a in range(na):
        ha = h[:, a * N_KEYS:(a + 1) * N_KEYS]
        act = 0.5 * ha * (1.0 + lax.erf(ha * (2.0 ** -0.5)))
        parts.append((g_ref[:, a].reshape(tt, N_KEYS) * act).astype(bf16))
    p = jnp.concatenate(parts, axis=1)
    o_ref[...] += jnp.dot(p, v_ref[...].astype(bf16), preferred_element_type=f32)


def _peer_mix(xn, g4, u_bf, v_bf):
    t, d = xn.shape
    tt = min(t, 1024)
    na = 4
    et = na * N_KEYS
    assert t % tt == 0 and tt % SUBLANES == 0
    return pl.pallas_call(
        functools.partial(_peer_mix_body, na=na, tt=tt),
        grid=(t // tt, N_EXPERTS // et),
        in_specs=[pl.BlockSpec((tt, d), lambda i, j: (i, 0)),
                  pl.BlockSpec((tt // SUBLANES, na, SUBLANES, N_KEYS), lambda i, j: (i, j, 0, 0)),
                  pl.BlockSpec((et, d), lambda i, j: (j, 0)),
                  pl.BlockSpec((et, d), lambda i, j: (j, 0))],
        out_specs=pl.BlockSpec((tt, d), lambda i, j: (i, 0)),
        out_shape=jax.ShapeDtypeStruct((t, d), f32),
        compiler_params=_cparams("parallel", "arbitrary"),
        name="peer_mix",
    )(xn, g4, u_bf, v_bf)


def _peer(x, gain, wq_bf, keys, u_bf, v_bf):
    xn, i1, i2, gate = _peer_route(x, gain, wq_bf, keys)
    return x + _peer_mix(xn, _gate_build(i1, i2, gate), u_bf, v_bf)


def _l2norm(x, eps):
    return x * lax.rsqrt(jnp.sum(x * x, axis=-1, keepdims=True) + eps)


def _rwkv7(p_cur, p_prev, s0, v_first, prm, first):
    b, t, _ = p_cur.shape
    m = (p_cur + (p_prev - p_cur) * prm['mu']).astype(f32)
    r, k, v = m[..., :C_B], m[..., C_B:2 * C_B], m[..., 2 * C_B:3 * C_B]
    o = 3 * C_B
    wl = m[..., o:o + LORA_W]
    o += LORA_W
    al = m[..., o:o + LORA_A]
    o += LORA_A
    gl = m[..., o:o + LORA_G]
    o += LORA_G
    w_log = -jax.nn.softplus(-(prm['w0'] + jnp.tanh(wl) @ prm['w2'])) - 0.5
    decay = jnp.exp(-jnp.exp(w_log.astype(f32)))
    a = jax.nn.sigmoid((prm['a0'] + al @ prm['a2']).astype(f32))
    g = (jax.nn.sigmoid(gl) @ prm['g2']).astype(f32)
    if first:
        v_first = v
    else:
        vl = m[..., o:o + LORA_V]
        v = v + (v_first - v) * jax.nn.sigmoid((prm['v0'] + vl @ prm['v2']).astype(f32))

    def hd(z):
        return z.reshape(b, t, H_B, DH_B)

    kk = _l2norm(hd(k * prm['kk_scale']), 1e-24)
    k = k * (1.0 + (a - 1.0) * prm['ka_mix'])
    r_h, k_h, v_h = hd(r), hd(k), hd(v)
    b_h = kk * hd(a)
    y, s_fin = _wkv_scan(r_h, hd(decay), k_h, kk, b_h, v_h, s0)
    mu = jnp.mean(y, axis=-1, keepdims=True)
    var = jnp.mean(jnp.square(y - mu), axis=-1, keepdims=True)
    y = ((y - mu) * lax.rsqrt(var + LNX_EPS)).reshape(b, t, C_B) * prm['lnx_w'] + prm['lnx_b']
    bonus = jnp.sum(r_h * k_h * prm['rk_bonus'], axis=-1, keepdims=True) * v_h
    y = (y + bonus.reshape(b, t, C_B)) * g
    return y, s_fin.astype(s0.dtype), v_first


def _select_blocks(q, kmean, qblk):
    n_sel = min(MOBA_TOPK, kmean.shape[2])
    gate = jnp.einsum('bhqd,bhjd->bhqj', q.astype(f32), kmean, precision=lax.Precision.HIGHEST)
    past = jnp.arange(kmean.shape[2])[None, :] < qblk[:, None]
    _, idx = lax.top_k(jnp.where(past, gate, NEG_INF), n_sel)
    return idx, idx < qblk[:, None]


def _kmean_body(pt_ref, *refs, ppb):
    o_ref = refs[-1]
    pages = refs[:-1]
    for g in range(len(pages) // ppb):
        acc = jnp.sum(pages[g * ppb][0], axis=0)
        for r in range(1, ppb):
            acc = acc + jnp.sum(pages[g * ppb + r][0], axis=0)
        o_ref[0, g] = acc * (1.0 / MOBA_BLOCK)


def _paged_block_means(cache, page_table):
    bd, n_pages = page_table.shape
    ppb = MOBA_BLOCK // PAGE_SIZE
    nbp = n_pages // ppb
    bps = math.gcd(nbp, 4)
    hd = cache.shape[2:]
    specs = [pl.BlockSpec((1, PAGE_SIZE) + hd,
                          functools.partial(lambda b, j, pt, r: (pt[b, j * (bps * ppb) + r], 0, 0, 0), r=r))
             for r in range(bps * ppb)]
    return pl.pallas_call(
        functools.partial(_kmean_body, ppb=ppb),
        grid_spec=pltpu.PrefetchScalarGridSpec(
            num_scalar_prefetch=1, grid=(bd, nbp // bps), in_specs=specs,
            out_specs=pl.BlockSpec((1, bps) + hd, lambda b, j, pt: (b, j, 0, 0))),
        out_shape=jax.ShapeDtypeStruct((bd, nbp) + hd, f32),
        compiler_params=_cparams("parallel", "arbitrary"),
        name="moba_block_means",
    )(page_table, *([cache] * (bps * ppb)))


def _moba_decode_body(pp_ref, lp_ref, ok_ref, q_ref, kn_ref, vn_ref, ob_ref, k_ref, v_ref, b_ref, o_ref,
                      m_ref, l_ref, acc_ref, *, ppb):
    b, h, s = pl.program_id(0), pl.program_id(1), pl.program_id(2)
    scale = DH_A ** -0.5
    q = q_ref[0]

    @pl.when(s == 0)
    def _():
        m_ref[...] = jnp.sum(q * kn_ref[0], axis=1, keepdims=True) * scale + ob_ref[0][:, 0:1]
        l_ref[...] = jnp.ones_like(l_ref)
        acc_ref[...] = vn_ref[0]

    def update(kp, vp):
        sc = lax.dot_general(q.astype(bf16), kp.astype(bf16), (((1,), (1,)), ((), ())),
                             preferred_element_type=f32) * scale + b_ref[0, 0]
        sc = jnp.where(ok_ref[b, h, s // ppb] != 0, sc, NEG_INF)
        m_old = m_ref[...]
        m_new = jnp.maximum(m_old, jnp.max(sc, axis=1, keepdims=True))
        alpha = jnp.exp(m_old - m_new)
        p = jnp.exp(sc - m_new)
        m_ref[...] = m_new
        l_ref[...] = alpha * l_ref[...] + jnp.sum(p, axis=1, keepdims=True)
        acc_ref[...] = alpha * acc_ref[...] + jnp.dot(p.astype(bf16), vp.astype(bf16), preferred_element_type=f32)

    for hh in range(H_A):
        @pl.when(h == hh)
        def _(hh=hh):
            update(k_ref[0, :, hh, :], v_ref[0, :, hh, :])

    @pl.when(s == pl.num_programs(2) - 1)
    def _():
        o_ref[0] = acc_ref[...] / l_ref[...]


def _moba_sample(q, k, v, cache_k, cache_v, page_table, rel_bias):
    bd, tn = q.shape[:2]
    n_pages = page_table.shape[1]
    past = n_pages * PAGE_SIZE
    assert tn == 1 and past % MOBA_BLOCK == 0 and MOBA_BLOCK % PAGE_SIZE == 0
    ppb = MOBA_BLOCK // PAGE_SIZE
    km_past = _paged_block_means(cache_k, page_table)
    km_new = k.astype(f32).reshape(bd, 1, H_A, DH_A) * (1.0 / MOBA_BLOCK)
    kmean = jnp.concatenate([km_past, km_new], axis=1).transpose(0, 2, 1, 3)
    qpos = past + jnp.arange(tn)
    idx, ok = _select_blocks(q.transpose(0, 2, 1, 3), kmean, qpos // MOBA_BLOCK)
    idx, ok = idx[:, :, 0], ok[:, :, 0]
    n_sel = idx.shape[-1]
    lpage = jnp.clip((idx[..., None] * ppb + jnp.arange(ppb)).reshape(bd, H_A, n_sel * ppb), 0, n_pages - 1)
    ppage = jnp.take_along_axis(page_table[:, None, :], lpage, axis=2).astype(i32)
    okp = ok.astype(i32)
    pos = jnp.arange(past)
    bias_pos = rel_bias.astype(f32)[_t5_bucket(qpos[0] - pos)].T.reshape(H_A, n_pages, 1, PAGE_SIZE)
    own_bias = jnp.broadcast_to(rel_bias.astype(f32)[_t5_bucket(jnp.zeros((), i32))][:, None, None], (H_A, 1, LANES))
    rep8 = lambda z: jnp.broadcast_to(z.reshape(bd, 1, C_A).astype(f32), (bd, SUBLANES, C_A))
    hspec = pl.BlockSpec((1, SUBLANES, DH_A), lambda b, h, s, pp, lp, okr: (b, 0, h))
    page = pl.BlockSpec((1, PAGE_SIZE, H_A, DH_A), lambda b, h, s, pp, lp, okr: (pp[b, h, s], 0, 0, 0))
    o = pl.pallas_call(
        functools.partial(_moba_decode_body, ppb=ppb),
        grid_spec=pltpu.PrefetchScalarGridSpec(
            num_scalar_prefetch=3, grid=(bd, H_A, n_sel * ppb),
            in_specs=[hspec, hspec, hspec,
                      pl.BlockSpec((1, 1, LANES), lambda b, h, s, pp, lp, okr: (h, 0, 0)),
                      page, page,
                      pl.BlockSpec((1, 1, 1, PAGE_SIZE), lambda b, h, s, pp, lp, okr: (h, lp[b, h, s], 0, 0))],
            out_specs=hspec,
            scratch_shapes=[pltpu.VMEM((SUBLANES, 1), f32), pltpu.VMEM((SUBLANES, 1), f32),
                            pltpu.VMEM((SUBLANES, DH_A), f32)]),
        out_shape=jax.ShapeDtypeStruct((bd, SUBLANES, C_A), f32),
        compiler_params=_cparams("parallel", "parallel", "arbitrary"),
        name="moba_decode",
    )(ppage, lpage.astype(i32), okp, rep8(q), rep8(k), rep8(v), own_bias, cache_k, cache_v, bias_pos)
    return o[:, 0].reshape(bd, tn, H_A, DH_A)


def _sigmoid(x):
    return 1.0 / (1.0 + jnp.exp(-x))


def _dot3(a, b):
    ah = a.astype(bf16)
    al = (a - ah.astype(f32)).astype(bf16)
    bh = b.astype(bf16)
    bl = (b - bh.astype(f32)).astype(bf16)
    return (jnp.dot(ah, bh, preferred_element_type=f32) + jnp.dot(ah, bl, preferred_element_type=f32)
            + jnp.dot(al, bh, preferred_element_type=f32))


def _col_to_row(col, eye):
    return jnp.sum(jnp.where(eye, col, 0.0), axis=0, keepdims=True)


def _gdn_prep_body(q_ref, k_ref, v_ref, bg_ref, cq_ref, ck_ref, cv_ref, wq_ref, wk_ref, wv_ref, gp_ref,
                   uin_ref, wcum_ref, qdec_ref, ktail_ref, attn_ref, gtot_ref, pq_ref, pk_ref, pv_ref, *, t_len, nsub):
    hk = pl.program_id(1)
    c = pl.program_id(2)
    cs = GDN_CHUNK
    rep = HV_C // HK_C
    rows = nsub * cs

    @pl.when(c == 0)
    def _():
        pq_ref[...] = cq_ref[0]
        pk_ref[...] = ck_ref[0]
        pv_ref[...] = cv_ref[0]

    def conv_silu(x_ref, prev_ref, w_ref):
        cur = x_ref[...]
        ext = jnp.concatenate([prev_ref[...], cur], axis=0)
        acc = cur * w_ref[CONV_W - 1:CONV_W, :]
        for i in range(CONV_W - 1):
            lo = SUBLANES - (CONV_W - 1) + i
            acc = acc + ext[lo:lo + rows] * w_ref[i:i + 1, :]
        prev_ref[...] = cur[rows - SUBLANES:]
        return acc * _sigmoid(acc)

    def l2n(x):
        return x * lax.rsqrt(jnp.sum(x * x, axis=-1, keepdims=True) + 1e-6)

    q_all = l2n(conv_silu(q_ref, pq_ref, wq_ref)) * (DK_C ** -0.5)
    k_all = l2n(conv_silu(k_ref, pk_ref, wk_ref))
    v_all = conv_silu(v_ref, pv_ref, wv_ref)

    bg = bg_ref[...]
    beta_all = _sigmoid(bg)
    xg = bg + gp_ref[1:2, :]
    g_all = -jnp.exp(gp_ref[0:1, :]) * (jnp.maximum(xg, 0.0) + jnp.log1p(jnp.exp(-jnp.abs(xg))))
    lane = lax.broadcasted_iota(i32, bg.shape, 1)
    if t_len % cs:
        live = (c * rows + lax.broadcasted_iota(i32, (rows, 1), 0)) < t_len
    hsel = []
    for e in range(rep):
        hv = hk * rep + e
        beta = jnp.sum(jnp.where(lane == hv, beta_all, 0.0), axis=1, keepdims=True)
        g = jnp.sum(jnp.where(lane == hv + HV_C, g_all, 0.0), axis=1, keepdims=True)
        if t_len % cs:
            beta = jnp.where(live, beta, 0.0)
            g = jnp.where(live, g, 0.0)
        hsel.append((beta, g))
    ri = lax.broadcasted_iota(i32, (cs, cs), 0)
    ci = lax.broadcasted_iota(i32, (cs, cs), 1)
    eye = ri == ci
    tril = ci <= ri
    strict = ci < ri
    nt = (((1,), (1,)), ((), ()))
    results = []

    chains = []
    for sub in range(nsub):
        rs = slice(sub * cs, (sub + 1) * cs)
        q, k = q_all[rs], k_all[rs]
        kb16 = k.astype(bf16)
        kk = lax.dot_general(kb16, kb16, nt, preferred_element_type=f32)
        qk = lax.dot_general(q.astype(bf16), kb16, nt, preferred_element_type=f32)
        for e in range(rep):
            beta, g = hsel[e][0][rs], hsel[e][1][rs]
            gc = jnp.sum(jnp.where(tril, _col_to_row(g, eye), 0.0), axis=1, keepdims=True)
            gc_row = _col_to_row(gc, eye)
            decay = jnp.where(tril, jnp.exp(jnp.where(tril, gc - gc_row, 0.0)), 0.0)
            pw = jnp.where(strict, -(kk * beta * decay), 0.0)
            chains.append(dict(sub=sub, e=e, q=q, k=k, qk=qk, beta=beta, gc=gc, decay=decay, pw=pw,
                               inv=jnp.where(eye, 1.0, pw)))
    for _ in range(5):
        for ch in chains:
            ch['pw'] = _dot3(ch['pw'], ch['pw'])
        for ch in chains:
            ch['inv'] = ch['inv'] + _dot3(ch['inv'], ch['pw'])
    for ch in chains:
        sub, e, q, k, beta, gc = ch['sub'], ch['e'], ch['q'], ch['k'], ch['beta'], ch['gc']
        e_col = jnp.exp(gc)
        v = v_all[sub * cs:(sub + 1) * cs, e * DV_C:(e + 1) * DV_C]
        rhs = jnp.concatenate([v * beta, k * (beta * e_col)], axis=1)
        sol = _dot3(ch['inv'], rhs)
        gl = jnp.sum(jnp.where(ri[:, 0:1] == cs - 1, gc, 0.0), axis=0, keepdims=True)
        results.append((sub, e, sol[:, :DV_C], sol[:, DV_C:], q * e_col, k * jnp.exp(gl - gc),
                        jnp.where(tril, ch['qk'] * ch['decay'], 0.0), jnp.broadcast_to(jnp.exp(gl), (cs, DV_C))))

    for sub, e, uin, wcum, qdec, ktail, attn, gtot in results:
        rs = slice(sub * cs, (sub + 1) * cs)
        sl = slice(e * DV_C, (e + 1) * DV_C)
        uin_ref[rs, sl] = uin
        wcum_ref[rs, sl] = wcum
        qdec_ref[rs, sl] = qdec
        ktail_ref[rs, sl] = ktail
        attn_ref[rs, e * cs:(e + 1) * cs] = attn
        gtot_ref[rs, sl] = gtot


def _gdn_scan_body(uin_ref, wcum_ref, qdec_ref, ktail_ref, attn_ref, gtot_ref, z_ref, ow_ref, s0_ref,
                   o_ref, sf_ref, s_ref, *, ng):
    c = pl.program_id(2)
    cs = GDN_CHUNK

    @pl.when(c == 0)
    def _():
        s_ref[...] = s0_ref[0]

    tn = (((0,), (0,)), ((), ()))
    sls = [slice(e * DV_C, (e + 1) * DV_C) for e in range(ng)]
    ss = [s_ref[e] for e in range(ng)]
    s16 = [s.astype(bf16) for s in ss]
    us = [uin_ref[:, sls[e]] - jnp.dot(wcum_ref[:, sls[e]].astype(bf16), s16[e], preferred_element_type=f32)
          for e in range(ng)]
    u16 = [u.astype(bf16) for u in us]
    os_ = [jnp.dot(qdec_ref[:, sls[e]].astype(bf16), s16[e], preferred_element_type=f32)
           + jnp.dot(attn_ref[:, e * cs:(e + 1) * cs].astype(bf16), u16[e], preferred_element_type=f32)
           for e in range(ng)]
    states = [ss[e] * jnp.concatenate([gtot_ref[:, sls[e]]] * (DK_C // cs), axis=0)
              + lax.dot_general(ktail_ref[:, sls[e]].astype(bf16), u16[e], tn, preferred_element_type=f32)
              for e in range(ng)]
    outs = []
    for e in range(ng):
        o = os_[e] * lax.rsqrt(jnp.mean(os_[e] * os_[e], axis=-1, keepdims=True) + RMS_EPS) * ow_ref[...]
        z = z_ref[:, sls[e]]
        outs.append(o * (z * _sigmoid(z)))
    for e in range(ng):
        s_ref[e] = states[e]
        o_ref[:, sls[e]] = outs[e]

    @pl.when(c == pl.num_programs(2) - 1)
    def _():
        sf_ref[0] = s_ref[...]


def _gdn_mixer(p, t_len, conv_buf, s0, prm):
    b, tp, wd = p.shape
    cs = GDN_CHUNK
    assert tp % cs == 0 and DK_C == DV_C and DK_C % cs == 0
    n = tp // cs
    rep = HV_C // HK_C
    p2 = p.reshape(b * tp, wd)
    cw = prm['conv_w'].astype(f32)
    cb = jnp.pad(conv_buf.astype(f32), ((0, 0), (SUBLANES - (CONV_W - 1), 0), (0, 0)))
    gp = jnp.zeros((2, LANES), f32).at[0, HV_C:2 * HV_C].set(prm['a_log'].astype(f32))
    gp = gp.at[1, HV_C:2 * HV_C].set(prm['dt_bias'].astype(f32))
    nq = QK_C // DK_C
    vw = rep * DV_C
    nsub = math.gcd(n, 8)
    rows = nsub * cs
    ns = n // nsub
    row = lambda bi, hk, c: bi * ns + c
    hv_out = jax.ShapeDtypeStruct((b * tp, V_C), f32)
    hspec = pl.BlockSpec((rows, vw), lambda bi, hk, c: (row(bi, hk, c), hk))
    uin, wcum, qdec, ktail, attn, gtot = pl.pallas_call(
        functools.partial(_gdn_prep_body, t_len=t_len, nsub=nsub),
        grid=(b, HK_C, ns),
        in_specs=[pl.BlockSpec((rows, DK_C), lambda bi, hk, c: (row(bi, hk, c), hk)),
                  pl.BlockSpec((rows, DK_C), lambda bi, hk, c: (row(bi, hk, c), nq + hk)),
                  pl.BlockSpec((rows, vw), lambda bi, hk, c: (row(bi, hk, c), 2 * QK_C // vw + hk)),
                  pl.BlockSpec((rows, LANES), lambda bi, hk, c: (row(bi, hk, c), (CONV_CH + V_C) // LANES)),
                  pl.BlockSpec((1, SUBLANES, DK_C), lambda bi, hk, c: (bi, 0, hk)),
                  pl.BlockSpec((1, SUBLANES, DK_C), lambda bi, hk, c: (bi, 0, nq + hk)),
                  pl.BlockSpec((1, SUBLANES, vw), lambda bi, hk, c: (bi, 0, 2 * QK_C // vw + hk)),
                  pl.BlockSpec((CONV_W, DK_C), lambda bi, hk, c: (0, hk)),
                  pl.BlockSpec((CONV_W, DK_C), lambda bi, hk, c: (0, nq + hk)),
                  pl.BlockSpec((CONV_W, vw), lambda bi, hk, c: (0, 2 * QK_C // vw + hk)),
                  pl.BlockSpec((2, LANES), lambda bi, hk, c: (0, 0))],
        out_specs=[hspec, hspec, hspec, hspec,
                   pl.BlockSpec((rows, rep * cs), lambda bi, hk, c: (row(bi, hk, c), hk)), hspec],
        out_shape=[hv_out, hv_out, hv_out, hv_out, jax.ShapeDtypeStruct((b * tp, HK_C * rep * cs), f32), hv_out],
        scratch_shapes=[pltpu.VMEM((SUBLANES, DK_C), f32), pltpu.VMEM((SUBLANES, DK_C), f32),
                        pltpu.VMEM((SUBLANES, vw), f32)],
        compiler_params=_cparams("parallel", "parallel", "arbitrary"),
        name="gdn_prep",
    )(p2, p2, p2, p2, cb, cb, cb, cw, cw, cw, gp)

    ng = 16
    gw = ng * DV_C
    gspec = pl.BlockSpec((cs, gw), lambda bi, hg, c: (bi * n + c, hg))
    sspec = pl.BlockSpec((1, ng, DK_C, DV_C), lambda bi, hg, c: (bi, hg, 0, 0))
    o, sf = pl.pallas_call(
        functools.partial(_gdn_scan_body, ng=ng),
        grid=(b, HV_C // ng, n),
        in_specs=[gspec, gspec, gspec, gspec,
                  pl.BlockSpec((cs, ng * cs), lambda bi, hg, c: (bi * n + c, hg)), gspec,
                  pl.BlockSpec((cs, gw), lambda bi, hg, c: (bi * n + c, CONV_CH // gw + hg)),
                  pl.BlockSpec((1, DV_C), lambda bi, hg, c: (0, 0)), sspec],
        out_specs=[gspec, sspec],
        out_shape=[hv_out, jax.ShapeDtypeStruct((b, HV_C, DK_C, DV_C), f32)],
        scratch_shapes=[pltpu.VMEM((ng, DK_C, DV_C), f32)],
        compiler_params=_cparams("parallel", "parallel", "arbitrary"),
        name="gdn_scan",
    )(uin, wcum, qdec, ktail, attn, gtot, p2, prm['onorm_w'].astype(f32).reshape(1, DV_C), s0.astype(f32))
    return o, sf


def _pad_rows(x, rows):
    return jnp.pad(x, ((0, rows - x.shape[0]), (0, 0)))


def kernel(x_prompt, x_sample, cache_k_0, cache_v_0, state_wkv_0, state_shift_0, state_gdn_1, state_conv_1, cache_k_2, cache_v_2, state_wkv_2, state_shift_2, state_gdn_3, state_conv_3, page_table, rel_bias, norm_mix_0, w_in_0, mu_0, w0_0, w2_0, a0_0, a2_0, g2_0, kk_scale_0, ka_mix_0, rk_bonus_0, lnx_w_0, lnx_b_0, w_out_0, norm_ffn_0, peer_wq_0, peer_keys_0, peer_u_0, peer_v_0, norm_mix_1, w_in_1, conv_w_1, a_log_1, dt_bias_1, onorm_w_1, w_out_1, norm_ffn_1, peer_wq_1, peer_keys_1, peer_u_1, peer_v_1, norm_mix_2, w_in_2, mu_2, w0_2, w2_2, a0_2, a2_2, v0_2, v2_2, g2_2, kk_scale_2, ka_mix_2, rk_bonus_2, lnx_w_2, lnx_b_2, w_out_2, norm_ffn_2, peer_wq_2, peer_keys_2, peer_u_2, peer_v_2, norm_mix_3, w_in_3, conv_w_3, a_log_3, dt_bias_3, onorm_w_3, w_out_3, norm_ffn_3, peer_wq_3, peer_keys_3, peer_u_3, peer_v_3, norm_final):
    mix_prm = [
        dict(norm=norm_mix_0, w_in=w_in_0, mu=mu_0, w0=w0_0, w2=w2_0, a0=a0_0, a2=a2_0, g2=g2_0,
             kk_scale=kk_scale_0, ka_mix=ka_mix_0, rk_bonus=rk_bonus_0, lnx_w=lnx_w_0, lnx_b=lnx_b_0,
             w_out=w_out_0),
        dict(norm=norm_mix_1, w_in=w_in_1, conv_w=conv_w_1, a_log=a_log_1, dt_bias=dt_bias_1,
             onorm_w=onorm_w_1, w_out=w_out_1),
        dict(norm=norm_mix_2, w_in=w_in_2, mu=mu_2, w0=w0_2, w2=w2_2, a0=a0_2, a2=a2_2, v0=v0_2, v2=v2_2,
             g2=g2_2, kk_scale=kk_scale_2, ka_mix=ka_mix_2, rk_bonus=rk_bonus_2, lnx_w=lnx_w_2,
             lnx_b=lnx_b_2, w_out=w_out_2),
        dict(norm=norm_mix_3, w_in=w_in_3, conv_w=conv_w_3, a_log=a_log_3, dt_bias=dt_bias_3,
             onorm_w=onorm_w_3, w_out=w_out_3),
    ]
    ffn_prm = [
        (norm_ffn_0, peer_wq_0, peer_keys_0, peer_u_0, peer_v_0),
        (norm_ffn_1, peer_wq_1, peer_keys_1, peer_u_1, peer_v_1),
        (norm_ffn_2, peer_wq_2, peer_keys_2, peer_u_2, peer_v_2),
        (norm_ffn_3, peer_wq_3, peer_keys_3, peer_u_3, peer_v_3),
    ]
    layer_state = [
        (cache_k_0, cache_v_0, state_wkv_0, state_shift_0),
        (state_gdn_1, state_conv_1),
        (cache_k_2, cache_v_2, state_wkv_2, state_shift_2),
        (state_gdn_3, state_conv_3),
    ]
    bp, sp, d = x_prompt.shape
    bs, ts, _ = x_sample.shape
    ns = bs * ts
    ns_mm = -(-ns // 16) * 16
    ns_peer = -(-ns // LANES) * LANES
    xp = x_prompt.reshape(bp * sp, d)
    xs = x_sample.reshape(ns, d)
    depth = len(mix_prm)
    vf_p = vf_s = None
    new = []
    for i in range(depth):
        prm = dict(mix_prm[i])
        prm['w_in'] = prm['w_in'].astype(bf16)
        prm['w_out'] = prm['w_out'].astype(bf16)
        xs_pad = _pad_rows(xs, ns_mm)
        pp = _matmul(xp, prm['w_in'], gain=prm['norm'])
        ps = _matmul(xs_pad, prm['w_in'], gain=prm['norm'])[:ns]
        if i % 2 == 0:
            ck, cv, wkv_in, shift_in = layer_state[i]
            first = i == 0
            rw = prm['w_in'].shape[1] - 3 * C_A
            o_a = _moba_prompt(pp, rel_bias, bp, sp)
            p_rw = pp[:, 3 * C_A:].reshape(bp, sp, rw)
            p_prev = jnp.concatenate([jnp.zeros((bp, 1, rw), f32), p_rw[:, :-1]], axis=1)
            o_b, wkv_p, vf_p = _rwkv7(p_rw, p_prev, jnp.zeros((bp, H_B, DH_B, DH_B), f32), vf_p, prm, first)
            cat = jnp.concatenate([o_a, o_b.reshape(bp * sp, C_B)], axis=-1)
            xp_new = _matmul(cat, prm['w_out'], res=xp)
            kp = pp[:, C_A:2 * C_A].reshape(bp, sp // PAGE_SIZE, PAGE_SIZE, H_A, DH_A)
            vp = pp[:, 2 * C_A:3 * C_A].reshape(bp, sp // PAGE_SIZE, PAGE_SIZE, H_A, DH_A)
            hp_last = _rms(xp.reshape(bp, sp, d)[:, -1], prm['norm'])
            qs = ps[:, :C_A].reshape(bs, ts, H_A, DH_A)
            ks = ps[:, C_A:2 * C_A].reshape(bs, ts, H_A, DH_A)
            vs = ps[:, 2 * C_A:3 * C_A].reshape(bs, ts, H_A, DH_A)
            o_as = _moba_sample(qs, ks, vs, ck, cv, page_table, rel_bias).reshape(ns, C_A)
            ps_rw = ps[:, 3 * C_A:].reshape(bs, ts, rw)
            prev0 = _matmul(_pad_rows(shift_in, ns_mm), prm['w_in'], col0=3 * C_A)[:bs]
            ps_prev = jnp.concatenate([prev0[:, None], ps_rw[:, :-1]], axis=1)
            o_bs, wkv_s, vf_s = _rwkv7(ps_rw, ps_prev, wkv_in, vf_s, prm, first)
            cat_s = jnp.concatenate([o_as, o_bs.reshape(ns, C_B)], axis=-1)
            xs_new = _matmul(_pad_rows(cat_s, ns_mm), prm['w_out'], res=xs_pad)[:ns]
            hs_last = _rms(xs.reshape(bs, ts, d)[:, -1], prm['norm'])
            new.append((kp, vp, ks, vs, wkv_p, wkv_s, hp_last, hs_last))
        else:
            gdn_in, conv_in = layer_state[i]
            wd = prm['w_in'].shape[1]

            def gdn(p3, conv_buf, s0):
                bb, tt, _ = p3.shape
                tpad = -(-tt // GDN_CHUNK) * GDN_CHUNK
                o, sf = _gdn_mixer(jnp.pad(p3, ((0, 0), (0, tpad - tt), (0, 0))), tt, conv_buf, s0, prm)
                conv = jnp.concatenate([conv_buf.astype(f32), p3[..., :CONV_CH]], axis=1)[:, -(CONV_W - 1):]
                return o.reshape(bb, tpad, V_C)[:, :tt].reshape(bb * tt, V_C), conv, sf

            o_p, conv_p, gdn_p = gdn(pp.reshape(bp, sp, wd), jnp.zeros((bp, CONV_W - 1, CONV_CH), f32),
                                     jnp.zeros((bp, HV_C, DK_C, DV_C), f32))
            o_s, conv_s, gdn_s = gdn(ps.reshape(bs, ts, wd), conv_in, gdn_in)
            xp_new = _matmul(o_p, prm['w_out'], res=xp)
            xs_new = _matmul(_pad_rows(o_s, ns_mm), prm['w_out'], res=xs_pad)[:ns]
            new.append((gdn_p, gdn_s, conv_p, conv_s))
        xp, xs = xp_new, xs_new
        g_ffn, wq, sub_keys, u_tab, v_tab = ffn_prm[i]
        wq_bf, u_bf, v_bf = wq.astype(bf16), u_tab, v_tab
        keys = sub_keys.astype(f32)
        xp = _peer(xp, g_ffn, wq_bf, keys, u_bf, v_bf)
        xs = _peer(_pad_rows(xs, ns_peer), g_ffn, wq_bf, keys, u_bf, v_bf)[:ns]
    y_prompt = _rms(xp, norm_final).reshape(bp, sp, d)
    y_sample = _rms(xs, norm_final).reshape(bs, ts, d)
    out = [y_prompt, y_sample]
    for layer in new:
        out.extend(layer)
    return tuple(out)
```

```python
import functools
import math

import jax
import jax.numpy as jnp
from jax import lax
from jax.experimental import pallas as pl
from jax.experimental.pallas import tpu as pltpu

f32 = jnp.float32
bf16 = jnp.bfloat16
i32 = jnp.int32

RMS_EPS = 1e-6
NEG_INF = -1e30

PAGE_SIZE = 128
H_A, DH_A = 8, 128
C_A = H_A * DH_A
MOBA_BLOCK, MOBA_TOPK = 256, 3
NUM_BUCKETS, MAX_DISTANCE = 32, 128
H_B, DH_B = 16, 64
C_B = H_B * DH_B
LORA_W, LORA_A, LORA_G, LORA_V = 64, 64, 160, 32
LNX_EPS = 64e-5
HK_C, HV_C, DK_C, DV_C = 16, 32, 128, 128
QK_C, V_C = HK_C * DK_C, HV_C * DV_C
CONV_W = 4
CONV_CH = 2 * QK_C + V_C
GDN_CHUNK = 64
PEER_HEADS, N_KEYS, PEER_DK, PEER_TOPK = 8, 128, 256, 16
N_EXPERTS = N_KEYS * N_KEYS

LANES = 128
SUBLANES = 8
VMEM_LIMIT = 56 * 1024 * 1024


def _cparams(*sem):
    return pltpu.CompilerParams(dimension_semantics=sem, vmem_limit_bytes=VMEM_LIMIT)


def _rms(x, g):
    return x * lax.rsqrt(jnp.mean(x * x, axis=-1, keepdims=True) + RMS_EPS) * g


def _mm_body(*refs, norm, has_res):
    if has_res:
        x_ref, g_ref, w_ref, res_ref, o_ref, h_ref = refs
    else:
        x_ref, g_ref, w_ref, o_ref, h_ref = refs

    @pl.when(pl.program_id(1) == 0)
    def _():
        x = x_ref[...]
        if norm:
            x = _rms(x, g_ref[...])
        h_ref[...] = x.astype(bf16)

    acc = jnp.dot(h_ref[...], w_ref[...], preferred_element_type=f32)
    if has_res:
        acc = res_ref[...] + acc
    o_ref[...] = acc


def _matmul(x, w, gain=None, res=None, col0=0, ncols=None, tn=512):
    m, k = x.shape
    ncols = w.shape[1] - col0 if ncols is None else ncols
    assert col0 % tn == 0 and m % 16 == 0 and w.dtype == bf16
    tm = min(m, 1024 if k <= 2048 else 512)
    assert m % tm == 0
    nj = pl.cdiv(ncols, tn)
    j0 = col0 // tn
    norm = gain is not None
    g = (gain if norm else jnp.ones((k,), f32)).reshape(1, k)
    in_specs = [pl.BlockSpec((tm, k), lambda i, j: (i, 0)),
                pl.BlockSpec((1, k), lambda i, j: (0, 0)),
                pl.BlockSpec((k, tn), lambda i, j: (0, j + j0))]
    args = [x, g, w]
    if res is not None:
        in_specs.append(pl.BlockSpec((tm, tn), lambda i, j: (i, j)))
        args.append(res)
    return pl.pallas_call(
        functools.partial(_mm_body, norm=norm, has_res=res is not None),
        grid=(m // tm, nj),
        in_specs=in_specs,
        out_specs=pl.BlockSpec((tm, tn), lambda i, j: (i, j)),
        out_shape=jax.ShapeDtypeStruct((m, ncols), f32),
        scratch_shapes=[pltpu.VMEM((tm, k), bf16)],
        compiler_params=_cparams("parallel", "arbitrary"),
        name="matmul",
    )(*args)


def _t5_bucket(dist):
    n = jnp.maximum(dist, 0)
    max_exact = NUM_BUCKETS // 2
    ratio = jnp.log(jnp.maximum(n, 1).astype(f32) / max_exact) / math.log(MAX_DISTANCE / max_exact)
    large = max_exact + (ratio * (NUM_BUCKETS - max_exact)).astype(i32)
    return jnp.where(n < max_exact, n, jnp.minimum(large, NUM_BUCKETS - 1))


def _moba_prompt_body(q_ref, k_ref, v_ref, bias_ref, o_ref, kmh_ref, kml_ref, kb_ref, vb_ref, *, nb, hp):
    qb = pl.program_id(2)
    blk = MOBA_BLOCK
    scale = DH_A ** -0.5
    nt = (((1,), (1,)), ((), ()))
    hs = [slice(e * DH_A, (e + 1) * DH_A) for e in range(hp)]
    nbr = -(-nb // SUBLANES) * SUBLANES

    @pl.when(qb == 0)
    def _():
        k = k_ref[...]
        kmh_ref[...] = jnp.zeros_like(kmh_ref)
        kml_ref[...] = jnp.zeros_like(kml_ref)
        for e in range(hp):
            km = jnp.mean(k[:, hs[e]].reshape(nb, blk, DH_A), axis=1)
            hi = km.astype(bf16)
            kmh_ref[e, 0:nb, :] = hi
            kml_ref[e, 0:nb, :] = (km - hi.astype(f32)).astype(bf16)
        kb_ref[...] = k.astype(bf16)
        vb_ref[...] = v_ref[...].astype(bf16)

    q = q_ref[...]
    q16 = [q[:, hs[e]].astype(bf16) for e in range(hp)]
    rowj = lax.broadcasted_iota(i32, (nbr, blk), 0)
    past = rowj < qb
    lhs = []
    for e in range(hp):
        qlo = (q[:, hs[e]] - q16[e].astype(f32)).astype(bf16)
        gate = (lax.dot_general(kmh_ref[e], q16[e], nt, preferred_element_type=f32)
                + lax.dot_general(kmh_ref[e], qlo, nt, preferred_element_type=f32)
                + lax.dot_general(kml_ref[e], q16[e], nt, preferred_element_type=f32))[0:nbr]
        g = jnp.where(past, gate, NEG_INF)
        sel = jnp.zeros(g.shape, f32)
        for _ in range(MOBA_TOPK):
            m = jnp.max(g, axis=0, keepdims=True)
            idx = jnp.min(jnp.where(g == m, rowj, nbr), axis=0, keepdims=True)
            pick = rowj == idx
            sel = jnp.where(pick, jnp.where(past, 1.0, 0.0), sel)
            g = jnp.where(pick, -jnp.inf, g)
        sel = jnp.where(rowj == qb, 1.0, sel)
        selneg = jnp.where(sel > 0.0, 0.0, NEG_INF / scale)
        selneg = jnp.concatenate([selneg, jnp.zeros((LANES - nbr, blk), f32)], axis=0).T
        lhs.append(jnp.concatenate([q16[e], selneg.astype(bf16)], axis=1))

    rowi = qb * blk + lax.broadcasted_iota(i32, (blk, blk), 0)
    coli = lax.broadcasted_iota(i32, (blk, blk), 1)
    lane = lax.broadcasted_iota(i32, (blk, LANES), 1)

    def tile(j, carry):
        ks = pl.multiple_of(j * blk, blk)
        boff = pl.multiple_of((nb - 1 - qb + j) * blk, blk)
        causal = coli + j * blk <= rowi
        onehot = jnp.where(lane == j, 1.0, 0.0).astype(bf16)
        ss = [lax.dot_general(lhs[e], jnp.concatenate([kb_ref[pl.ds(ks, blk), hs[e]], onehot], axis=1), nt,
                              preferred_element_type=f32) * scale + bias_ref[e, :, pl.ds(boff, blk)]
              for e in range(hp)]
        ss = [jnp.where(causal, s, NEG_INF) for s in ss]
        ms = [jnp.maximum(carry[e][0], jnp.max(ss[e], axis=1, keepdims=True)) for e in range(hp)]
        ps = [jnp.exp(ss[e] - ms[e]) for e in range(hp)]
        als = [jnp.exp(carry[e][0] - ms[e]) for e in range(hp)]
        return tuple((ms[e], als[e] * carry[e][1] + jnp.sum(ps[e], axis=1, keepdims=True),
                      als[e] * carry[e][2] + jnp.dot(ps[e].astype(bf16), vb_ref[pl.ds(ks, blk), hs[e]],
                                                     preferred_element_type=f32)) for e in range(hp))

    init = tuple((jnp.full((blk, 1), -jnp.inf, f32), jnp.zeros((blk, 1), f32), jnp.zeros((blk, DH_A), f32))
                 for _ in range(hp))
    res = lax.fori_loop(0, qb + 1, tile, init)
    o_ref[...] = jnp.concatenate([acc / l for _, l, acc in res], axis=1)


def _bias_strip_body(f_ref, o_ref, *, width):
    row = jnp.broadcast_to(f_ref[0], (MOBA_BLOCK, f_ref.shape[2]))
    o_ref[0] = pltpu.roll(row, 0, 1, stride=1, stride_axis=0)[:, :width]


def _bias_strip(rel_bias, nb):
    blk = MOBA_BLOCK
    width = (2 * nb - 1) * blk
    padded = width + blk
    x = jnp.arange(padded)
    dist = jnp.where(x < width, (nb - 1) * blk - x, (nb - 1) * blk + padded - x)
    prof = rel_bias.astype(f32)[_t5_bucket(dist)].T.reshape(H_A, 1, padded)
    return pl.pallas_call(
        functools.partial(_bias_strip_body, width=width),
        grid=(H_A,),
        in_specs=[pl.BlockSpec((1, 1, padded), lambda h: (h, 0, 0))],
        out_specs=pl.BlockSpec((1, blk, width), lambda h: (h, 0, 0)),
        out_shape=jax.ShapeDtypeStruct((H_A, blk, width), f32),
        compiler_params=_cparams("parallel"),
        name="moba_bias_strip",
    )(prof)


def _moba_prompt(p, rel_bias, b, s_len):
    blk = MOBA_BLOCK
    assert s_len % blk == 0
    nb = s_len // blk
    strip = _bias_strip(rel_bias, nb)
    hp = 2
    hw = hp * DH_A
    ng = C_A // hw
    return pl.pallas_call(
        functools.partial(_moba_prompt_body, nb=nb, hp=hp),
        grid=(b, ng, nb),
        in_specs=[pl.BlockSpec((blk, hw), lambda bi, h, qb: (bi * nb + qb, h)),
                  pl.BlockSpec((s_len, hw), lambda bi, h, qb: (bi, ng + h)),
                  pl.BlockSpec((s_len, hw), lambda bi, h, qb: (bi, 2 * ng + h)),
                  pl.BlockSpec((hp, blk, (2 * nb - 1) * blk), lambda bi, h, qb: (h, 0, 0))],
        out_specs=pl.BlockSpec((blk, hw), lambda bi, h, qb: (bi * nb + qb, h)),
        out_shape=jax.ShapeDtypeStruct((b * s_len, C_A), f32),
        scratch_shapes=[pltpu.VMEM((hp, LANES, DH_A), bf16), pltpu.VMEM((hp, LANES, DH_A), bf16),
                        pltpu.VMEM((s_len, hw), bf16), pltpu.VMEM((s_len, hw), bf16)],
        compiler_params=_cparams("parallel", "parallel", "arbitrary"),
        name="moba_prompt",
    )(p, p, p, strip)


def _tree_sum(xs):
    xs = list(xs)
    while len(xs) > 1:
        xs = [xs[i] + xs[i + 1] for i in range(0, len(xs) - 1, 2)] + ([xs[-1]] if len(xs) % 2 else [])
    return xs[0]


def _wkv_body(r_ref, w_ref, k_ref, kk_ref, b_ref, v_ref, s0_ref, y_ref, sf_ref, s_ref, *, tc):
    @pl.when(pl.program_id(1) == 0)
    def _():
        s_ref[...] = s0_ref[0]

    half = LANES // 2
    ni, hn = s_ref.shape[0], s_ref.shape[1]
    pair = 4

    def step(t, c):
        def rows(ref):
            return [ref[0, t, pl.ds(j, 1), :] for j in range(hn)]

        r, w, k, kk, b = rows(r_ref), rows(w_ref), rows(k_ref), rows(kk_ref), rows(b_ref)
        for i0 in range(0, ni, pair):
            its = range(i0, i0 + pair)
            vs = [v_ref[0, t, pl.ds(it * SUBLANES, SUBLANES), :] for it in its]
            ss = [[s_ref[it, j] for j in range(hn)] for it in its]
            t1 = [_tree_sum(s[j] * kk[j] for j in range(hn)) for s in ss]
            sa = [-(x + pltpu.roll(x, half, 1)) for x in t1]
            ss = [[s[j] * w[j] + sa_ * b[j] + v_ * k[j] for j in range(hn)] for s, sa_, v_ in zip(ss, sa, vs)]
            t2 = [_tree_sum(s[j] * r[j] for j in range(hn)) for s in ss]
            ys = [x + pltpu.roll(x, half, 1) for x in t2]
            for it, s, y in zip(its, ss, ys):
                for j in range(hn):
                    s_ref[it, j] = s[j]
                y_ref[0, t, pl.ds(it * SUBLANES, SUBLANES), :] = y
        return c

    lax.fori_loop(0, tc, step, 0)

    @pl.when(pl.program_id(1) == pl.num_programs(1) - 1)
    def _():
        sf_ref[0] = s_ref[...]


def _wkv_scan(r, w, k, kk, b, v, s0):
    bsz, t, h, n = r.shape
    assert n == DH_B and (bsz * h) % (LANES // 2) == 0 and n % SUBLANES == 0
    grp = bsz * h // (LANES // 2)
    bg = bsz // grp
    hn = n // 2
    ni = n // SUBLANES

    def vec(z):
        z = z.reshape(grp, bg, t, h, 2, hn)
        return z.transpose(0, 2, 5, 4, 1, 3).reshape(grp, t, hn, LANES)

    def row(z):
        z = z.reshape(grp, bg, t, h, n).transpose(0, 2, 4, 1, 3).reshape(grp, t, n, LANES // 2)
        return jnp.concatenate([z, z], axis=-1)

    s0k = s0.astype(f32).reshape(grp, bg, h, ni, SUBLANES, 2, hn).transpose(0, 3, 6, 4, 5, 1, 2)
    s0k = s0k.reshape(grp, ni, hn, SUBLANES, LANES)
    tc = math.gcd(t, 64)
    vspec = pl.BlockSpec((1, tc, hn, LANES), lambda g, c: (g, c, 0, 0))
    rspec = pl.BlockSpec((1, tc, n, LANES), lambda g, c: (g, c, 0, 0))
    sspec = pl.BlockSpec((1, ni, hn, SUBLANES, LANES), lambda g, c: (g, 0, 0, 0, 0))
    y, sf = pl.pallas_call(
        functools.partial(_wkv_body, tc=tc),
        grid=(grp, t // tc),
        in_specs=[vspec] * 5 + [rspec, sspec],
        out_specs=[rspec, sspec],
        out_shape=[jax.ShapeDtypeStruct((grp, t, n, LANES), f32),
                   jax.ShapeDtypeStruct((grp, ni, hn, SUBLANES, LANES), f32)],
        scratch_shapes=[pltpu.VMEM((ni, hn, SUBLANES, LANES), f32)],
        compiler_params=_cparams("parallel", "arbitrary"),
        name="wkv_scan",
    )(vec(r), vec(w), vec(k), vec(kk), vec(b), row(v), s0k)
    y = y[..., :LANES // 2].reshape(grp, t, n, bg, h).transpose(0, 3, 1, 4, 2).reshape(bsz, t, h, n)
    sf = sf.reshape(grp, ni, hn, SUBLANES, 2, bg, h).transpose(0, 5, 6, 1, 3, 4, 2).reshape(bsz, h, n, n)
    return y, sf


def _top_rows(vs, k, payloads=None):
    rows = lax.broadcasted_iota(i32, vs[0].shape, 0)
    n = vs[0].shape[0]
    vs = list(vs)
    vals = [[] for _ in vs]
    outs = [[] for _ in vs]
    for _ in range(k):
        ms = [jnp.max(v, axis=0, keepdims=True) for v in vs]
        ids = [jnp.min(jnp.where(v == m, rows, n), axis=0, keepdims=True) for v, m in zip(vs, ms)]
        picks = [rows == i for i in ids]
        vs = [jnp.where(p, -jnp.inf, v) for p, v in zip(picks, vs)]
        for j in range(len(vs)):
            vals[j].append(ms[j])
            outs[j].append(ids[j].astype(f32) if payloads is None else
                           jnp.sum(jnp.where(picks[j], payloads[j], 0.0), axis=0, keepdims=True))
    return [(jnp.concatenate(v, axis=0), jnp.concatenate(o, axis=0)) for v, o in zip(vals, outs)]


_PAIR_CNT = [PEER_TOPK // (m + 1) for m in range(PEER_TOPK)]
_PAIR_PAD = -sum(_PAIR_CNT) % SUBLANES


def _route_body(x_ref, g_ref, wq_ref, keys_ref, xn_ref, i1_ref, i2_ref, gate_ref, q_scr, e_scr, w_scr, *, tt):
    xn = _rms(x_ref[...], g_ref[...]).astype(bf16)
    xn_ref[...] = xn
    q_scr[...] = jnp.dot(xn, wq_ref[...], preferred_element_type=f32)
    hw = PEER_DK // 2
    nt = (((1,), (1,)), ((), ()))

    def head(h, c):
        row = pl.multiple_of(h * PEER_TOPK, PEER_TOPK)
        nsub = tt // LANES
        sts = []
        for sub in range(nsub):
            for p in range(2):
                col = pl.multiple_of(h * PEER_DK + p * hw, hw)
                qp = q_scr[sub * LANES:(sub + 1) * LANES, pl.ds(col, hw)]
                sts.append(lax.dot_general(keys_ref[p], qp, nt, preferred_element_type=f32))
        tops = _top_rows(sts, PEER_TOPK)
        cands, ceids = [], []
        for sub in range(nsub):
            (sv1, si1), (sv2, si2) = tops[2 * sub], tops[2 * sub + 1]
            si1 = si1 * float(N_KEYS)
            cands.append(jnp.concatenate([sv1[m:m + 1] + sv2[:_PAIR_CNT[m]] for m in range(PEER_TOPK)]
                                         + [jnp.full((_PAIR_PAD, LANES), -jnp.inf, f32)], axis=0))
            ceids.append(jnp.concatenate([si1[m:m + 1] + si2[:_PAIR_CNT[m]] for m in range(PEER_TOPK)]
                                         + [jnp.zeros((_PAIR_PAD, LANES), f32)], axis=0))
        joint = _top_rows(cands, PEER_TOPK, payloads=ceids)
        for sub in range(nsub):
            cv, eid = joint[sub]
            ex = jnp.exp(cv - cv[0:1])
            e_scr[pl.ds(row, PEER_TOPK), sub * LANES:(sub + 1) * LANES] = eid
            w_scr[pl.ds(row, PEER_TOPK), sub * LANES:(sub + 1) * LANES] = ex / jnp.sum(ex, axis=0, keepdims=True)
        return c

    lax.fori_loop(0, PEER_HEADS, head, 0)
    eid = e_scr[...].T
    i1 = jnp.floor(eid * (1.0 / N_KEYS))
    i1_ref[...] = i1
    i2_ref[...] = eid - i1 * float(N_KEYS)
    gate_ref[...] = w_scr[...].T


def _peer_route(x, gain, wq_bf, keys):
    t, d = x.shape
    tt = min(t, 256)
    assert t % tt == 0 and tt % LANES == 0
    nsel = PEER_HEADS * PEER_TOPK
    sel = jax.ShapeDtypeStruct((t, nsel), f32)
    sspec = pl.BlockSpec((tt, nsel), lambda i: (i, 0))
    return pl.pallas_call(
        functools.partial(_route_body, tt=tt),
        grid=(t // tt,),
        in_specs=[pl.BlockSpec((tt, d), lambda i: (i, 0)),
                  pl.BlockSpec((1, d), lambda i: (0, 0)),
                  pl.BlockSpec(wq_bf.shape, lambda i: (0, 0)),
                  pl.BlockSpec(keys.shape, lambda i: (0, 0, 0))],
        out_specs=[pl.BlockSpec((tt, d), lambda i: (i, 0)), sspec, sspec, sspec],
        out_shape=[jax.ShapeDtypeStruct((t, d), bf16), sel, sel, sel],
        scratch_shapes=[pltpu.VMEM((tt, wq_bf.shape[1]), f32), pltpu.VMEM((nsel, tt), f32),
                        pltpu.VMEM((nsel, tt), f32)],
        compiler_params=_cparams("parallel"),
        name="peer_route",
    )(x, gain.reshape(1, d), wq_bf, keys)


def _gate_build_body(i1_ref, i2_ref, g_ref, o_ref, *, tb):
    keyrow = lax.broadcasted_iota(i32, (N_KEYS, i1_ref.shape[1]), 0).astype(f32).astype(bf16)
    one, zero = jnp.ones((), bf16), jnp.zeros((), bf16)
    nt = (((1,), (1,)), ((), ()))

    ngrp = 2

    def toks(i, c):
        t0 = pl.multiple_of(i * (ngrp * SUBLANES), ngrp * SUBLANES)
        i1 = i1_ref[pl.ds(t0, ngrp * SUBLANES), :]
        i2 = i2_ref[pl.ds(t0, ngrp * SUBLANES), :]
        g = g_ref[pl.ds(t0, ngrp * SUBLANES), :]
        ghi = g.astype(bf16).astype(f32)
        glo = g - ghi
        ops = []
        for r in range(ngrp * SUBLANES):
            a = jnp.where(keyrow == i1[r:r + 1].astype(bf16), one, zero)
            m2 = keyrow == i2[r:r + 1].astype(bf16)
            bhi = jnp.where(m2, ghi[r:r + 1].astype(bf16), zero)
            blo = jnp.where(m2, glo[r:r + 1].astype(bf16), zero)
            ops.append((jnp.concatenate([a, a], axis=1), jnp.concatenate([bhi, blo], axis=1)))
        gs = [lax.dot_general(a2, b2, nt, preferred_element_type=f32) for a2, b2 in ops]
        for q in range(ngrp):
            o_ref[i * ngrp + q] = jnp.swapaxes(jnp.stack(gs[q * SUBLANES:(q + 1) * SUBLANES], axis=0), 0, 1)
        return c

    lax.fori_loop(0, tb // (ngrp * SUBLANES), toks, 0)


def _gate_build(i1, i2, gate):
    t, nsel = i1.shape
    tb = min(t, 128)
    assert t % tb == 0 and tb % (2 * SUBLANES) == 0
    spec = pl.BlockSpec((tb, nsel), lambda i: (i, 0))
    return pl.pallas_call(
        functools.partial(_gate_build_body, tb=tb),
        grid=(t // tb,),
        in_specs=[spec, spec, spec],
        out_specs=pl.BlockSpec((tb // SUBLANES, N_KEYS, SUBLANES, N_KEYS), lambda i: (i, 0, 0, 0)),
        out_shape=jax.ShapeDtypeStruct((t // SUBLANES, N_KEYS, SUBLANES, N_KEYS), f32),
        compiler_params=_cparams("parallel"),
        name="peer_gate_build",
    )(i1, i2, gate)


def _peer_mix_body(x_ref, g_ref, u_ref, v_ref, o_ref, *, na, tt):
    j = pl.program_id(1)

    @pl.when(j == 0)
    def _():
        o_ref[...] = jnp.zeros_like(o_ref)

    h = lax.dot_general(x_ref[...], u_ref[...].astype(bf16), (((1,), (1,)), ((), ())),
                        preferred_element_type=f32)
    parts = []
    for a in range(na):
        ha = h[:, a * N_KEYS:(a + 1) * N_KEYS]
        act = 0.5 * ha * (1.0 + lax.erf(ha * (2.0 ** -0.5)))
        parts.append((g_ref[:, a].reshape(tt, N_KEYS) * act).astype(bf16))
    p = jnp.concatenate(parts, axis=1)
    o_ref[...] += jnp.dot(p, v_ref[...].astype(bf16), preferred_element_type=f32)


def _peer_mix(xn, g4, u_bf, v_bf):
    t, d = xn.shape
    tt = min(t, 1024)
    na = 4
    et = na * N_KEYS
    assert t % tt == 0 and tt % SUBLANES == 0
    return pl.pallas_call(
        functools.partial(_peer_mix_body, na=na, tt=tt),
        grid=(t // tt, N_EXPERTS // et),
        in_specs=[pl.BlockSpec((tt, d), lambda i, j: (i, 0)),
                  pl.BlockSpec((tt // SUBLANES, na, SUBLANES, N_KEYS), lambda i, j: (i, j, 0, 0)),
                  pl.BlockSpec((et, d), lambda i, j: (j, 0)),
                  pl.BlockSpec((et, d), lambda i, j: (j, 0))],
        out_specs=pl.BlockSpec((tt, d), lambda i, j: (i, 0)),
        out_shape=jax.ShapeDtypeStruct((t, d), f32),
        compiler_params=_cparams("parallel", "arbitrary"),
        name="peer_mix",
    )(xn, g4, u_bf, v_bf)


def _peer(x, gain, wq_bf, keys, u_bf, v_bf):
    xn, i1, i2, gate = _peer_route(x, gain, wq_bf, keys)
    return x + _peer_mix(xn, _gate_build(i1, i2, gate), u_bf, v_bf)


def _l2norm(x, eps):
    return x * lax.rsqrt(jnp.sum(x * x, axis=-1, keepdims=True) + eps)


def _rwkv7(p_cur, p_prev, s0, v_first, prm, first):
    b, t, _ = p_cur.shape
    m = (p_cur + (p_prev - p_cur) * prm['mu']).astype(f32)
    r, k, v = m[..., :C_B], m[..., C_B:2 * C_B], m[..., 2 * C_B:3 * C_B]
    o = 3 * C_B
    wl = m[..., o:o + LORA_W]
    o += LORA_W
    al = m[..., o:o + LORA_A]
    o += LORA_A
    gl = m[..., o:o + LORA_G]
    o += LORA_G
    w_log = -jax.nn.softplus(-(prm['w0'] + jnp.tanh(wl) @ prm['w2'])) - 0.5
    decay = jnp.exp(-jnp.exp(w_log.astype(f32)))
    a = jax.nn.sigmoid((prm['a0'] + al @ prm['a2']).astype(f32))
    g = (jax.nn.sigmoid(gl) @ prm['g2']).astype(f32)
    if first:
        v_first = v
    else:
        vl = m[..., o:o + LORA_V]
        v = v + (v_first - v) * jax.nn.sigmoid((prm['v0'] + vl @ prm['v2']).astype(f32))

    def hd(z):
        return z.reshape(b, t, H_B, DH_B)

    kk = _l2norm(hd(k * prm['kk_scale']), 1e-24)
    k = k * (1.0 + (a - 1.0) * prm['ka_mix'])
    r_h, k_h, v_h = hd(r), hd(k), hd(v)
    b_h = kk * hd(a)
    y, s_fin = _wkv_scan(r_h, hd(decay), k_h, kk, b_h, v_h, s0)
    mu = jnp.mean(y, axis=-1, keepdims=True)
    var = jnp.mean(jnp.square(y - mu), axis=-1, keepdims=True)
    y = ((y - mu) * lax.rsqrt(var + LNX_EPS)).reshape(b, t, C_B) * prm['lnx_w'] + prm['lnx_b']
    bonus = jnp.sum(r_h * k_h * prm['rk_bonus'], axis=-1, keepdims=True) * v_h
    y = (y + bonus.reshape(b, t, C_B)) * g
    return y, s_fin.astype(s0.dtype), v_first


def _select_blocks(q, kmean, qblk):
    n_sel = min(MOBA_TOPK, kmean.shape[2])
    gate = jnp.einsum('bhqd,bhjd->bhqj', q.astype(f32), kmean, precision=lax.Precision.HIGHEST)
    past = jnp.arange(kmean.shape[2])[None, :] < qblk[:, None]
    _, idx = lax.top_k(jnp.where(past, gate, NEG_INF), n_sel)
    return idx, idx < qblk[:, None]


def _kmean_body(pt_ref, *refs, ppb):
    o_ref = refs[-1]
    pages = refs[:-1]
    for g in range(len(pages) // ppb):
        acc = jnp.sum(pages[g * ppb][0], axis=0)
        for r in range(1, ppb):
            acc = acc + jnp.sum(pages[g * ppb + r][0], axis=0)
        o_ref[0, g] = acc * (1.0 / MOBA_BLOCK)


def _paged_block_means(cache, page_table):
    bd, n_pages = page_table.shape
    ppb = MOBA_BLOCK // PAGE_SIZE
    nbp = n_pages // ppb
    bps = math.gcd(nbp, 4)
    hd = cache.shape[2:]
    specs = [pl.BlockSpec((1, PAGE_SIZE) + hd,
                          functools.partial(lambda b, j, pt, r: (pt[b, j * (bps * ppb) + r], 0, 0, 0), r=r))
             for r in range(bps * ppb)]
    return pl.pallas_call(
        functools.partial(_kmean_body, ppb=ppb),
        grid_spec=pltpu.PrefetchScalarGridSpec(
            num_scalar_prefetch=1, grid=(bd, nbp // bps), in_specs=specs,
            out_specs=pl.BlockSpec((1, bps) + hd, lambda b, j, pt: (b, j, 0, 0))),
        out_shape=jax.ShapeDtypeStruct((bd, nbp) + hd, f32),
        compiler_params=_cparams("parallel", "arbitrary"),
        name="moba_block_means",
    )(page_table, *([cache] * (bps * ppb)))


def _moba_decode_body(pp_ref, lp_ref, ok_ref, q_ref, kn_ref, vn_ref, ob_ref, *refs, ppb):
    k_refs, v_refs, b_refs = refs[:ppb], refs[ppb:2 * ppb], refs[2 * ppb:3 * ppb]
    o_ref, m_ref, l_ref, acc_ref = refs[3 * ppb:]
    b, h, s = pl.program_id(0), pl.program_id(1), pl.program_id(2)
    scale = DH_A ** -0.5
    q = q_ref[0]

    @pl.when(s == 0)
    def _():
        m_ref[...] = jnp.sum(q * kn_ref[0], axis=1, keepdims=True) * scale + ob_ref[0][:, 0:1]
        l_ref[...] = jnp.ones_like(l_ref)
        acc_ref[...] = vn_ref[0]

    def update(kp, vp, bias):
        sc = lax.dot_general(q.astype(bf16), kp.astype(bf16), (((1,), (1,)), ((), ())),
                             preferred_element_type=f32) * scale + bias
        sc = jnp.where(ok_ref[b, h, s] != 0, sc, NEG_INF)
        m_old = m_ref[...]
        m_new = jnp.maximum(m_old, jnp.max(sc, axis=1, keepdims=True))
        alpha = jnp.exp(m_old - m_new)
        p = jnp.exp(sc - m_new)
        m_ref[...] = m_new
        l_ref[...] = alpha * l_ref[...] + jnp.sum(p, axis=1, keepdims=True)
        acc_ref[...] = alpha * acc_ref[...] + jnp.dot(p.astype(bf16), vp.astype(bf16), preferred_element_type=f32)

    for hh in range(H_A):
        @pl.when(h == hh)
        def _(hh=hh):
            for r in range(ppb):
                update(k_refs[r][0, :, hh, :], v_refs[r][0, :, hh, :], b_refs[r][0, 0])

    @pl.when(s == pl.num_programs(2) - 1)
    def _():
        o_ref[0] = acc_ref[...] / l_ref[...]


def _moba_sample(q, k, v, cache_k, cache_v, page_table, rel_bias):
    bd, tn = q.shape[:2]
    n_pages = page_table.shape[1]
    past = n_pages * PAGE_SIZE
    assert tn == 1 and past % MOBA_BLOCK == 0 and MOBA_BLOCK % PAGE_SIZE == 0
    ppb = MOBA_BLOCK // PAGE_SIZE
    km_past = _paged_block_means(cache_k, page_table)
    km_new = k.astype(f32).reshape(bd, 1, H_A, DH_A) * (1.0 / MOBA_BLOCK)
    kmean = jnp.concatenate([km_past, km_new], axis=1).transpose(0, 2, 1, 3)
    qpos = past + jnp.arange(tn)
    idx, ok = _select_blocks(q.transpose(0, 2, 1, 3), kmean, qpos // MOBA_BLOCK)
    idx, ok = idx[:, :, 0], ok[:, :, 0]
    n_sel = idx.shape[-1]
    lpage = jnp.clip((idx[..., None] * ppb + jnp.arange(ppb)).reshape(bd, H_A, n_sel * ppb), 0, n_pages - 1)
    ppage = jnp.take_along_axis(page_table[:, None, :], lpage, axis=2).astype(i32)
    okp = ok.astype(i32)
    pos = jnp.arange(past)
    bias_pos = rel_bias.astype(f32)[_t5_bucket(qpos[0] - pos)].T.reshape(H_A, n_pages, 1, PAGE_SIZE)
    own_bias = jnp.broadcast_to(rel_bias.astype(f32)[_t5_bucket(jnp.zeros((), i32))][:, None, None], (H_A, 1, LANES))
    rep8 = lambda z: jnp.broadcast_to(z.reshape(bd, 1, C_A).astype(f32), (bd, SUBLANES, C_A))
    hspec = pl.BlockSpec((1, SUBLANES, DH_A), lambda b, h, s, pp, lp, okr: (b, 0, h))
    pages = [pl.BlockSpec((1, PAGE_SIZE, H_A, DH_A),
                          functools.partial(lambda b, h, s, pp, lp, okr, r: (pp[b, h, s * ppb + r], 0, 0, 0), r=r))
             for r in range(ppb)]
    biases = [pl.BlockSpec((1, 1, 1, PAGE_SIZE),
                           functools.partial(lambda b, h, s, pp, lp, okr, r: (h, lp[b, h, s * ppb + r], 0, 0), r=r))
              for r in range(ppb)]
    o = pl.pallas_call(
        functools.partial(_moba_decode_body, ppb=ppb),
        grid_spec=pltpu.PrefetchScalarGridSpec(
            num_scalar_prefetch=3, grid=(bd, H_A, n_sel),
            in_specs=[hspec, hspec, hspec,
                      pl.BlockSpec((1, 1, LANES), lambda b, h, s, pp, lp, okr: (h, 0, 0))]
            + pages + pages + biases,
            out_specs=hspec,
            scratch_shapes=[pltpu.VMEM((SUBLANES, 1), f32), pltpu.VMEM((SUBLANES, 1), f32),
                            pltpu.VMEM((SUBLANES, DH_A), f32)]),
        out_shape=jax.ShapeDtypeStruct((bd, SUBLANES, C_A), f32),
        compiler_params=_cparams("parallel", "parallel", "arbitrary"),
        name="moba_decode",
    )(ppage, lpage.astype(i32), okp, rep8(q), rep8(k), rep8(v), own_bias,
      *([cache_k] * ppb + [cache_v] * ppb + [bias_pos] * ppb))
    return o[:, 0].reshape(bd, tn, H_A, DH_A)


def _sigmoid(x):
    return 1.0 / (1.0 + jnp.exp(-x))


def _dot3(a, b):
    ah = a.astype(bf16)
    al = (a - ah.astype(f32)).astype(bf16)
    bh = b.astype(bf16)
    bl = (b - bh.astype(f32)).astype(bf16)
    return (jnp.dot(ah, bh, preferred_element_type=f32) + jnp.dot(ah, bl, preferred_element_type=f32)
            + jnp.dot(al, bh, preferred_element_type=f32))


def _col_to_row(col, eye):
    return jnp.sum(jnp.where(eye, col, 0.0), axis=0, keepdims=True)


def _gdn_prep_body(q_ref, k_ref, v_ref, bg_ref, cq_ref, ck_ref, cv_ref, wq_ref, wk_ref, wv_ref, gp_ref,
                   uin_ref, wcum_ref, qdec_ref, ktail_ref, attn_ref, gtot_ref, pq_ref, pk_ref, pv_ref, *, t_len, nsub):
    hk = pl.program_id(1)
    c = pl.program_id(2)
    cs = GDN_CHUNK
    rep = HV_C // HK_C
    rows = nsub * cs

    @pl.when(c == 0)
    def _():
        pq_ref[...] = cq_ref[0]
        pk_ref[...] = ck_ref[0]
        pv_ref[...] = cv_ref[0]

    def conv_silu(x_ref, prev_ref, w_ref):
        cur = x_ref[...]
        ext = jnp.concatenate([prev_ref[...], cur], axis=0)
        acc = cur * w_ref[CONV_W - 1:CONV_W, :]
        for i in range(CONV_W - 1):
            lo = SUBLANES - (CONV_W - 1) + i
            acc = acc + ext[lo:lo + rows] * w_ref[i:i + 1, :]
        prev_ref[...] = cur[rows - SUBLANES:]
        return acc * _sigmoid(acc)

    def l2n(x):
        return x * lax.rsqrt(jnp.sum(x * x, axis=-1, keepdims=True) + 1e-6)

    q_all = l2n(conv_silu(q_ref, pq_ref, wq_ref)) * (DK_C ** -0.5)
    k_all = l2n(conv_silu(k_ref, pk_ref, wk_ref))
    v_all = conv_silu(v_ref, pv_ref, wv_ref)

    bg = bg_ref[...]
    beta_all = _sigmoid(bg)
    xg = bg + gp_ref[1:2, :]
    g_all = -jnp.exp(gp_ref[0:1, :]) * (jnp.maximum(xg, 0.0) + jnp.log1p(jnp.exp(-jnp.abs(xg))))
    lane = lax.broadcasted_iota(i32, bg.shape, 1)
    if t_len % cs:
        live = (c * rows + lax.broadcasted_iota(i32, (rows, 1), 0)) < t_len
    hsel = []
    for e in range(rep):
        hv = hk * rep + e
        beta = jnp.sum(jnp.where(lane == hv, beta_all, 0.0), axis=1, keepdims=True)
        g = jnp.sum(jnp.where(lane == hv + HV_C, g_all, 0.0), axis=1, keepdims=True)
        if t_len % cs:
            beta = jnp.where(live, beta, 0.0)
            g = jnp.where(live, g, 0.0)
        hsel.append((beta, g))
    ri = lax.broadcasted_iota(i32, (cs, cs), 0)
    ci = lax.broadcasted_iota(i32, (cs, cs), 1)
    eye = ri == ci
    tril = ci <= ri
    strict = ci < ri
    nt = (((1,), (1,)), ((), ()))
    results = []

    chains = []
    for sub in range(nsub):
        rs = slice(sub * cs, (sub + 1) * cs)
        q, k = q_all[rs], k_all[rs]
        kb16 = k.astype(bf16)
        kk = lax.dot_general(kb16, kb16, nt, preferred_element_type=f32)
        qk = lax.dot_general(q.astype(bf16), kb16, nt, preferred_element_type=f32)
        for e in range(rep):
            beta, g = hsel[e][0][rs], hsel[e][1][rs]
            gc = jnp.sum(jnp.where(tril, _col_to_row(g, eye), 0.0), axis=1, keepdims=True)
            gc_row = _col_to_row(gc, eye)
            decay = jnp.where(tril, jnp.exp(jnp.where(tril, gc - gc_row, 0.0)), 0.0)
            pw = jnp.where(strict, -(kk * beta * decay), 0.0)
            chains.append(dict(sub=sub, e=e, q=q, k=k, qk=qk, beta=beta, gc=gc, decay=decay, pw=pw,
                               inv=jnp.where(eye, 1.0, pw)))
    live = min(t_len, cs)
    for _ in range(max(math.ceil(math.log2(live)) - 1, 0)):
        for ch in chains:
            ch['pw'] = _dot3(ch['pw'], ch['pw'])
        for ch in chains:
            ch['inv'] = ch['inv'] + _dot3(ch['inv'], ch['pw'])
    for ch in chains:
        sub, e, q, k, beta, gc = ch['sub'], ch['e'], ch['q'], ch['k'], ch['beta'], ch['gc']
        e_col = jnp.exp(gc)
        v = v_all[sub * cs:(sub + 1) * cs, e * DV_C:(e + 1) * DV_C]
        rhs = jnp.concatenate([v * beta, k * (beta * e_col)], axis=1)
        sol = _dot3(ch['inv'], rhs)
        gl = jnp.sum(jnp.where(ri[:, 0:1] == cs - 1, gc, 0.0), axis=0, keepdims=True)
        results.append((sub, e, sol[:, :DV_C], sol[:, DV_C:], q * e_col, k * jnp.exp(gl - gc),
                        jnp.where(tril, ch['qk'] * ch['decay'], 0.0), jnp.broadcast_to(jnp.exp(gl), (cs, DV_C))))

    for sub, e, uin, wcum, qdec, ktail, attn, gtot in results:
        rs = slice(sub * cs, (sub + 1) * cs)
        sl = slice(e * DV_C, (e + 1) * DV_C)
        uin_ref[rs, sl] = uin
        wcum_ref[rs, sl] = wcum
        qdec_ref[rs, sl] = qdec
        ktail_ref[rs, sl] = ktail
        attn_ref[rs, e * cs:(e + 1) * cs] = attn
        gtot_ref[rs, sl] = gtot


def _gdn_scan_body(uin_ref, wcum_ref, qdec_ref, ktail_ref, attn_ref, gtot_ref, z_ref, ow_ref, s0_ref,
                   o_ref, sf_ref, s_ref, *, ng):
    c = pl.program_id(2)
    cs = GDN_CHUNK

    @pl.when(c == 0)
    def _():
        s_ref[...] = s0_ref[0]

    tn = (((0,), (0,)), ((), ()))
    sls = [slice(e * DV_C, (e + 1) * DV_C) for e in range(ng)]
    ss = [s_ref[e] for e in range(ng)]
    s16 = [s.astype(bf16) for s in ss]
    us = [uin_ref[:, sls[e]] - jnp.dot(wcum_ref[:, sls[e]].astype(bf16), s16[e], preferred_element_type=f32)
          for e in range(ng)]
    u16 = [u.astype(bf16) for u in us]
    os_ = [jnp.dot(qdec_ref[:, sls[e]].astype(bf16), s16[e], preferred_element_type=f32)
           + jnp.dot(attn_ref[:, e * cs:(e + 1) * cs].astype(bf16), u16[e], preferred_element_type=f32)
           for e in range(ng)]
    states = [ss[e] * jnp.concatenate([gtot_ref[:, sls[e]]] * (DK_C // cs), axis=0)
              + lax.dot_general(ktail_ref[:, sls[e]].astype(bf16), u16[e], tn, preferred_element_type=f32)
              for e in range(ng)]
    outs = []
    for e in range(ng):
        o = os_[e] * lax.rsqrt(jnp.mean(os_[e] * os_[e], axis=-1, keepdims=True) + RMS_EPS) * ow_ref[...]
        z = z_ref[:, sls[e]]
        outs.append(o * (z * _sigmoid(z)))
    for e in range(ng):
        s_ref[e] = states[e]
        o_ref[:, sls[e]] = outs[e]

    @pl.when(c == pl.num_programs(2) - 1)
    def _():
        sf_ref[0] = s_ref[...]


def _gdn_mixer(p, t_len, conv_buf, s0, prm):
    b, tp, wd = p.shape
    cs = GDN_CHUNK
    assert tp % cs == 0 and DK_C == DV_C and DK_C % cs == 0
    n = tp // cs
    rep = HV_C // HK_C
    p2 = p.reshape(b * tp, wd)
    cw = prm['conv_w'].astype(f32)
    cb = jnp.pad(conv_buf.astype(f32), ((0, 0), (SUBLANES - (CONV_W - 1), 0), (0, 0)))
    gp = jnp.zeros((2, LANES), f32).at[0, HV_C:2 * HV_C].set(prm['a_log'].astype(f32))
    gp = gp.at[1, HV_C:2 * HV_C].set(prm['dt_bias'].astype(f32))
    nq = QK_C // DK_C
    vw = rep * DV_C
    nsub = math.gcd(n, 8)
    rows = nsub * cs
    ns = n // nsub
    row = lambda bi, hk, c: bi * ns + c
    hv_out = jax.ShapeDtypeStruct((b * tp, V_C), f32)
    hspec = pl.BlockSpec((rows, vw), lambda bi, hk, c: (row(bi, hk, c), hk))
    uin, wcum, qdec, ktail, attn, gtot = pl.pallas_call(
        functools.partial(_gdn_prep_body, t_len=t_len, nsub=nsub),
        grid=(b, HK_C, ns),
        in_specs=[pl.BlockSpec((rows, DK_C), lambda bi, hk, c: (row(bi, hk, c), hk)),
                  pl.BlockSpec((rows, DK_C), lambda bi, hk, c: (row(bi, hk, c), nq + hk)),
                  pl.BlockSpec((rows, vw), lambda bi, hk, c: (row(bi, hk, c), 2 * QK_C // vw + hk)),
                  pl.BlockSpec((rows, LANES), lambda bi, hk, c: (row(bi, hk, c), (CONV_CH + V_C) // LANES)),
                  pl.BlockSpec((1, SUBLANES, DK_C), lambda bi, hk, c: (bi, 0, hk)),
                  pl.BlockSpec((1, SUBLANES, DK_C), lambda bi, hk, c: (bi, 0, nq + hk)),
                  pl.BlockSpec((1, SUBLANES, vw), lambda bi, hk, c: (bi, 0, 2 * QK_C // vw + hk)),
                  pl.BlockSpec((CONV_W, DK_C), lambda bi, hk, c: (0, hk)),
                  pl.BlockSpec((CONV_W, DK_C), lambda bi, hk, c: (0, nq + hk)),
                  pl.BlockSpec((CONV_W, vw), lambda bi, hk, c: (0, 2 * QK_C // vw + hk)),
                  pl.BlockSpec((2, LANES), lambda bi, hk, c: (0, 0))],
        out_specs=[hspec, hspec, hspec, hspec,
                   pl.BlockSpec((rows, rep * cs), lambda bi, hk, c: (row(bi, hk, c), hk)), hspec],
        out_shape=[hv_out, hv_out, hv_out, hv_out, jax.ShapeDtypeStruct((b * tp, HK_C * rep * cs), f32), hv_out],
        scratch_shapes=[pltpu.VMEM((SUBLANES, DK_C), f32), pltpu.VMEM((SUBLANES, DK_C), f32),
                        pltpu.VMEM((SUBLANES, vw), f32)],
        compiler_params=_cparams("parallel", "parallel", "arbitrary"),
        name="gdn_prep",
    )(p2, p2, p2, p2, cb, cb, cb, cw, cw, cw, gp)

    ng = 16
    gw = ng * DV_C
    gspec = pl.BlockSpec((cs, gw), lambda bi, hg, c: (bi * n + c, hg))
    sspec = pl.BlockSpec((1, ng, DK_C, DV_C), lambda bi, hg, c: (bi, hg, 0, 0))
    o, sf = pl.pallas_call(
        functools.partial(_gdn_scan_body, ng=ng),
        grid=(b, HV_C // ng, n),
        in_specs=[gspec, gspec, gspec, gspec,
                  pl.BlockSpec((cs, ng * cs), lambda bi, hg, c: (bi * n + c, hg)), gspec,
                  pl.BlockSpec((cs, gw), lambda bi, hg, c: (bi * n + c, CONV_CH // gw + hg)),
                  pl.BlockSpec((1, DV_C), lambda bi, hg, c: (0, 0)), sspec],
        out_specs=[gspec, sspec],
        out_shape=[hv_out, jax.ShapeDtypeStruct((b, HV_C, DK_C, DV_C), f32)],
        scratch_shapes=[pltpu.VMEM((ng, DK_C, DV_C), f32)],
        compiler_params=_cparams("parallel", "parallel", "arbitrary"),
        name="gdn_scan",
    )(uin, wcum, qdec, ktail, attn, gtot, p2, prm['onorm_w'].astype(f32).reshape(1, DV_C), s0.astype(f32))
    return o, sf


def _pad_rows(x, rows):
    return jnp.pad(x, ((0, rows - x.shape[0]), (0, 0)))


def kernel(x_prompt, x_sample, cache_k_0, cache_v_0, state_wkv_0, state_shift_0, state_gdn_1, state_conv_1, cache_k_2, cache_v_2, state_wkv_2, state_shift_2, state_gdn_3, state_conv_3, page_table, rel_bias, norm_mix_0, w_in_0, mu_0, w0_0, w2_0, a0_0, a2_0, g2_0, kk_scale_0, ka_mix_0, rk_bonus_0, lnx_w_0, lnx_b_0, w_out_0, norm_ffn_0, peer_wq_0, peer_keys_0, peer_u_0, peer_v_0, norm_mix_1, w_in_1, conv_w_1, a_log_1, dt_bias_1, onorm_w_1, w_out_1, norm_ffn_1, peer_wq_1, peer_keys_1, peer_u_1, peer_v_1, norm_mix_2, w_in_2, mu_2, w0_2, w2_2, a0_2, a2_2, v0_2, v2_2, g2_2, kk_scale_2, ka_mix_2, rk_bonus_2, lnx_w_2, lnx_b_2, w_out_2, norm_ffn_2, peer_wq_2, peer_keys_2, peer_u_2, peer_v_2, norm_mix_3, w_in_3, conv_w_3, a_log_3, dt_bias_3, onorm_w_3, w_out_3, norm_ffn_3, peer_wq_3, peer_keys_3, peer_u_3, peer_v_3, norm_final):
    mix_prm = [
        dict(norm=norm_mix_0, w_in=w_in_0, mu=mu_0, w0=w0_0, w2=w2_0, a0=a0_0, a2=a2_0, g2=g2_0,
             kk_scale=kk_scale_0, ka_mix=ka_mix_0, rk_bonus=rk_bonus_0, lnx_w=lnx_w_0, lnx_b=lnx_b_0,
             w_out=w_out_0),
        dict(norm=norm_mix_1, w_in=w_in_1, conv_w=conv_w_1, a_log=a_log_1, dt_bias=dt_bias_1,
             onorm_w=onorm_w_1, w_out=w_out_1),
        dict(norm=norm_mix_2, w_in=w_in_2, mu=mu_2, w0=w0_2, w2=w2_2, a0=a0_2, a2=a2_2, v0=v0_2, v2=v2_2,
             g2=g2_2, kk_scale=kk_scale_2, ka_mix=ka_mix_2, rk_bonus=rk_bonus_2, lnx_w=lnx_w_2,
             lnx_b=lnx_b_2, w_out=w_out_2),
        dict(norm=norm_mix_3, w_in=w_in_3, conv_w=conv_w_3, a_log=a_log_3, dt_bias=dt_bias_3,
             onorm_w=onorm_w_3, w_out=w_out_3),
    ]
    ffn_prm = [
        (norm_ffn_0, peer_wq_0, peer_keys_0, peer_u_0, peer_v_0),
        (norm_ffn_1, peer_wq_1, peer_keys_1, peer_u_1, peer_v_1),
        (norm_ffn_2, peer_wq_2, peer_keys_2, peer_u_2, peer_v_2),
        (norm_ffn_3, peer_wq_3, peer_keys_3, peer_u_3, peer_v_3),
    ]
    layer_state = [
        (cache_k_0, cache_v_0, state_wkv_0, state_shift_0),
        (state_gdn_1, state_conv_1),
        (cache_k_2, cache_v_2, state_wkv_2, state_shift_2),
        (state_gdn_3, state_conv_3),
    ]
    bp, sp, d = x_prompt.shape
    bs, ts, _ = x_sample.shape
    ns = bs * ts
    ns_mm = -(-ns // 16) * 16
    ns_peer = -(-ns // LANES) * LANES
    xp = x_prompt.reshape(bp * sp, d)
    xs = x_sample.reshape(ns, d)
    depth = len(mix_prm)
    vf_p = vf_s = None
    new = []
    for i in range(depth):
        prm = dict(mix_prm[i])
        prm['w_in'] = prm['w_in'].astype(bf16)
        prm['w_out'] = prm['w_out'].astype(bf16)
        xs_pad = _pad_rows(xs, ns_mm)
        pp = _matmul(xp, prm['w_in'], gain=prm['norm'])
        ps = _matmul(xs_pad, prm['w_in'], gain=prm['norm'])[:ns]
        if i % 2 == 0:
            ck, cv, wkv_in, shift_in = layer_state[i]
            first = i == 0
            rw = prm['w_in'].shape[1] - 3 * C_A
            o_a = _moba_prompt(pp, rel_bias, bp, sp)
            p_rw = pp[:, 3 * C_A:].reshape(bp, sp, rw)
            p_prev = jnp.concatenate([jnp.zeros((bp, 1, rw), f32), p_rw[:, :-1]], axis=1)
            o_b, wkv_p, vf_p = _rwkv7(p_rw, p_prev, jnp.zeros((bp, H_B, DH_B, DH_B), f32), vf_p, prm, first)
            cat = jnp.concatenate([o_a, o_b.reshape(bp * sp, C_B)], axis=-1)
            xp_new = _matmul(cat, prm['w_out'], res=xp)
            kp = pp[:, C_A:2 * C_A].reshape(bp, sp // PAGE_SIZE, PAGE_SIZE, H_A, DH_A)
            vp = pp[:, 2 * C_A:3 * C_A].reshape(bp, sp // PAGE_SIZE, PAGE_SIZE, H_A, DH_A)
            hp_last = _rms(xp.reshape(bp, sp, d)[:, -1], prm['norm'])
            qs = ps[:, :C_A].reshape(bs, ts, H_A, DH_A)
            ks = ps[:, C_A:2 * C_A].reshape(bs, ts, H_A, DH_A)
            vs = ps[:, 2 * C_A:3 * C_A].reshape(bs, ts, H_A, DH_A)
            o_as = _moba_sample(qs, ks, vs, ck, cv, page_table, rel_bias).reshape(ns, C_A)
            ps_rw = ps[:, 3 * C_A:].reshape(bs, ts, rw)
            prev0 = _matmul(_pad_rows(shift_in, ns_mm), prm['w_in'], col0=3 * C_A)[:bs]
            ps_prev = jnp.concatenate([prev0[:, None], ps_rw[:, :-1]], axis=1)
            o_bs, wkv_s, vf_s = _rwkv7(ps_rw, ps_prev, wkv_in, vf_s, prm, first)
            cat_s = jnp.concatenate([o_as, o_bs.reshape(ns, C_B)], axis=-1)
            xs_new = _matmul(_pad_rows(cat_s, ns_mm), prm['w_out'], res=xs_pad)[:ns]
            hs_last = _rms(xs.reshape(bs, ts, d)[:, -1], prm['norm'])
            new.append((kp, vp, ks, vs, wkv_p, wkv_s, hp_last, hs_last))
        else:
            gdn_in, conv_in = layer_state[i]
            wd = prm['w_in'].shape[1]

            def gdn(p3, conv_buf, s0):
                bb, tt, _ = p3.shape
                tpad = -(-tt // GDN_CHUNK) * GDN_CHUNK
                o, sf = _gdn_mixer(jnp.pad(p3, ((0, 0), (0, tpad - tt), (0, 0))), tt, conv_buf, s0, prm)
                conv = jnp.concatenate([conv_buf.astype(f32), p3[..., :CONV_CH]], axis=1)[:, -(CONV_W - 1):]
                return o.reshape(bb, tpad, V_C)[:, :tt].reshape(bb * tt, V_C), conv, sf

            o_p, conv_p, gdn_p = gdn(pp.reshape(bp, sp, wd), jnp.zeros((bp, CONV_W - 1, CONV_CH), f32),
                                     jnp.zeros((bp, HV_C, DK_C, DV_C), f32))
            o_s, conv_s, gdn_s = gdn(ps.reshape(bs, ts, wd), conv_in, gdn_in)
            xp_new = _matmul(o_p, prm['w_out'], res=xp)
            xs_new = _matmul(_pad_rows(o_s, ns_mm), prm['w_out'], res=xs_pad)[:ns]
            new.append((gdn_p, gdn_s, conv_p, conv_s))
        xp, xs = xp_new, xs_new
        g_ffn, wq, sub_keys, u_tab, v_tab = ffn_prm[i]
        wq_bf, u_bf, v_bf = wq.astype(bf16), u_tab, v_tab
        keys = sub_keys.astype(f32)
        xp = _peer(xp, g_ffn, wq_bf, keys, u_bf, v_bf)
        xs = _peer(_pad_rows(xs, ns_peer), g_ffn, wq_bf, keys, u_bf, v_bf)[:ns]
    y_prompt = _rms(xp, norm_final).reshape(bp, sp, d)
    y_sample = _rms(xs, norm_final).reshape(bs, ts, d)
    out = [y_prompt, y_sample]
    for layer in new:
        out.extend(layer)
    return tuple(out)
```

```python
import functools
import math

import jax
import jax.numpy as jnp
from jax import lax
from jax.experimental import pallas as pl
from jax.experimental.pallas import tpu as pltpu

f32 = jnp.float32
bf16 = jnp.bfloat16
i32 = jnp.int32

RMS_EPS = 1e-6
NEG_INF = -1e30

PAGE_SIZE = 128
H_A, DH_A = 8, 128
C_A = H_A * DH_A
MOBA_BLOCK, MOBA_TOPK = 256, 3
NUM_BUCKETS, MAX_DISTANCE = 32, 128
H_B, DH_B = 16, 64
C_B = H_B * DH_B
LORA_W, LORA_A, LORA_G, LORA_V = 64, 64, 160, 32
LNX_EPS = 64e-5
HK_C, HV_C, DK_C, DV_C = 16, 32, 128, 128
QK_C, V_C = HK_C * DK_C, HV_C * DV_C
CONV_W = 4
CONV_CH = 2 * QK_C + V_C
GDN_CHUNK = 64
PEER_HEADS, N_KEYS, PEER_DK, PEER_TOPK = 8, 128, 256, 16
N_EXPERTS = N_KEYS * N_KEYS

LANES = 128
SUBLANES = 8
VMEM_LIMIT = 56 * 1024 * 1024


def _cparams(*sem):
    return pltpu.CompilerParams(dimension_semantics=sem, vmem_limit_bytes=VMEM_LIMIT)


def _rms(x, g):
    return x * lax.rsqrt(jnp.mean(x * x, axis=-1, keepdims=True) + RMS_EPS) * g


def _mm_body(*refs, norm, has_res):
    if has_res:
        x_ref, g_ref, w_ref, res_ref, o_ref, h_ref = refs
    else:
        x_ref, g_ref, w_ref, o_ref, h_ref = refs

    @pl.when(pl.program_id(1) == 0)
    def _():
        x = x_ref[...]
        if norm:
            x = _rms(x, g_ref[...])
        h_ref[...] = x.astype(bf16)

    acc = jnp.dot(h_ref[...], w_ref[...], preferred_element_type=f32)
    if has_res:
        acc = res_ref[...] + acc
    o_ref[...] = acc


def _matmul(x, w, gain=None, res=None, col0=0, ncols=None, tn=512):
    m, k = x.shape
    ncols = w.shape[1] - col0 if ncols is None else ncols
    assert col0 % tn == 0 and m % 16 == 0 and w.dtype == bf16
    tm = min(m, 1024 if k <= 2048 else 512)
    assert m % tm == 0
    nj = pl.cdiv(ncols, tn)
    j0 = col0 // tn
    norm = gain is not None
    g = (gain if norm else jnp.ones((k,), f32)).reshape(1, k)
    in_specs = [pl.BlockSpec((tm, k), lambda i, j: (i, 0)),
                pl.BlockSpec((1, k), lambda i, j: (0, 0)),
                pl.BlockSpec((k, tn), lambda i, j: (0, j + j0))]
    args = [x, g, w]
    if res is not None:
        in_specs.append(pl.BlockSpec((tm, tn), lambda i, j: (i, j)))
        args.append(res)
    return pl.pallas_call(
        functools.partial(_mm_body, norm=norm, has_res=res is not None),
        grid=(m // tm, nj),
        in_specs=in_specs,
        out_specs=pl.BlockSpec((tm, tn), lambda i, j: (i, j)),
        out_shape=jax.ShapeDtypeStruct((m, ncols), f32),
        scratch_shapes=[pltpu.VMEM((tm, k), bf16)],
        compiler_params=_cparams("parallel", "arbitrary"),
        name="matmul",
    )(*args)


def _t5_bucket(dist):
    n = jnp.maximum(dist, 0)
    max_exact = NUM_BUCKETS // 2
    ratio = jnp.log(jnp.maximum(n, 1).astype(f32) / max_exact) / math.log(MAX_DISTANCE / max_exact)
    large = max_exact + (ratio * (NUM_BUCKETS - max_exact)).astype(i32)
    return jnp.where(n < max_exact, n, jnp.minimum(large, NUM_BUCKETS - 1))


def _moba_prompt_body(q_ref, k_ref, v_ref, bias_ref, o_ref, kmh_ref, kml_ref, kb_ref, vb_ref, *, nb, hp):
    qb = pl.program_id(2)
    blk = MOBA_BLOCK
    scale = DH_A ** -0.5
    nt = (((1,), (1,)), ((), ()))
    hs = [slice(e * DH_A, (e + 1) * DH_A) for e in range(hp)]
    nbr = -(-nb // SUBLANES) * SUBLANES

    @pl.when(qb == 0)
    def _():
        k = k_ref[...]
        kmh_ref[...] = jnp.zeros_like(kmh_ref)
        kml_ref[...] = jnp.zeros_like(kml_ref)
        for e in range(hp):
            km = jnp.mean(k[:, hs[e]].reshape(nb, blk, DH_A), axis=1)
            hi = km.astype(bf16)
            kmh_ref[e, 0:nb, :] = hi
            kml_ref[e, 0:nb, :] = (km - hi.astype(f32)).astype(bf16)
        kb_ref[...] = k.astype(bf16)
        vb_ref[...] = v_ref[...].astype(bf16)

    q = q_ref[...]
    q16 = [q[:, hs[e]].astype(bf16) for e in range(hp)]
    rowj = lax.broadcasted_iota(i32, (nbr, blk), 0)
    past = rowj < qb
    lhs = []
    for e in range(hp):
        qlo = (q[:, hs[e]] - q16[e].astype(f32)).astype(bf16)
        gate = (lax.dot_general(kmh_ref[e], q16[e], nt, preferred_element_type=f32)
                + lax.dot_general(kmh_ref[e], qlo, nt, preferred_element_type=f32)
                + lax.dot_general(kml_ref[e], q16[e], nt, preferred_element_type=f32))[0:nbr]
        g = jnp.where(past, gate, NEG_INF)
        sel = jnp.zeros(g.shape, f32)
        for _ in range(MOBA_TOPK):
            m = jnp.max(g, axis=0, keepdims=True)
            idx = jnp.min(jnp.where(g == m, rowj, nbr), axis=0, keepdims=True)
            pick = rowj == idx
            sel = jnp.where(pick, jnp.where(past, 1.0, 0.0), sel)
            g = jnp.where(pick, -jnp.inf, g)
        sel = jnp.where(rowj == qb, 1.0, sel)
        selneg = jnp.where(sel > 0.0, 0.0, NEG_INF / scale)
        selneg = jnp.concatenate([selneg, jnp.zeros((LANES - nbr, blk), f32)], axis=0).T
        lhs.append(jnp.concatenate([q16[e], selneg.astype(bf16)], axis=1))

    rowi = qb * blk + lax.broadcasted_iota(i32, (blk, blk), 0)
    coli = lax.broadcasted_iota(i32, (blk, blk), 1)
    lane = lax.broadcasted_iota(i32, (blk, LANES), 1)

    def tile(j, carry):
        ks = pl.multiple_of(j * blk, blk)
        boff = pl.multiple_of((nb - 1 - qb + j) * blk, blk)
        causal = coli + j * blk <= rowi
        onehot = jnp.where(lane == j, 1.0, 0.0).astype(bf16)
        ss = [lax.dot_general(lhs[e], jnp.concatenate([kb_ref[pl.ds(ks, blk), hs[e]], onehot], axis=1), nt,
                              preferred_element_type=f32) * scale + bias_ref[e, :, pl.ds(boff, blk)]
              for e in range(hp)]
        ss = [jnp.where(causal, s, NEG_INF) for s in ss]
        ms = [jnp.maximum(carry[e][0], jnp.max(ss[e], axis=1, keepdims=True)) for e in range(hp)]
        ps = [jnp.exp(ss[e] - ms[e]) for e in range(hp)]
        als = [jnp.exp(carry[e][0] - ms[e]) for e in range(hp)]
        return tuple((ms[e], als[e] * carry[e][1] + jnp.sum(ps[e], axis=1, keepdims=True),
                      als[e] * carry[e][2] + jnp.dot(ps[e].astype(bf16), vb_ref[pl.ds(ks, blk), hs[e]],
                                                     preferred_element_type=f32)) for e in range(hp))

    init = tuple((jnp.full((blk, 1), -jnp.inf, f32), jnp.zeros((blk, 1), f32), jnp.zeros((blk, DH_A), f32))
                 for _ in range(hp))
    res = lax.fori_loop(0, qb + 1, tile, init)
    o_ref[...] = jnp.concatenate([acc / l for _, l, acc in res], axis=1)


def _bias_strip_body(f_ref, o_ref, *, width):
    row = jnp.broadcast_to(f_ref[0], (MOBA_BLOCK, f_ref.shape[2]))
    o_ref[0] = pltpu.roll(row, 0, 1, stride=1, stride_axis=0)[:, :width]


def _bias_strip(rel_bias, nb):
    blk = MOBA_BLOCK
    width = (2 * nb - 1) * blk
    padded = width + blk
    x = jnp.arange(padded)
    dist = jnp.where(x < width, (nb - 1) * blk - x, (nb - 1) * blk + padded - x)
    prof = rel_bias.astype(f32)[_t5_bucket(dist)].T.reshape(H_A, 1, padded)
    return pl.pallas_call(
        functools.partial(_bias_strip_body, width=width),
        grid=(H_A,),
        in_specs=[pl.BlockSpec((1, 1, padded), lambda h: (h, 0, 0))],
        out_specs=pl.BlockSpec((1, blk, width), lambda h: (h, 0, 0)),
        out_shape=jax.ShapeDtypeStruct((H_A, blk, width), f32),
        compiler_params=_cparams("parallel"),
        name="moba_bias_strip",
    )(prof)


def _moba_prompt(p, rel_bias, b, s_len):
    blk = MOBA_BLOCK
    assert s_len % blk == 0
    nb = s_len // blk
    strip = _bias_strip(rel_bias, nb)
    hp = 2
    hw = hp * DH_A
    ng = C_A // hw
    return pl.pallas_call(
        functools.partial(_moba_prompt_body, nb=nb, hp=hp),
        grid=(b, ng, nb),
        in_specs=[pl.BlockSpec((blk, hw), lambda bi, h, qb: (bi * nb + qb, h)),
                  pl.BlockSpec((s_len, hw), lambda bi, h, qb: (bi, ng + h)),
                  pl.BlockSpec((s_len, hw), lambda bi, h, qb: (bi, 2 * ng + h)),
                  pl.BlockSpec((hp, blk, (2 * nb - 1) * blk), lambda bi, h, qb: (h, 0, 0))],
        out_specs=pl.BlockSpec((blk, hw), lambda bi, h, qb: (bi * nb + qb, h)),
        out_shape=jax.ShapeDtypeStruct((b * s_len, C_A), f32),
        scratch_shapes=[pltpu.VMEM((hp, LANES, DH_A), bf16), pltpu.VMEM((hp, LANES, DH_A), bf16),
                        pltpu.VMEM((s_len, hw), bf16), pltpu.VMEM((s_len, hw), bf16)],
        compiler_params=_cparams("parallel", "parallel", "arbitrary"),
        name="moba_prompt",
    )(p, p, p, strip)


def _tree_sum(xs):
    xs = list(xs)
    while len(xs) > 1:
        xs = [xs[i] + xs[i + 1] for i in range(0, len(xs) - 1, 2)] + ([xs[-1]] if len(xs) % 2 else [])
    return xs[0]


def _wkv_body(r_ref, w_ref, k_ref, kk_ref, b_ref, v_ref, s0_ref, y_ref, sf_ref, s_ref, *, tc):
    @pl.when(pl.program_id(1) == 0)
    def _():
        s_ref[...] = s0_ref[0]

    half = LANES // 2
    ni, hn = s_ref.shape[0], s_ref.shape[1]
    pair = 4

    def step(t, c):
        def rows(ref):
            return [ref[0, t, pl.ds(j, 1), :] for j in range(hn)]

        r, w, k, kk, b = rows(r_ref), rows(w_ref), rows(k_ref), rows(kk_ref), rows(b_ref)
        for i0 in range(0, ni, pair):
            its = range(i0, i0 + pair)
            vs = [v_ref[0, t, pl.ds(it * SUBLANES, SUBLANES), :] for it in its]
            ss = [[s_ref[it, j] for j in range(hn)] for it in its]
            t1 = [_tree_sum(s[j] * kk[j] for j in range(hn)) for s in ss]
            sa = [-(x + pltpu.roll(x, half, 1)) for x in t1]
            ss = [[s[j] * w[j] + sa_ * b[j] + v_ * k[j] for j in range(hn)] for s, sa_, v_ in zip(ss, sa, vs)]
            t2 = [_tree_sum(s[j] * r[j] for j in range(hn)) for s in ss]
            ys = [x + pltpu.roll(x, half, 1) for x in t2]
            for it, s, y in zip(its, ss, ys):
                for j in range(hn):
                    s_ref[it, j] = s[j]
                y_ref[0, t, pl.ds(it * SUBLANES, SUBLANES), :] = y
        return c

    lax.fori_loop(0, tc, step, 0)

    @pl.when(pl.program_id(1) == pl.num_programs(1) - 1)
    def _():
        sf_ref[0] = s_ref[...]


def _wkv_scan(r, w, k, kk, b, v, s0):
    bsz, t, h, n = r.shape
    assert n == DH_B and (bsz * h) % (LANES // 2) == 0 and n % SUBLANES == 0
    grp = bsz * h // (LANES // 2)
    bg = bsz // grp
    hn = n // 2
    ni = n // SUBLANES

    def vec(z):
        z = z.reshape(grp, bg, t, h, 2, hn)
        return z.transpose(0, 2, 5, 4, 1, 3).reshape(grp, t, hn, LANES)

    def row(z):
        z = z.reshape(grp, bg, t, h, n).transpose(0, 2, 4, 1, 3).reshape(grp, t, n, LANES // 2)
        return jnp.concatenate([z, z], axis=-1)

    s0k = s0.astype(f32).reshape(grp, bg, h, ni, SUBLANES, 2, hn).transpose(0, 3, 6, 4, 5, 1, 2)
    s0k = s0k.reshape(grp, ni, hn, SUBLANES, LANES)
    tc = math.gcd(t, 64)
    vspec = pl.BlockSpec((1, tc, hn, LANES), lambda g, c: (g, c, 0, 0))
    rspec = pl.BlockSpec((1, tc, n, LANES), lambda g, c: (g, c, 0, 0))
    sspec = pl.BlockSpec((1, ni, hn, SUBLANES, LANES), lambda g, c: (g, 0, 0, 0, 0))
    y, sf = pl.pallas_call(
        functools.partial(_wkv_body, tc=tc),
        grid=(grp, t // tc),
        in_specs=[vspec] * 5 + [rspec, sspec],
        out_specs=[rspec, sspec],
        out_shape=[jax.ShapeDtypeStruct((grp, t, n, LANES), f32),
                   jax.ShapeDtypeStruct((grp, ni, hn, SUBLANES, LANES), f32)],
        scratch_shapes=[pltpu.VMEM((ni, hn, SUBLANES, LANES), f32)],
        compiler_params=_cparams("parallel", "arbitrary"),
        name="wkv_scan",
    )(vec(r), vec(w), vec(k), vec(kk), vec(b), row(v), s0k)
    y = y[..., :LANES // 2].reshape(grp, t, n, bg, h).transpose(0, 3, 1, 4, 2).reshape(bsz, t, h, n)
    sf = sf.reshape(grp, ni, hn, SUBLANES, 2, bg, h).transpose(0, 5, 6, 1, 3, 4, 2).reshape(bsz, h, n, n)
    return y, sf


def _top_rows(vs, k, payloads=None):
    rows = lax.broadcasted_iota(i32, vs[0].shape, 0)
    n = vs[0].shape[0]
    vs = list(vs)
    vals = [[] for _ in vs]
    outs = [[] for _ in vs]
    for _ in range(k):
        ms = [jnp.max(v, axis=0, keepdims=True) for v in vs]
        ids = [jnp.min(jnp.where(v == m, rows, n), axis=0, keepdims=True) for v, m in zip(vs, ms)]
        picks = [rows == i for i in ids]
        vs = [jnp.where(p, -jnp.inf, v) for p, v in zip(picks, vs)]
        for j in range(len(vs)):
            vals[j].append(ms[j])
            outs[j].append(ids[j].astype(f32) if payloads is None else
                           jnp.sum(jnp.where(picks[j], payloads[j], 0.0), axis=0, keepdims=True))
    return [(jnp.concatenate(v, axis=0), jnp.concatenate(o, axis=0)) for v, o in zip(vals, outs)]


_PAIR_CNT = [PEER_TOPK // (m + 1) for m in range(PEER_TOPK)]
_PAIR_PAD = -sum(_PAIR_CNT) % SUBLANES


def _route_body(x_ref, g_ref, wq_ref, keys_ref, xn_ref, i1_ref, i2_ref, gate_ref, q_scr, e_scr, w_scr, *, tt):
    xn = _rms(x_ref[...], g_ref[...]).astype(bf16)
    xn_ref[...] = xn
    q_scr[...] = jnp.dot(xn, wq_ref[...], preferred_element_type=f32)
    hw = PEER_DK // 2
    nt = (((1,), (1,)), ((), ()))

    def head(h, c):
        row = pl.multiple_of(h * PEER_TOPK, PEER_TOPK)
        nsub = tt // LANES
        sts = []
        for sub in range(nsub):
            for p in range(2):
                col = pl.multiple_of(h * PEER_DK + p * hw, hw)
                qp = q_scr[sub * LANES:(sub + 1) * LANES, pl.ds(col, hw)]
                sts.append(lax.dot_general(keys_ref[p], qp, nt, preferred_element_type=f32))
        tops = _top_rows(sts, PEER_TOPK)
        cands, ceids = [], []
        for sub in range(nsub):
            (sv1, si1), (sv2, si2) = tops[2 * sub], tops[2 * sub + 1]
            si1 = si1 * float(N_KEYS)
            cands.append(jnp.concatenate([sv1[m:m + 1] + sv2[:_PAIR_CNT[m]] for m in range(PEER_TOPK)]
                                         + [jnp.full((_PAIR_PAD, LANES), -jnp.inf, f32)], axis=0))
            ceids.append(jnp.concatenate([si1[m:m + 1] + si2[:_PAIR_CNT[m]] for m in range(PEER_TOPK)]
                                         + [jnp.zeros((_PAIR_PAD, LANES), f32)], axis=0))
        joint = _top_rows(cands, PEER_TOPK, payloads=ceids)
        for sub in range(nsub):
            cv, eid = joint[sub]
            ex = jnp.exp(cv - cv[0:1])
            e_scr[pl.ds(row, PEER_TOPK), sub * LANES:(sub + 1) * LANES] = eid
            w_scr[pl.ds(row, PEER_TOPK), sub * LANES:(sub + 1) * LANES] = ex / jnp.sum(ex, axis=0, keepdims=True)
        return c

    lax.fori_loop(0, PEER_HEADS, head, 0)
    eid = e_scr[...].T
    i1 = jnp.floor(eid * (1.0 / N_KEYS))
    i1_ref[...] = i1
    i2_ref[...] = eid - i1 * float(N_KEYS)
    gate_ref[...] = w_scr[...].T


def _peer_route(x, gain, wq_bf, keys):
    t, d = x.shape
    tt = min(t, 256)
    assert t % tt == 0 and tt % LANES == 0
    nsel = PEER_HEADS * PEER_TOPK
    sel = jax.ShapeDtypeStruct((t, nsel), f32)
    sspec = pl.BlockSpec((tt, nsel), lambda i: (i, 0))
    return pl.pallas_call(
        functools.partial(_route_body, tt=tt),
        grid=(t // tt,),
        in_specs=[pl.BlockSpec((tt, d), lambda i: (i, 0)),
                  pl.BlockSpec((1, d), lambda i: (0, 0)),
                  pl.BlockSpec(wq_bf.shape, lambda i: (0, 0)),
                  pl.BlockSpec(keys.shape, lambda i: (0, 0, 0))],
        out_specs=[pl.BlockSpec((tt, d), lambda i: (i, 0)), sspec, sspec, sspec],
        out_shape=[jax.ShapeDtypeStruct((t, d), bf16), sel, sel, sel],
        scratch_shapes=[pltpu.VMEM((tt, wq_bf.shape[1]), f32), pltpu.VMEM((nsel, tt), f32),
                        pltpu.VMEM((nsel, tt), f32)],
        compiler_params=_cparams("parallel"),
        name="peer_route",
    )(x, gain.reshape(1, d), wq_bf, keys)


def _gate_build_body(i1_ref, i2_ref, g_ref, o_ref, *, tb):
    keyrow = lax.broadcasted_iota(i32, (N_KEYS, i1_ref.shape[1]), 0).astype(f32).astype(bf16)
    one, zero = jnp.ones((), bf16), jnp.zeros((), bf16)
    nt = (((1,), (1,)), ((), ()))

    ngrp = 2

    def toks(i, c):
        t0 = pl.multiple_of(i * (ngrp * SUBLANES), ngrp * SUBLANES)
        i1 = i1_ref[pl.ds(t0, ngrp * SUBLANES), :]
        i2 = i2_ref[pl.ds(t0, ngrp * SUBLANES), :]
        g = g_ref[pl.ds(t0, ngrp * SUBLANES), :]
        ghi = g.astype(bf16).astype(f32)
        glo = g - ghi
        ops = []
        for r in range(ngrp * SUBLANES):
            a = jnp.where(keyrow == i1[r:r + 1].astype(bf16), one, zero)
            m2 = keyrow == i2[r:r + 1].astype(bf16)
            bhi = jnp.where(m2, ghi[r:r + 1].astype(bf16), zero)
            blo = jnp.where(m2, glo[r:r + 1].astype(bf16), zero)
            ops.append((jnp.concatenate([a, a], axis=1), jnp.concatenate([bhi, blo], axis=1)))
        gs = [lax.dot_general(a2, b2, nt, preferred_element_type=f32) for a2, b2 in ops]
        for q in range(ngrp):
            o_ref[i * ngrp + q] = jnp.swapaxes(jnp.stack(gs[q * SUBLANES:(q + 1) * SUBLANES], axis=0), 0, 1)
        return c

    lax.fori_loop(0, tb // (ngrp * SUBLANES), toks, 0)


def _gate_build(i1, i2, gate):
    t, nsel = i1.shape
    tb = min(t, 128)
    assert t % tb == 0 and tb % (2 * SUBLANES) == 0
    spec = pl.BlockSpec((tb, nsel), lambda i: (i, 0))
    return pl.pallas_call(
        functools.partial(_gate_build_body, tb=tb),
        grid=(t // tb,),
        in_specs=[spec, spec, spec],
        out_specs=pl.BlockSpec((tb // SUBLANES, N_KEYS, SUBLANES, N_KEYS), lambda i: (i, 0, 0, 0)),
        out_shape=jax.ShapeDtypeStruct((t // SUBLANES, N_KEYS, SUBLANES, N_KEYS), f32),
        compiler_params=_cparams("parallel"),
        name="peer_gate_build",
    )(i1, i2, gate)


def _peer_mix_body(x_ref, g_ref, u_ref, v_ref, o_ref, *, na, tt):
    j = pl.program_id(1)

    @pl.when(j == 0)
    def _():
        o_ref[...] = jnp.zeros_like(o_ref)

    h = lax.dot_general(x_ref[...], u_ref[...].astype(bf16), (((1,), (1,)), ((), ())),
                        preferred_element_type=f32)
    parts = []
    for a in range(na):
        ha = h[:, a * N_KEYS:(a + 1) * N_KEYS]
        act = 0.5 * ha * (1.0 + lax.erf(ha * (2.0 ** -0.5)))
        parts.append((g_ref[:, a].reshape(tt, N_KEYS) * act).astype(bf16))
    p = jnp.concatenate(parts, axis=1)
    o_ref[...] += jnp.dot(p, v_ref[...].astype(bf16), preferred_element_type=f32)


def _peer_mix(xn, g4, u_bf, v_bf):
    t, d = xn.shape
    tt = min(t, 1024)
    na = 4
    et = na * N_KEYS
    assert t % tt == 0 and tt % SUBLANES == 0
    return pl.pallas_call(
        functools.partial(_peer_mix_body, na=na, tt=tt),
        grid=(t // tt, N_EXPERTS // et),
        in_specs=[pl.BlockSpec((tt, d), lambda i, j: (i, 0)),
                  pl.BlockSpec((tt // SUBLANES, na, SUBLANES, N_KEYS), lambda i, j: (i, j, 0, 0)),
                  pl.BlockSpec((et, d), lambda i, j: (j, 0)),
                  pl.BlockSpec((et, d), lambda i, j: (j, 0))],
        out_specs=pl.BlockSpec((tt, d), lambda i, j: (i, 0)),
        out_shape=jax.ShapeDtypeStruct((t, d), f32),
        compiler_params=_cparams("parallel", "arbitrary"),
        name="peer_mix",
    )(xn, g4, u_bf, v_bf)


def _peer(x, gain, wq_bf, keys, u_bf, v_bf):
    xn, i1, i2, gate = _peer_route(x, gain, wq_bf, keys)
    return x + _peer_mix(xn, _gate_build(i1, i2, gate), u_bf, v_bf)


def _l2norm(x, eps):
    return x * lax.rsqrt(jnp.sum(x * x, axis=-1, keepdims=True) + eps)


def _rwkv7(p_cur, p_prev, s0, v_first, prm, first):
    b, t, _ = p_cur.shape
    m = (p_cur + (p_prev - p_cur) * prm['mu']).astype(f32)
    r, k, v = m[..., :C_B], m[..., C_B:2 * C_B], m[..., 2 * C_B:3 * C_B]
    o = 3 * C_B
    wl = m[..., o:o + LORA_W]
    o += LORA_W
    al = m[..., o:o + LORA_A]
    o += LORA_A
    gl = m[..., o:o + LORA_G]
    o += LORA_G
    w_log = -jax.nn.softplus(-(prm['w0'] + jnp.tanh(wl) @ prm['w2'])) - 0.5
    decay = jnp.exp(-jnp.exp(w_log.astype(f32)))
    a = jax.nn.sigmoid((prm['a0'] + al @ prm['a2']).astype(f32))
    g = (jax.nn.sigmoid(gl) @ prm['g2']).astype(f32)
    if first:
        v_first = v
    else:
        vl = m[..., o:o + LORA_V]
        v = v + (v_first - v) * jax.nn.sigmoid((prm['v0'] + vl @ prm['v2']).astype(f32))

    def hd(z):
        return z.reshape(b, t, H_B, DH_B)

    kk = _l2norm(hd(k * prm['kk_scale']), 1e-24)
    k = k * (1.0 + (a - 1.0) * prm['ka_mix'])
    r_h, k_h, v_h = hd(r), hd(k), hd(v)
    b_h = kk * hd(a)
    y, s_fin = _wkv_scan(r_h, hd(decay), k_h, kk, b_h, v_h, s0)
    mu = jnp.mean(y, axis=-1, keepdims=True)
    var = jnp.mean(jnp.square(y - mu), axis=-1, keepdims=True)
    y = ((y - mu) * lax.rsqrt(var + LNX_EPS)).reshape(b, t, C_B) * prm['lnx_w'] + prm['lnx_b']
    bonus = jnp.sum(r_h * k_h * prm['rk_bonus'], axis=-1, keepdims=True) * v_h
    y = (y + bonus.reshape(b, t, C_B)) * g
    return y, s_fin.astype(s0.dtype), v_first


def _select_blocks(q, kmean, qblk):
    n_sel = min(MOBA_TOPK, kmean.shape[2])
    gate = jnp.einsum('bhqd,bhjd->bhqj', q.astype(f32), kmean, precision=lax.Precision.HIGHEST)
    past = jnp.arange(kmean.shape[2])[None, :] < qblk[:, None]
    _, idx = lax.top_k(jnp.where(past, gate, NEG_INF), n_sel)
    return idx, idx < qblk[:, None]


def _kmean_body(pt_ref, *refs, ppb):
    o_ref = refs[-1]
    pages = refs[:-1]
    for g in range(len(pages) // ppb):
        acc = jnp.sum(pages[g * ppb][0], axis=0)
        for r in range(1, ppb):
            acc = acc + jnp.sum(pages[g * ppb + r][0], axis=0)
        o_ref[0, g] = acc * (1.0 / MOBA_BLOCK)


def _paged_block_means(cache, page_table):
    bd, n_pages = page_table.shape
    ppb = MOBA_BLOCK // PAGE_SIZE
    nbp = n_pages // ppb
    bps = math.gcd(nbp, 4)
    hd = cache.shape[2:]
    specs = [pl.BlockSpec((1, PAGE_SIZE) + hd,
                          functools.partial(lambda b, j, pt, r: (pt[b, j * (bps * ppb) + r], 0, 0, 0), r=r))
             for r in range(bps * ppb)]
    return pl.pallas_call(
        functools.partial(_kmean_body, ppb=ppb),
        grid_spec=pltpu.PrefetchScalarGridSpec(
            num_scalar_prefetch=1, grid=(bd, nbp // bps), in_specs=specs,
            out_specs=pl.BlockSpec((1, bps) + hd, lambda b, j, pt: (b, j, 0, 0))),
        out_shape=jax.ShapeDtypeStruct((bd, nbp) + hd, f32),
        compiler_params=_cparams("parallel", "arbitrary"),
        name="moba_block_means",
    )(page_table, *([cache] * (bps * ppb)))


def _moba_decode_body(pp_ref, lp_ref, ok_ref, q_ref, kn_ref, vn_ref, ob_ref, *refs, ppb):
    k_refs, v_refs, b_refs = refs[:ppb], refs[ppb:2 * ppb], refs[2 * ppb:3 * ppb]
    o_ref, m_ref, l_ref, acc_ref = refs[3 * ppb:]
    b, h, s = pl.program_id(0), pl.program_id(1), pl.program_id(2)
    scale = DH_A ** -0.5
    q = q_ref[0]

    @pl.when(s == 0)
    def _():
        m_ref[...] = jnp.sum(q * kn_ref[0], axis=1, keepdims=True) * scale + ob_ref[0][:, 0:1]
        l_ref[...] = jnp.ones_like(l_ref)
        acc_ref[...] = vn_ref[0]

    def update(kp, vp, bias):
        sc = lax.dot_general(q.astype(bf16), kp.astype(bf16), (((1,), (1,)), ((), ())),
                             preferred_element_type=f32) * scale + bias
        sc = jnp.where(ok_ref[b, h, s] != 0, sc, NEG_INF)
        m_old = m_ref[...]
        m_new = jnp.maximum(m_old, jnp.max(sc, axis=1, keepdims=True))
        alpha = jnp.exp(m_old - m_new)
        p = jnp.exp(sc - m_new)
        m_ref[...] = m_new
        l_ref[...] = alpha * l_ref[...] + jnp.sum(p, axis=1, keepdims=True)
        acc_ref[...] = alpha * acc_ref[...] + jnp.dot(p.astype(bf16), vp.astype(bf16), preferred_element_type=f32)

    for hh in range(H_A):
        @pl.when(h == hh)
        def _(hh=hh):
            for r in range(ppb):
                update(k_refs[r][0, :, hh, :], v_refs[r][0, :, hh, :], b_refs[r][0, 0])

    @pl.when(s == pl.num_programs(2) - 1)
    def _():
        o_ref[0] = acc_ref[...] / l_ref[...]


def _moba_sample(q, k, v, cache_k, cache_v, page_table, rel_bias):
    bd, tn = q.shape[:2]
    n_pages = page_table.shape[1]
    past = n_pages * PAGE_SIZE
    assert tn == 1 and past % MOBA_BLOCK == 0 and MOBA_BLOCK % PAGE_SIZE == 0
    ppb = MOBA_BLOCK // PAGE_SIZE
    km_past = _paged_block_means(cache_k, page_table)
    km_new = k.astype(f32).reshape(bd, 1, H_A, DH_A) * (1.0 / MOBA_BLOCK)
    kmean = jnp.concatenate([km_past, km_new], axis=1).transpose(0, 2, 1, 3)
    qpos = past + jnp.arange(tn)
    idx, ok = _select_blocks(q.transpose(0, 2, 1, 3), kmean, qpos // MOBA_BLOCK)
    idx, ok = idx[:, :, 0], ok[:, :, 0]
    n_sel = idx.shape[-1]
    lpage = jnp.clip((idx[..., None] * ppb + jnp.arange(ppb)).reshape(bd, H_A, n_sel * ppb), 0, n_pages - 1)
    ppage = jnp.take_along_axis(page_table[:, None, :], lpage, axis=2).astype(i32)
    okp = ok.astype(i32)
    pos = jnp.arange(past)
    bias_pos = rel_bias.astype(f32)[_t5_bucket(qpos[0] - pos)].T.reshape(H_A, n_pages, 1, PAGE_SIZE)
    own_bias = jnp.broadcast_to(rel_bias.astype(f32)[_t5_bucket(jnp.zeros((), i32))][:, None, None], (H_A, 1, LANES))
    rep8 = lambda z: jnp.broadcast_to(z.reshape(bd, 1, C_A).astype(f32), (bd, SUBLANES, C_A))
    hspec = pl.BlockSpec((1, SUBLANES, DH_A), lambda b, h, s, pp, lp, okr: (b, 0, h))
    pages = [pl.BlockSpec((1, PAGE_SIZE, H_A, DH_A),
                          functools.partial(lambda b, h, s, pp, lp, okr, r: (pp[b, h, s * ppb + r], 0, 0, 0), r=r))
             for r in range(ppb)]
    biases = [pl.BlockSpec((1, 1, 1, PAGE_SIZE),
                           functools.partial(lambda b, h, s, pp, lp, okr, r: (h, lp[b, h, s * ppb + r], 0, 0), r=r))
              for r in range(ppb)]
    o = pl.pallas_call(
        functools.partial(_moba_decode_body, ppb=ppb),
        grid_spec=pltpu.PrefetchScalarGridSpec(
            num_scalar_prefetch=3, grid=(bd, H_A, n_sel),
            in_specs=[hspec, hspec, hspec,
                      pl.BlockSpec((1, 1, LANES), lambda b, h, s, pp, lp, okr: (h, 0, 0))]
            + pages + pages + biases,
            out_specs=hspec,
            scratch_shapes=[pltpu.VMEM((SUBLANES, 1), f32), pltpu.VMEM((SUBLANES, 1), f32),
                            pltpu.VMEM((SUBLANES, DH_A), f32)]),
        out_shape=jax.ShapeDtypeStruct((bd, SUBLANES, C_A), f32),
        compiler_params=_cparams("parallel", "parallel", "arbitrary"),
        name="moba_decode",
    )(ppage, lpage.astype(i32), okp, rep8(q), rep8(k), rep8(v), own_bias,
      *([cache_k] * ppb + [cache_v] * ppb + [bias_pos] * ppb))
    return o[:, 0].reshape(bd, tn, H_A, DH_A)


def _sigmoid(x):
    return 1.0 / (1.0 + jnp.exp(-x))


def _dot3(a, b):
    ah = a.astype(bf16)
    al = (a - ah.astype(f32)).astype(bf16)
    bh = b.astype(bf16)
    bl = (b - bh.astype(f32)).astype(bf16)
    return (jnp.dot(ah, bh, preferred_element_type=f32) + jnp.dot(ah, bl, preferred_element_type=f32)
            + jnp.dot(al, bh, preferred_element_type=f32))


def _col_to_row(col, eye):
    return jnp.sum(jnp.where(eye, col, 0.0), axis=0, keepdims=True)


def _gdn_prep_body(q_ref, k_ref, v_ref, bg_ref, cq_ref, ck_ref, cv_ref, wq_ref, wk_ref, wv_ref, gp_ref,
                   uin_ref, wcum_ref, qdec_ref, ktail_ref, attn_ref, gtot_ref, pq_ref, pk_ref, pv_ref, *, t_len, nsub):
    hk = pl.program_id(1)
    c = pl.program_id(2)
    cs = GDN_CHUNK
    rep = HV_C // HK_C
    rows = nsub * cs

    @pl.when(c == 0)
    def _():
        pq_ref[...] = cq_ref[0]
        pk_ref[...] = ck_ref[0]
        pv_ref[...] = cv_ref[0]

    def conv_silu(x_ref, prev_ref, w_ref):
        cur = x_ref[...]
        ext = jnp.concatenate([prev_ref[...], cur], axis=0)
        acc = cur * w_ref[CONV_W - 1:CONV_W, :]
        for i in range(CONV_W - 1):
            lo = SUBLANES - (CONV_W - 1) + i
            acc = acc + ext[lo:lo + rows] * w_ref[i:i + 1, :]
        prev_ref[...] = cur[rows - SUBLANES:]
        return acc * _sigmoid(acc)

    def l2n(x):
        return x * lax.rsqrt(jnp.sum(x * x, axis=-1, keepdims=True) + 1e-6)

    q_all = l2n(conv_silu(q_ref, pq_ref, wq_ref)) * (DK_C ** -0.5)
    k_all = l2n(conv_silu(k_ref, pk_ref, wk_ref))
    v_all = conv_silu(v_ref, pv_ref, wv_ref)

    bg = bg_ref[...]
    beta_all = _sigmoid(bg)
    xg = bg + gp_ref[1:2, :]
    g_all = -jnp.exp(gp_ref[0:1, :]) * (jnp.maximum(xg, 0.0) + jnp.log1p(jnp.exp(-jnp.abs(xg))))
    lane = lax.broadcasted_iota(i32, bg.shape, 1)
    if t_len % cs:
        live = (c * rows + lax.broadcasted_iota(i32, (rows, 1), 0)) < t_len
    hsel = []
    for e in range(rep):
        hv = hk * rep + e
        beta = jnp.sum(jnp.where(lane == hv, beta_all, 0.0), axis=1, keepdims=True)
        g = jnp.sum(jnp.where(lane == hv + HV_C, g_all, 0.0), axis=1, keepdims=True)
        if t_len % cs:
            beta = jnp.where(live, beta, 0.0)
            g = jnp.where(live, g, 0.0)
        hsel.append((beta, g))
    ri = lax.broadcasted_iota(i32, (cs, cs), 0)
    ci = lax.broadcasted_iota(i32, (cs, cs), 1)
    eye = ri == ci
    tril = ci <= ri
    strict = ci < ri
    nt = (((1,), (1,)), ((), ()))
    results = []

    chains = []
    for sub in range(nsub):
        rs = slice(sub * cs, (sub + 1) * cs)
        q, k = q_all[rs], k_all[rs]
        kb16 = k.astype(bf16)
        kk = lax.dot_general(kb16, kb16, nt, preferred_element_type=f32)
        qk = lax.dot_general(q.astype(bf16), kb16, nt, preferred_element_type=f32)
        for e in range(rep):
            beta, g = hsel[e][0][rs], hsel[e][1][rs]
            gc = jnp.sum(jnp.where(tril, _col_to_row(g, eye), 0.0), axis=1, keepdims=True)
            gc_row = _col_to_row(gc, eye)
            decay = jnp.where(tril, jnp.exp(jnp.where(tril, gc - gc_row, 0.0)), 0.0)
            pw = jnp.where(strict, -(kk * beta * decay), 0.0)
            chains.append(dict(sub=sub, e=e, q=q, k=k, qk=qk, beta=beta, gc=gc, decay=decay, pw=pw,
                               inv=jnp.where(eye, 1.0, pw)))
    live = min(t_len, cs)
    for _ in range(max(math.ceil(math.log2(live)) - 1, 0)):
        for ch in chains:
            ch['pw'] = _dot3(ch['pw'], ch['pw'])
        for ch in chains:
            ch['inv'] = ch['inv'] + _dot3(ch['inv'], ch['pw'])
    for ch in chains:
        sub, e, q, k, beta, gc = ch['sub'], ch['e'], ch['q'], ch['k'], ch['beta'], ch['gc']
        e_col = jnp.exp(gc)
        v = v_all[sub * cs:(sub + 1) * cs, e * DV_C:(e + 1) * DV_C]
        rhs = jnp.concatenate([v * beta, k * (beta * e_col)], axis=1)
        sol = _dot3(ch['inv'], rhs)
        gl = jnp.sum(jnp.where(ri[:, 0:1] == cs - 1, gc, 0.0), axis=0, keepdims=True)
        results.append((sub, e, sol[:, :DV_C], sol[:, DV_C:], q * e_col, k * jnp.exp(gl - gc),
                        jnp.where(tril, ch['qk'] * ch['decay'], 0.0),
                        jnp.broadcast_to(jnp.exp(gl), (SUBLANES, DV_C))))

    for sub, e, uin, wcum, qdec, ktail, attn, gtot in results:
        rs = slice(sub * cs, (sub + 1) * cs)
        sl = slice(e * DV_C, (e + 1) * DV_C)
        uin_ref[rs, sl] = uin
        wcum_ref[rs, sl] = wcum
        qdec_ref[rs, sl] = qdec
        ktail_ref[rs, sl] = ktail
        attn_ref[rs, e * cs:(e + 1) * cs] = attn
        gtot_ref[sub * SUBLANES:(sub + 1) * SUBLANES, sl] = gtot


def _gdn_scan_body(uin_ref, wcum_ref, qdec_ref, ktail_ref, attn_ref, gtot_ref, z_ref, ow_ref, s0_ref,
                   o_ref, sf_ref, s_ref, *, ng):
    c = pl.program_id(2)
    cs = GDN_CHUNK

    @pl.when(c == 0)
    def _():
        s_ref[...] = s0_ref[0]

    tn = (((0,), (0,)), ((), ()))
    sls = [slice(e * DV_C, (e + 1) * DV_C) for e in range(ng)]
    ss = [s_ref[e] for e in range(ng)]
    s16 = [s.astype(bf16) for s in ss]
    us = [uin_ref[:, sls[e]] - jnp.dot(wcum_ref[:, sls[e]].astype(bf16), s16[e], preferred_element_type=f32)
          for e in range(ng)]
    u16 = [u.astype(bf16) for u in us]
    os_ = [jnp.dot(qdec_ref[:, sls[e]].astype(bf16), s16[e], preferred_element_type=f32)
           + jnp.dot(attn_ref[:, e * cs:(e + 1) * cs].astype(bf16), u16[e], preferred_element_type=f32)
           for e in range(ng)]
    states = [ss[e] * jnp.concatenate([gtot_ref[:, sls[e]]] * (DK_C // SUBLANES), axis=0)
              + lax.dot_general(ktail_ref[:, sls[e]].astype(bf16), u16[e], tn, preferred_element_type=f32)
              for e in range(ng)]
    outs = []
    for e in range(ng):
        o = os_[e] * lax.rsqrt(jnp.mean(os_[e] * os_[e], axis=-1, keepdims=True) + RMS_EPS) * ow_ref[...]
        z = z_ref[:, sls[e]]
        outs.append(o * (z * _sigmoid(z)))
    for e in range(ng):
        s_ref[e] = states[e]
        o_ref[:, sls[e]] = outs[e]

    @pl.when(c == pl.num_programs(2) - 1)
    def _():
        sf_ref[0] = s_ref[...]


def _gdn_mixer(p, t_len, conv_buf, s0, prm):
    b, tp, wd = p.shape
    cs = GDN_CHUNK
    assert tp % cs == 0 and DK_C == DV_C and DK_C % cs == 0
    n = tp // cs
    rep = HV_C // HK_C
    p2 = p.reshape(b * tp, wd)
    cw = prm['conv_w'].astype(f32)
    cb = jnp.pad(conv_buf.astype(f32), ((0, 0), (SUBLANES - (CONV_W - 1), 0), (0, 0)))
    gp = jnp.zeros((2, LANES), f32).at[0, HV_C:2 * HV_C].set(prm['a_log'].astype(f32))
    gp = gp.at[1, HV_C:2 * HV_C].set(prm['dt_bias'].astype(f32))
    nq = QK_C // DK_C
    vw = rep * DV_C
    nsub = math.gcd(n, 8)
    rows = nsub * cs
    ns = n // nsub
    row = lambda bi, hk, c: bi * ns + c
    hv_out = jax.ShapeDtypeStruct((b * tp, V_C), f32)
    hspec = pl.BlockSpec((rows, vw), lambda bi, hk, c: (row(bi, hk, c), hk))
    uin, wcum, qdec, ktail, attn, gtot = pl.pallas_call(
        functools.partial(_gdn_prep_body, t_len=t_len, nsub=nsub),
        grid=(b, HK_C, ns),
        in_specs=[pl.BlockSpec((rows, DK_C), lambda bi, hk, c: (row(bi, hk, c), hk)),
                  pl.BlockSpec((rows, DK_C), lambda bi, hk, c: (row(bi, hk, c), nq + hk)),
                  pl.BlockSpec((rows, vw), lambda bi, hk, c: (row(bi, hk, c), 2 * QK_C // vw + hk)),
                  pl.BlockSpec((rows, LANES), lambda bi, hk, c: (row(bi, hk, c), (CONV_CH + V_C) // LANES)),
                  pl.BlockSpec((1, SUBLANES, DK_C), lambda bi, hk, c: (bi, 0, hk)),
                  pl.BlockSpec((1, SUBLANES, DK_C), lambda bi, hk, c: (bi, 0, nq + hk)),
                  pl.BlockSpec((1, SUBLANES, vw), lambda bi, hk, c: (bi, 0, 2 * QK_C // vw + hk)),
                  pl.BlockSpec((CONV_W, DK_C), lambda bi, hk, c: (0, hk)),
                  pl.BlockSpec((CONV_W, DK_C), lambda bi, hk, c: (0, nq + hk)),
                  pl.BlockSpec((CONV_W, vw), lambda bi, hk, c: (0, 2 * QK_C // vw + hk)),
                  pl.BlockSpec((2, LANES), lambda bi, hk, c: (0, 0))],
        out_specs=[hspec, hspec, hspec, hspec,
                   pl.BlockSpec((rows, rep * cs), lambda bi, hk, c: (row(bi, hk, c), hk)),
                   pl.BlockSpec((nsub * SUBLANES, vw), lambda bi, hk, c: (row(bi, hk, c), hk))],
        out_shape=[hv_out, hv_out, hv_out, hv_out, jax.ShapeDtypeStruct((b * tp, HK_C * rep * cs), f32),
                   jax.ShapeDtypeStruct((b * n * SUBLANES, V_C), f32)],
        scratch_shapes=[pltpu.VMEM((SUBLANES, DK_C), f32), pltpu.VMEM((SUBLANES, DK_C), f32),
                        pltpu.VMEM((SUBLANES, vw), f32)],
        compiler_params=_cparams("parallel", "parallel", "arbitrary"),
        name="gdn_prep",
    )(p2, p2, p2, p2, cb, cb, cb, cw, cw, cw, gp)

    ng = 16
    gw = ng * DV_C
    gspec = pl.BlockSpec((cs, gw), lambda bi, hg, c: (bi * n + c, hg))
    sspec = pl.BlockSpec((1, ng, DK_C, DV_C), lambda bi, hg, c: (bi, hg, 0, 0))
    o, sf = pl.pallas_call(
        functools.partial(_gdn_scan_body, ng=ng),
        grid=(b, HV_C // ng, n),
        in_specs=[gspec, gspec, gspec, gspec,
                  pl.BlockSpec((cs, ng * cs), lambda bi, hg, c: (bi * n + c, hg)),
                  pl.BlockSpec((SUBLANES, gw), lambda bi, hg, c: (bi * n + c, hg)),
                  pl.BlockSpec((cs, gw), lambda bi, hg, c: (bi * n + c, CONV_CH // gw + hg)),
                  pl.BlockSpec((1, DV_C), lambda bi, hg, c: (0, 0)), sspec],
        out_specs=[gspec, sspec],
        out_shape=[hv_out, jax.ShapeDtypeStruct((b, HV_C, DK_C, DV_C), f32)],
        scratch_shapes=[pltpu.VMEM((ng, DK_C, DV_C), f32)],
        compiler_params=_cparams("parallel", "parallel", "arbitrary"),
        name="gdn_scan",
    )(uin, wcum, qdec, ktail, attn, gtot, p2, prm['onorm_w'].astype(f32).reshape(1, DV_C), s0.astype(f32))
    return o, sf


def _pad_rows(x, rows):
    return jnp.pad(x, ((0, rows - x.shape[0]), (0, 0)))


def kernel(x_prompt, x_sample, cache_k_0, cache_v_0, state_wkv_0, state_shift_0, state_gdn_1, state_conv_1, cache_k_2, cache_v_2, state_wkv_2, state_shift_2, state_gdn_3, state_conv_3, page_table, rel_bias, norm_mix_0, w_in_0, mu_0, w0_0, w2_0, a0_0, a2_0, g2_0, kk_scale_0, ka_mix_0, rk_bonus_0, lnx_w_0, lnx_b_0, w_out_0, norm_ffn_0, peer_wq_0, peer_keys_0, peer_u_0, peer_v_0, norm_mix_1, w_in_1, conv_w_1, a_log_1, dt_bias_1, onorm_w_1, w_out_1, norm_ffn_1, peer_wq_1, peer_keys_1, peer_u_1, peer_v_1, norm_mix_2, w_in_2, mu_2, w0_2, w2_2, a0_2, a2_2, v0_2, v2_2, g2_2, kk_scale_2, ka_mix_2, rk_bonus_2, lnx_w_2, lnx_b_2, w_out_2, norm_ffn_2, peer_wq_2, peer_keys_2, peer_u_2, peer_v_2, norm_mix_3, w_in_3, conv_w_3, a_log_3, dt_bias_3, onorm_w_3, w_out_3, norm_ffn_3, peer_wq_3, peer_keys_3, peer_u_3, peer_v_3, norm_final):
    mix_prm = [
        dict(norm=norm_mix_0, w_in=w_in_0, mu=mu_0, w0=w0_0, w2=w2_0, a0=a0_0, a2=a2_0, g2=g2_0,
             kk_scale=kk_scale_0, ka_mix=ka_mix_0, rk_bonus=rk_bonus_0, lnx_w=lnx_w_0, lnx_b=lnx_b_0,
             w_out=w_out_0),
        dict(norm=norm_mix_1, w_in=w_in_1, conv_w=conv_w_1, a_log=a_log_1, dt_bias=dt_bias_1,
             onorm_w=onorm_w_1, w_out=w_out_1),
        dict(norm=norm_mix_2, w_in=w_in_2, mu=mu_2, w0=w0_2, w2=w2_2, a0=a0_2, a2=a2_2, v0=v0_2, v2=v2_2,
             g2=g2_2, kk_scale=kk_scale_2, ka_mix=ka_mix_2, rk_bonus=rk_bonus_2, lnx_w=lnx_w_2,
             lnx_b=lnx_b_2, w_out=w_out_2),
        dict(norm=norm_mix_3, w_in=w_in_3, conv_w=conv_w_3, a_log=a_log_3, dt_bias=dt_bias_3,
             onorm_w=onorm_w_3, w_out=w_out_3),
    ]
    ffn_prm = [
        (norm_ffn_0, peer_wq_0, peer_keys_0, peer_u_0, peer_v_0),
        (norm_ffn_1, peer_wq_1, peer_keys_1, peer_u_1, peer_v_1),
        (norm_ffn_2, peer_wq_2, peer_keys_2, peer_u_2, peer_v_2),
        (norm_ffn_3, peer_wq_3, peer_keys_3, peer_u_3, peer_v_3),
    ]
    layer_state = [
        (cache_k_0, cache_v_0, state_wkv_0, state_shift_0),
        (state_gdn_1, state_conv_1),
        (cache_k_2, cache_v_2, state_wkv_2, state_shift_2),
        (state_gdn_3, state_conv_3),
    ]
    bp, sp, d = x_prompt.shape
    bs, ts, _ = x_sample.shape
    ns = bs * ts
    ns_mm = -(-ns // 16) * 16
    ns_peer = -(-ns // LANES) * LANES
    xp = x_prompt.reshape(bp * sp, d)
    xs = x_sample.reshape(ns, d)
    depth = len(mix_prm)
    vf_p = vf_s = None
    new = []
    for i in range(depth):
        prm = dict(mix_prm[i])
        prm['w_in'] = prm['w_in'].astype(bf16)
        prm['w_out'] = prm['w_out'].astype(bf16)
        xs_pad = _pad_rows(xs, ns_mm)
        pp = _matmul(xp, prm['w_in'], gain=prm['norm'])
        ps = _matmul(xs_pad, prm['w_in'], gain=prm['norm'])[:ns]
        if i % 2 == 0:
            ck, cv, wkv_in, shift_in = layer_state[i]
            first = i == 0
            rw = prm['w_in'].shape[1] - 3 * C_A
            o_a = _moba_prompt(pp, rel_bias, bp, sp)
            p_rw = pp[:, 3 * C_A:].reshape(bp, sp, rw)
            p_prev = jnp.concatenate([jnp.zeros((bp, 1, rw), f32), p_rw[:, :-1]], axis=1)
            o_b, wkv_p, vf_p = _rwkv7(p_rw, p_prev, jnp.zeros((bp, H_B, DH_B, DH_B), f32), vf_p, prm, first)
            cat = jnp.concatenate([o_a, o_b.reshape(bp * sp, C_B)], axis=-1)
            xp_new = _matmul(cat, prm['w_out'], res=xp)
            kp = pp[:, C_A:2 * C_A].reshape(bp, sp // PAGE_SIZE, PAGE_SIZE, H_A, DH_A)
            vp = pp[:, 2 * C_A:3 * C_A].reshape(bp, sp // PAGE_SIZE, PAGE_SIZE, H_A, DH_A)
            hp_last = _rms(xp.reshape(bp, sp, d)[:, -1], prm['norm'])
            qs = ps[:, :C_A].reshape(bs, ts, H_A, DH_A)
            ks = ps[:, C_A:2 * C_A].reshape(bs, ts, H_A, DH_A)
            vs = ps[:, 2 * C_A:3 * C_A].reshape(bs, ts, H_A, DH_A)
            o_as = _moba_sample(qs, ks, vs, ck, cv, page_table, rel_bias).reshape(ns, C_A)
            ps_rw = ps[:, 3 * C_A:].reshape(bs, ts, rw)
            prev0 = _matmul(_pad_rows(shift_in, ns_mm), prm['w_in'], col0=3 * C_A)[:bs]
            ps_prev = jnp.concatenate([prev0[:, None], ps_rw[:, :-1]], axis=1)
            o_bs, wkv_s, vf_s = _rwkv7(ps_rw, ps_prev, wkv_in, vf_s, prm, first)
            cat_s = jnp.concatenate([o_as, o_bs.reshape(ns, C_B)], axis=-1)
            xs_new = _matmul(_pad_rows(cat_s, ns_mm), prm['w_out'], res=xs_pad)[:ns]
            hs_last = _rms(xs.reshape(bs, ts, d)[:, -1], prm['norm'])
            new.append((kp, vp, ks, vs, wkv_p, wkv_s, hp_last, hs_last))
        else:
            gdn_in, conv_in = layer_state[i]
            wd = prm['w_in'].shape[1]

            def gdn(p3, conv_buf, s0):
                bb, tt, _ = p3.shape
                tpad = -(-tt // GDN_CHUNK) * GDN_CHUNK
                o, sf = _gdn_mixer(jnp.pad(p3, ((0, 0), (0, tpad - tt), (0, 0))), tt, conv_buf, s0, prm)
                conv = jnp.concatenate([conv_buf.astype(f32), p3[..., :CONV_CH]], axis=1)[:, -(CONV_W - 1):]
                return o.reshape(bb, tpad, V_C)[:, :tt].reshape(bb * tt, V_C), conv, sf

            o_p, conv_p, gdn_p = gdn(pp.reshape(bp, sp, wd), jnp.zeros((bp, CONV_W - 1, CONV_CH), f32),
                                     jnp.zeros((bp, HV_C, DK_C, DV_C), f32))
            o_s, conv_s, gdn_s = gdn(ps.reshape(bs, ts, wd), conv_in, gdn_in)
            xp_new = _matmul(o_p, prm['w_out'], res=xp)
            xs_new = _matmul(_pad_rows(o_s, ns_mm), prm['w_out'], res=xs_pad)[:ns]
            new.append((gdn_p, gdn_s, conv_p, conv_s))
        xp, xs = xp_new, xs_new
        g_ffn, wq, sub_keys, u_tab, v_tab = ffn_prm[i]
        wq_bf, u_bf, v_bf = wq.astype(bf16), u_tab, v_tab
        keys = sub_keys.astype(f32)
        xp = _peer(xp, g_ffn, wq_bf, keys, u_bf, v_bf)
        xs = _peer(_pad_rows(xs, ns_peer), g_ffn, wq_bf, keys, u_bf, v_bf)[:ns]
    y_prompt = _rms(xp, norm_final).reshape(bp, sp, d)
    y_sample = _rms(xs, norm_final).reshape(bs, ts, d)
    out = [y_prompt, y_sample]
    for layer in new:
        out.extend(layer)
    return tuple(out)
```
